```python
import math
import functools
import jax
import jax.numpy as jnp
from jax import lax
import numpy as np

D_MODEL = 1024
BATCH = 16
SEQ = 2048
DEPTH = 1
DEC_BATCH = 128
DEC_SEQ = 4
PAST_LEN = 8192
PAGE_SIZE = 128

H_GDN = 4
DK_GDN = 128
DV_GDN = 128
CONV_W = 4
CHUNK = 64
GDN_NORM_EPS = 1e-6
H_DIFF = 4
DH_DIFF = 64
Q_BLOCK = 128
DIFF_NORM_EPS = 1e-5
N_BUCKETS = 32
MAX_DISTANCE = 128
D_FF = 4 * D_MODEL
LN_EPS = 1e-5

GDN_QK = H_GDN * DK_GDN
GDN_V = H_GDN * DV_GDN
GDN_CONV_CH = 2 * GDN_QK + GDN_V
DIFF_QK = H_DIFF * 2 * DH_DIFF
DIFF_V = H_DIFF * 2 * DH_DIFF
IN_SPLITS = (GDN_CONV_CH, GDN_V, H_GDN, H_GDN, DIFF_QK, DIFF_QK, DIFF_V, D_MODEL, D_MODEL)
N_IN = sum(IN_SPLITS)

kernel_name = 'gdn_diffattn_hybrid_step'


def layer_norm(x, g=None, b=None, eps=LN_EPS):
    xf = x.astype(jnp.float32)
    mu = jnp.mean(xf, axis=-1, keepdims=True)
    var = jnp.mean(jnp.square(xf - mu), axis=-1, keepdims=True)
    y = (xf - mu) * lax.rsqrt(var + eps)
    if g is not None:
        y = y * g.astype(jnp.float32) + b.astype(jnp.float32)
    return y.astype(x.dtype)


def rms_norm(x, g, eps):
    xf = x.astype(jnp.float32)
    return xf * lax.rsqrt(jnp.mean(jnp.square(xf), axis=-1, keepdims=True) + eps) * g.astype(jnp.float32)


def l2_normalize(x, eps=1e-6):
    return x * lax.rsqrt(jnp.sum(jnp.square(x), axis=-1, keepdims=True) + eps)


def t5_bucket(dist):
    n = jnp.maximum(dist, 0)
    max_exact = N_BUCKETS // 2
    nf = jnp.maximum(n, 1).astype(jnp.float32)
    large = max_exact + (jnp.log(nf / max_exact) / math.log(MAX_DISTANCE / max_exact)
                         * (N_BUCKETS - max_exact)).astype(jnp.int32)
    large = jnp.minimum(large, N_BUCKETS - 1)
    return jnp.where(n < max_exact, n, large)


def short_conv(u, prev, w):
    T = u.shape[1]
    up = jnp.concatenate([prev, u], axis=1)
    y = sum(up[:, j:j + T] * w[j] for j in range(CONV_W))
    return jax.nn.silu(y), up[:, -(CONV_W - 1):]


def gated_delta_chunked(q, k, v, beta, g, s0):
    B, T = q.shape[:2]
    n_chunks = -(-T // CHUNK)
    pad = n_chunks * CHUNK - T

    def prep(a):
        a = jnp.pad(a, [(0, 0), (0, pad)] + [(0, 0)] * (a.ndim - 2))
        a = a.reshape((B, n_chunks, CHUNK) + a.shape[2:])
        return jnp.moveaxis(a, 3, 1)

    q, k, v, beta, g = (prep(a) for a in (q, k, v, beta, g))
    gc = jnp.cumsum(g, axis=-1)
    idx = jnp.arange(CHUNK)
    incl = idx[:, None] >= idx[None, :]
    strict = idx[:, None] > idx[None, :]
    diff = gc[..., :, None] - gc[..., None, :]
    gamma = jnp.where(incl, jnp.exp(jnp.where(incl, diff, 0.0)), 0.0)
    kk = jnp.einsum('bhncd,bhnjd->bhncj', k, k)
    m = jnp.where(strict, beta[..., :, None] * kk * gamma, 0.0) + jnp.eye(CHUNK, dtype=kk.dtype)
    solve = functools.partial(lax.linalg.triangular_solve, left_side=True, lower=True, unit_diagonal=True)
    u = solve(m, v * beta[..., None])
    w = solve(m, k * (beta * jnp.exp(gc))[..., None])
    qk = jnp.where(incl, jnp.einsum('bhncd,bhnjd->bhncj', q, k) * gamma, 0.0)
    q_dec = q * jnp.exp(gc)[..., None]
    g_last = gc[..., -1:]
    k_dec = k * jnp.exp(g_last - gc)[..., None]
    d_last = jnp.exp(g_last)[..., None]

    def step(s, xs):
        u_c, w_c, qk_c, qd_c, kd_c, dl_c = xs
        v_new = u_c - jnp.einsum('bhcd,bhde->bhce', w_c, s)
        o = jnp.einsum('bhcd,bhde->bhce', qd_c, s) + jnp.einsum('bhcj,bhje->bhce', qk_c, v_new)
        s = dl_c * s + jnp.einsum('bhcd,bhce->bhde', kd_c, v_new)
        return s, o

    xs = tuple(jnp.moveaxis(a, 2, 0) for a in (u, w, qk, q_dec, k_dec, d_last))
    s_final, o = lax.scan(step, s0, xs)
    o = jnp.moveaxis(jnp.moveaxis(o, 0, 2), 1, 3)
    o = o.reshape(B, n_chunks * CHUNK, o.shape[3], o.shape[4])[:, :T]
    return o, s_final


def diff_attention(q, k, v, k_past, v_past, lam, rel_bias):
    f32 = jnp.float32
    B, T = q.shape[:2]
    P = 0 if k_past is None else k_past.shape[1]
    qb = min(Q_BLOCK, T)
    nb = -(-T // qb)
    q_blocks = jnp.pad(q, ((0, 0), (0, nb * qb - T), (0, 0), (0, 0), (0, 0)))
    q_blocks = jnp.moveaxis(q_blocks.reshape(B, nb, qb, H_DIFF, 2, DH_DIFF), 1, 0)
    q_pos = (P + jnp.arange(nb * qb, dtype=jnp.int32)).reshape(nb, qb)
    k_pos = P + jnp.arange(T, dtype=jnp.int32)
    scale = DH_DIFF ** -0.5

    def logits(q_blk, keys, rel):
        s = jnp.einsum('bqhcd,bkhcd->bhcqk', q_blk, keys, preferred_element_type=f32) * scale
        bias = jnp.moveaxis(rel_bias[t5_bucket(rel)], -1, 0).astype(f32)
        return s + bias[None, :, None]

    def block(args):
        q_blk, qp = args
        rel = qp[:, None] - k_pos[None, :]
        s = jnp.where(rel >= 0, logits(q_blk, k, rel), -jnp.inf)
        if k_past is not None:
            rel_p = qp[:, None] - jnp.arange(P, dtype=jnp.int32)[None, :]
            s = jnp.concatenate([logits(q_blk, k_past, rel_p), s], axis=-1)
        p = jax.nn.softmax(s, axis=-1)
        a = p[:, :, 0] - lam * p[:, :, 1]
        o = jnp.einsum('bhqk,bkhe->bqhe', a[..., P:].astype(v.dtype), v, preferred_element_type=f32)
        if k_past is not None:
            o = o + jnp.einsum('bhqk,bkhe->bqhe', a[..., :P].astype(v.dtype), v_past, preferred_element_type=f32)
        return o

    o = lax.map(block, (q_blocks, q_pos))
    return jnp.moveaxis(o, 0, 1).reshape(B, nb * qb, H_DIFF, 2 * DH_DIFF)[:, :T]


def token_mix(h, conv_prev, s0, k_past, v_past, lam_init, rel_bias, w_in, w_conv, a_log, dt_bias, gdn_norm_g,
              lam_q1, lam_k1, lam_q2, lam_k2, diff_norm_g, w_br_gdn, w_br_diff, w_o):
    f32 = jnp.float32
    B, T, _ = h.shape
    proj = h @ w_in
    split_at = [int(i) for i in np.cumsum(IN_SPLITS)[:-1]]
    conv_in, z_g, b_g, a_g, q_d, k_d, v_d, gate_gdn, gate_diff = jnp.split(proj, split_at, axis=-1)
    qkv, conv_state = short_conv(conv_in, conv_prev, w_conv)
    q_g, k_g, v_g = jnp.split(qkv.astype(f32), [GDN_QK, 2 * GDN_QK], axis=-1)
    q_g = l2_normalize(q_g.reshape(B, T, H_GDN, DK_GDN)) * DK_GDN ** -0.5
    k_g = l2_normalize(k_g.reshape(B, T, H_GDN, DK_GDN))
    v_g = v_g.reshape(B, T, H_GDN, DV_GDN)
    beta = jax.nn.sigmoid(b_g.astype(f32))
    g = -jnp.exp(a_log.astype(f32)) * jax.nn.softplus(a_g.astype(f32) + dt_bias.astype(f32))
    o_g, s_new = gated_delta_chunked(q_g, k_g, v_g, beta, g, s0.astype(f32))
    o_g = rms_norm(o_g, gdn_norm_g, GDN_NORM_EPS) * jax.nn.silu(z_g.astype(f32).reshape(B, T, H_GDN, DV_GDN))
    o_g = o_g.reshape(B, T, GDN_V).astype(h.dtype)
    q_d = q_d.reshape(B, T, H_DIFF, 2, DH_DIFF)
    k_d = k_d.reshape(B, T, H_DIFF, 2, DH_DIFF)
    v_d = v_d.reshape(B, T, H_DIFF, 2 * DH_DIFF)
    lam = (jnp.exp(jnp.sum(lam_q1.astype(f32) * lam_k1.astype(f32)))
           - jnp.exp(jnp.sum(lam_q2.astype(f32) * lam_k2.astype(f32))) + lam_init)
    o_d = diff_attention(q_d, k_d, v_d, k_past, v_past, lam, rel_bias)
    o_d = (rms_norm(o_d, diff_norm_g, DIFF_NORM_EPS) * (1.0 - lam_init)).reshape(B, T, DIFF_V).astype(h.dtype)
    m = jax.nn.sigmoid(gate_gdn) * (o_g @ w_br_gdn) + jax.nn.sigmoid(gate_diff) * (o_d @ w_br_diff)
    return m @ w_o, conv_state, s_new, k_d, v_d


def decoder_layer(x, c, conv_prev, s0, k_past, v_past, lam_init, rel_bias, w_in, w_conv, a_log, dt_bias,
                  gdn_norm_g, lam_q1, lam_k1, lam_q2, lam_k2, diff_norm_g, w_br_gdn, w_br_diff, w_o,
                  w_ada, b_ada, ln1_g, ln1_b, ln2_g, ln2_b, w_up, b_up, w_down, b_down):
    alpha = (2.0 * DEPTH) ** 0.25
    ada = jax.nn.silu(c) @ w_ada + b_ada
    sh1, sc1, gt1, sh2, sc2, gt2 = jnp.split(ada[:, None, :], 6, axis=-1)
    h = layer_norm(x) * (1 + sc1) + sh1
    mix, conv_state, s_new, k_new, v_new = token_mix(
        h, conv_prev, s0, k_past, v_past, lam_init, rel_bias, w_in, w_conv, a_log, dt_bias, gdn_norm_g,
        lam_q1, lam_k1, lam_q2, lam_k2, diff_norm_g, w_br_gdn, w_br_diff, w_o)
    x = layer_norm(alpha * x + gt1 * mix, ln1_g, ln1_b)
    h = layer_norm(x) * (1 + sc2) + sh2
    f = jnp.square(jax.nn.relu(h @ w_up + b_up)) @ w_down + b_down
    x = layer_norm(alpha * x + gt2 * f, ln2_g, ln2_b)
    return x, conv_state, s_new, k_new, v_new


def setup_inputs(seed: int = 0) -> dict:
    key = jax.random.key(seed)
    ks = iter(jax.random.split(key, 48))
    f32 = jnp.float32

    def nrm(shape, scale):
        return jax.random.normal(next(ks), shape, f32) * scale

    dn_beta = (8.0 * DEPTH) ** -0.25
    n_pages = PAST_LEN // PAGE_SIZE
    n_used = DEC_BATCH * n_pages
    n_pool = n_used + (n_used + 3) // 4
    page_table = jax.random.permutation(next(ks), n_pool)[:n_used].reshape(DEC_BATCH, n_pages).astype(jnp.int32)
    off_vd = GDN_CONV_CH + GDN_V + 2 * H_GDN + 2 * DIFF_QK
    col_scale = jnp.ones((N_IN,), f32).at[2 * GDN_QK:GDN_CONV_CH].set(dn_beta).at[off_vd:off_vd + DIFF_V].set(dn_beta)
    dt = jnp.exp(jax.random.uniform(next(ks), (DEPTH, H_GDN), f32, math.log(1e-3), math.log(1e-1)))
    a_log = jnp.log(jax.random.uniform(next(ks), (DEPTH, H_GDN), f32, 1.0, 16.0))
    return {
        'x_prompt': nrm((BATCH, SEQ, D_MODEL), 1.0),
        'x_sample': nrm((DEC_BATCH, DEC_SEQ, D_MODEL), 1.0),
        'c_prompt': nrm((BATCH, D_MODEL), 1.0),
        'c_sample': nrm((DEC_BATCH, D_MODEL), 1.0),
        'cache_k': nrm((DEPTH, n_pool, PAGE_SIZE, H_DIFF, 2, DH_DIFF), 1.0),
        'cache_v': nrm((DEPTH, n_pool, PAGE_SIZE, H_DIFF, 2 * DH_DIFF), dn_beta),
        'page_table': page_table,
        'state_conv': nrm((DEPTH, DEC_BATCH, CONV_W - 1, GDN_CONV_CH), 1.0) * col_scale[:GDN_CONV_CH],
        'state_gdn': nrm((DEPTH, DEC_BATCH, H_GDN, DK_GDN, DV_GDN), 0.1),
        'rel_bias': nrm((N_BUCKETS, H_DIFF), 0.5),
        'w_in': nrm((DEPTH, D_MODEL, N_IN), D_MODEL ** -0.5) * col_scale,
        'w_conv': nrm((DEPTH, CONV_W, GDN_CONV_CH), CONV_W ** -0.5),
        'a_log': a_log,
        'dt_bias': dt + jnp.log(-jnp.expm1(-dt)),
        'gdn_norm_g': 1.0 + nrm((DEPTH, DV_GDN), 0.02),
        'lam_q1': nrm((DEPTH, DH_DIFF), 0.1),
        'lam_k1': nrm((DEPTH, DH_DIFF), 0.1),
        'lam_q2': nrm((DEPTH, DH_DIFF), 0.1),
        'lam_k2': nrm((DEPTH, DH_DIFF), 0.1),
        'diff_norm_g': 1.0 + nrm((DEPTH, 2 * DH_DIFF), 0.02),
        'w_br_gdn': nrm((DEPTH, GDN_V, D_MODEL), GDN_V ** -0.5),
        'w_br_diff': nrm((DEPTH, DIFF_V, D_MODEL), DIFF_V ** -0.5),
        'w_o': nrm((DEPTH, D_MODEL, D_MODEL), dn_beta * D_MODEL ** -0.5),
        'w_ada': nrm((DEPTH, D_MODEL, 6 * D_MODEL), D_MODEL ** -0.5),
        'b_ada': nrm((DEPTH, 6 * D_MODEL), 0.02),
        'ln1_g': 1.0 + nrm((DEPTH, D_MODEL), 0.02),
        'ln1_b': nrm((DEPTH, D_MODEL), 0.02),
        'ln2_g': 1.0 + nrm((DEPTH, D_MODEL), 0.02),
        'ln2_b': nrm((DEPTH, D_MODEL), 0.02),
        'w_up': nrm((DEPTH, D_MODEL, D_FF), D_MODEL ** -0.5),
        'b_up': nrm((DEPTH, D_FF), 0.02),
        'w_down': nrm((DEPTH, D_FF, D_MODEL), dn_beta * D_FF ** -0.5),
        'b_down': nrm((DEPTH, D_MODEL), 0.02),
    }


def reference(x_prompt, x_sample, c_prompt, c_sample, cache_k, cache_v, page_table, state_conv, state_gdn,
              rel_bias, w_in, w_conv, a_log, dt_bias, gdn_norm_g, lam_q1, lam_k1, lam_q2, lam_k2, diff_norm_g,
              w_br_gdn, w_br_diff, w_o, w_ada, b_ada, ln1_g, ln1_b, ln2_g, ln2_b, w_up, b_up, w_down, b_down):
    B = x_prompt.shape[0]
    Bs = x_sample.shape[0]
    yp, ys = x_prompt, x_sample
    kp, vp, cp, sp, ksm, vsm, csm, ssm = [], [], [], [], [], [], [], []
    for l in range(DEPTH):
        lam_init = 0.8 - 0.6 * math.exp(-0.3 * l)
        lw = (w_in[l], w_conv[l], a_log[l], dt_bias[l], gdn_norm_g[l], lam_q1[l], lam_k1[l], lam_q2[l], lam_k2[l],
              diff_norm_g[l], w_br_gdn[l], w_br_diff[l], w_o[l], w_ada[l], b_ada[l], ln1_g[l], ln1_b[l],
              ln2_g[l], ln2_b[l], w_up[l], b_up[l], w_down[l], b_down[l])
        conv0 = jnp.zeros((B, CONV_W - 1, GDN_CONV_CH), x_prompt.dtype)
        s0 = jnp.zeros((B, H_GDN, DK_GDN, DV_GDN), jnp.float32)
        yp, c1, s1, k1, v1 = decoder_layer(yp, c_prompt, conv0, s0, None, None, lam_init, rel_bias, *lw)
        k_past = cache_k[l][page_table].reshape((Bs, -1) + cache_k.shape[3:])
        v_past = cache_v[l][page_table].reshape((Bs, -1) + cache_v.shape[3:])
        ys, c2, s2, k2, v2 = decoder_layer(ys, c_sample, state_conv[l], state_gdn[l], k_past, v_past,
                                           lam_init, rel_bias, *lw)
        kp.append(k1); vp.append(v1); cp.append(c1); sp.append(s1)
        ksm.append(k2); vsm.append(v2); csm.append(c2); ssm.append(s2)
    return (yp, ys, jnp.stack(kp), jnp.stack(vp), jnp.stack(cp), jnp.stack(sp),
            jnp.stack(ksm), jnp.stack(vsm), jnp.stack(csm), jnp.stack(ssm))
```

```python
import functools
import math

import numpy as np
import jax
import jax.numpy as jnp
from jax import lax
from jax.experimental import pallas as pl
from jax.experimental.pallas import tpu as pltpu

F32 = jnp.float32
BF16 = jnp.bfloat16

D_MODEL = 1024
DEPTH = 1
PAGE_SIZE = 128
H_GDN = 4
DK_GDN = 128
DV_GDN = 128
CONV_W = 4
CHUNK = 64
GDN_NORM_EPS = 1e-6
H_DIFF = 4
DH_DIFF = 64
DIFF_NORM_EPS = 1e-5
N_BUCKETS = 32
MAX_DISTANCE = 128
D_FF = 4 * D_MODEL
LN_EPS = 1e-5
GDN_QK = H_GDN * DK_GDN
GDN_V = H_GDN * DV_GDN
GDN_CONV_CH = 2 * GDN_QK + GDN_V
DIFF_QK = H_DIFF * 2 * DH_DIFF
DIFF_V = H_DIFF * 2 * DH_DIFF
HEAD_W = 2 * DH_DIFF
IN_SPLITS = (GDN_CONV_CH, GDN_V, H_GDN, H_GDN, DIFF_QK, DIFF_QK, DIFF_V, D_MODEL, D_MODEL)
ALPHA = (2.0 * DEPTH) ** 0.25

LANES = 128
SUBLANES = 8
VMEM_LIMIT = 56 * 1024 * 1024

Q_TILE = 256
PAGES_PER_STEP = 8
MASKED = N_BUCKETS

NN = (((1,), (0,)), ((), ()))
NT = (((1,), (1,)), ((), ()))
TN = (((0,), (0,)), ((), ()))


def _cparams(sem):
    return pltpu.CompilerParams(dimension_semantics=sem, vmem_limit_bytes=VMEM_LIMIT)


def _const_spec(shape):
    nd = len(shape)
    return pl.BlockSpec(shape, lambda *_: (0,) * nd, pipeline_mode=pl.Buffered(1))


def _sigmoid(x):
    return 1.0 / (1.0 + jnp.exp(-x))


def _silu(x):
    return x * _sigmoid(x)


def _softplus(x):
    return jnp.maximum(x, 0.0) + jnp.log(1.0 + jnp.exp(-jnp.abs(x)))


def _ln(x):
    mu = jnp.mean(x, axis=-1, keepdims=True)
    xc = x - mu
    var = jnp.mean(xc * xc, axis=-1, keepdims=True)
    return xc * lax.rsqrt(var + LN_EPS)


def _dot(a, b, dims=NN):
    return lax.dot_general(a, b, dims, preferred_element_type=F32)


def _dot_bf(a, b, dims=NN):
    return _dot(a.astype(BF16), b.astype(BF16), dims)


def _split(a):
    hi = a.astype(BF16)
    lo = (a - hi.astype(F32)).astype(BF16)
    return hi, lo


def _dot3(a, b, dims=NN):
    ah, al = _split(a)
    bh, bl = _split(b)
    return _dot(ah, bh, dims) + (_dot(ah, bl, dims) + _dot(al, bh, dims))


def _dot_f32(a, b, dims=NN):
    return lax.dot_general(a, b, dims, precision=lax.Precision.HIGHEST, preferred_element_type=F32)


def _mod(mod_ref, i, per_token):
    return mod_ref[i] if per_token else mod_ref[i:i + 1, :]


def _ada_body(c_ref, w_ref, b_ref, o_ref):
    s = _silu(c_ref[...])
    o_ref[...] = _dot(s.astype(BF16), w_ref[...]) + b_ref[...]


def _ada(c_all, w_ada, b_ada):
    n = c_all.shape[0]
    return pl.pallas_call(
        _ada_body,
        grid=(6,),
        in_specs=[pl.BlockSpec((n, D_MODEL), lambda j: (0, 0)),
                  pl.BlockSpec((D_MODEL, D_MODEL), lambda j: (0, j)),
                  pl.BlockSpec((1, D_MODEL), lambda j: (0, j))],
        out_specs=pl.BlockSpec((n, D_MODEL), lambda j: (0, j)),
        out_shape=jax.ShapeDtypeStruct((n, 6 * D_MODEL), F32),
        compiler_params=_cparams(("arbitrary",)),
        name="ada",
    )(c_all, w_ada, b_ada)


def _t5_bucket_np(rel):
    n = np.maximum(rel, 0)
    max_exact = N_BUCKETS // 2
    nf = np.maximum(n, 1).astype(np.float32)
    large = max_exact + (np.log(nf / np.float32(max_exact)) / np.float32(math.log(MAX_DISTANCE / max_exact))
                         * np.float32(N_BUCKETS - max_exact)).astype(np.int32)
    large = np.minimum(large, N_BUCKETS - 1)
    return np.where(n < max_exact, n, large).astype(np.int32)


def _bias_codes(past_len, dec_seq):
    r = np.arange(Q_TILE)[:, None]
    c = np.arange(Q_TILE)[None, :]
    prev_tile = _t5_bucket_np(Q_TILE + r - c)
    diag_tile = np.where(r >= c, _t5_bucket_np(r - c), MASKED)
    prompt = np.concatenate([prev_tile, diag_tile], axis=1).astype(np.int32)
    nrow = dec_seq * 2 * H_DIFF
    qrow = (np.arange(nrow) // (2 * H_DIFF))[:, None]
    nkeys = PAGES_PER_STEP * PAGE_SIZE
    kpos = past_len - nkeys + np.arange(nkeys)[None, :]
    last = _t5_bucket_np(past_len + qrow - kpos).astype(np.int32)
    j = np.arange(SUBLANES)[None, :]
    new = np.where((j <= qrow) & (j < dec_seq), _t5_bucket_np(qrow - j), MASKED).astype(np.int32)
    return prompt, last, new


def _bias_body(tbl_ref, cp_ref, cl_ref, cn_ref, bp_ref, bl_ref, bn_ref, far_ref):
    def lookup(codes, h):
        acc = jnp.full(codes.shape, -jnp.inf, F32)
        for b in range(N_BUCKETS):
            acc = jnp.where(codes == b, tbl_ref[b, h], acc)
        return acc

    cp = cp_ref[...]
    for h in range(H_DIFF):
        bp_ref[h] = lookup(cp, h)
    cl = cl_ref[...]
    cn = cn_ref[...]
    nrow = cl.shape[0]
    row_head = (lax.broadcasted_iota(jnp.int32, (nrow, 1), 0) % (2 * H_DIFF)) // 2
    bl = jnp.zeros(cl.shape, F32)
    bn = jnp.zeros(cn.shape, F32)
    far = jnp.zeros((nrow, LANES), F32)
    for h in range(H_DIFF):
        bl = jnp.where(row_head == h, lookup(cl, h), bl)
        bn = jnp.where(row_head == h, lookup(cn, h), bn)
        far = jnp.where(row_head == h, tbl_ref[N_BUCKETS - 1, h], far)
    bl_ref[...] = bl
    bn_ref[...] = bn
    far_ref[...] = far


def _bias_prep(rel_bias, past_len, dec_seq):
    cp, cl, cn = _bias_codes(past_len, dec_seq)
    nrow = cl.shape[0]
    vm = pl.BlockSpec(memory_space=pltpu.VMEM)
    return pl.pallas_call(
        _bias_body,
        in_specs=[pl.BlockSpec(memory_space=pltpu.SMEM), vm, vm, vm],
        out_specs=[vm, vm, vm, vm],
        out_shape=[jax.ShapeDtypeStruct((H_DIFF,) + cp.shape, F32),
                   jax.ShapeDtypeStruct(cl.shape, F32),
                   jax.ShapeDtypeStruct(cn.shape, F32),
                   jax.ShapeDtypeStruct((nrow, LANES), F32)],
        name="bias_prep",
    )(rel_bias, jnp.asarray(cp), jnp.asarray(cl), jnp.asarray(cn))


def _lam(lq1_ref, lk1_ref, lq2_ref, lk2_ref, lam_init):
    s1 = jnp.sum(lq1_ref[...] * lk1_ref[...], axis=-1, keepdims=True)
    s2 = jnp.sum(lq2_ref[...] * lk2_ref[...], axis=-1, keepdims=True)
    return jnp.exp(s1) - jnp.exp(s2) + lam_init


_SEG = {"conv": (0, 1536), "z": (1536, 2048), "q": (2048, 2560), "k": (2560, 3072), "v": (3072, 3584),
        "gg": (3584, 4608), "gd": (4608, 5632)}
N_MAIN = 5632


def _inproj_body(x_ref, mod_ref, wm_ref, ws_ref, conv_ref, z_ref, q_ref, k_ref, v_ref, gg_ref, gd_ref, ba_ref,
                 *, per_token):
    h = _ln(x_ref[...]) * (1.0 + _mod(mod_ref, 1, per_token)) + _mod(mod_ref, 0, per_token)
    hb = h.astype(BF16)

    def seg(name):
        a, b = _SEG[name]
        return _dot(hb, wm_ref[:, a:b])

    conv_ref[...] = seg("conv")
    z_ref[...] = seg("z").astype(BF16)
    q_ref[...] = seg("q").astype(BF16)
    k_ref[...] = seg("k")
    v_ref[...] = seg("v")
    gg_ref[...] = seg("gg").astype(BF16)
    gd_ref[...] = seg("gd").astype(BF16)
    ba_ref[...] = _dot(hb, ws_ref[...])


def _inproj(x, mods, w_main, w_small, tm, per_token):
    nb, t, _ = x.shape
    if per_token:
        mod_spec = pl.BlockSpec((6, tm, D_MODEL), lambda b, i: (0, i, 0))
    else:
        mod_spec = pl.BlockSpec((None, 6, D_MODEL), lambda b, i: (b, 0, 0))

    def out(n, dt):
        return pl.BlockSpec((None, tm, n), lambda b, i: (b, i, 0)), jax.ShapeDtypeStruct((nb, t, n), dt)

    outs = [out(GDN_CONV_CH, F32), out(GDN_V, BF16), out(DIFF_QK, BF16), out(DIFF_QK, F32), out(DIFF_V, F32),
            out(D_MODEL, BF16), out(D_MODEL, BF16), out(LANES, F32)]
    return pl.pallas_call(
        functools.partial(_inproj_body, per_token=per_token),
        grid=(nb, t // tm),
        in_specs=[pl.BlockSpec((None, tm, D_MODEL), lambda b, i: (b, i, 0)), mod_spec,
                  _const_spec((D_MODEL, N_MAIN)), _const_spec((D_MODEL, LANES))],
        out_specs=[o[0] for o in outs],
        out_shape=[o[1] for o in outs],
        compiler_params=_cparams(("arbitrary", "arbitrary")),
        name="inproj",
    )(x, mods, w_main, w_small)


def _tri_inv(a, c):
    ii = lax.broadcasted_iota(jnp.int32, (c, c), 0)
    jj = lax.broadcasted_iota(jnp.int32, (c, c), 1)
    eye = (ii == jj).astype(F32)
    a8 = jnp.where((ii >> 3) == (jj >> 3), a, 0.0)
    p = _dot3(a8, a8)
    x = eye - a8
    x = x + _dot3(x, p)
    p = _dot3(p, p)
    x = x + _dot3(x, p)
    sh = 3
    while (1 << sh) < c:
        lvl = ((ii >> (sh + 1)) == (jj >> (sh + 1))) & ((ii >> sh) != (jj >> sh))
        low = jnp.where(lvl, a, 0.0)
        x = x - _dot3(x, _dot3(low, x))
        sh += 1
    return x


def _gdn_chunk(qn, kn, v, beta, gc_col, gc_row, s, c):
    ii = lax.broadcasted_iota(jnp.int32, (c, c), 0)
    jj = lax.broadcasted_iota(jnp.int32, (c, c), 1)
    incl = ii >= jj
    gamma = jnp.where(incl, jnp.exp(jnp.where(incl, gc_col - gc_row, 0.0)), 0.0)
    kk = _dot3(kn, kn, NT)
    minv = _tri_inv(jnp.where(ii > jj, beta * kk * gamma, 0.0), c)
    egc = jnp.exp(gc_col)
    uw = _dot3(minv, jnp.concatenate([v * beta, kn * (beta * egc)], axis=1))
    u = uw[:, :DV_GDN]
    w = uw[:, DV_GDN:]
    qk = jnp.where(incl, _dot_bf(qn, kn, NT) * gamma, 0.0)
    g_last = gc_col[c - 1:c, :]
    q_dec = qn * egc
    k_dec = kn * jnp.exp(g_last - gc_col)
    sb = s.astype(BF16)
    v_new = u - _dot(w.astype(BF16), sb)
    o = _dot(q_dec.astype(BF16), sb) + _dot_bf(qk, v_new)
    s_new = jnp.exp(g_last) * s + _dot_bf(k_dec, v_new, TN)
    return o, s_new


def _l2n(x):
    return x * lax.rsqrt(jnp.sum(x * x, axis=-1, keepdims=True) + 1e-6)


def _gdn_gates(ba, alog_ref, dtb_ref):
    beta = _sigmoid(ba)
    g = -jnp.exp(alog_ref[...]) * _softplus(ba + dtb_ref[...])
    return beta, g


def _gdn_out(o, z, gng):
    o = o * lax.rsqrt(jnp.mean(o * o, axis=-1, keepdims=True) + GDN_NORM_EPS) * gng
    return o * _silu(z)


def _cumsum_mats(c):
    ii = lax.broadcasted_iota(jnp.int32, (c, c), 0)
    jj = lax.broadcasted_iota(jnp.int32, (c, c), 1)
    return (ii >= jj).astype(F32)


def _gdn_prompt_body(conv_ref, z_ref, ba_ref, wc_ref, alog_ref, dtb_ref, gng_ref, og_ref, s_ref, cbuf_ref, *, tt):
    t = pl.program_id(1)

    @pl.when(t == 0)
    def _():
        s_ref[...] = jnp.zeros(s_ref.shape, F32)
        cbuf_ref[0:SUBLANES, :] = jnp.zeros((SUBLANES, GDN_CONV_CH), F32)

    u_in = conv_ref[...]
    cbuf_ref[SUBLANES:, :] = u_in
    y = cbuf_ref[SUBLANES:, :] * wc_ref[CONV_W - 1:CONV_W, :]
    for j in range(CONV_W - 1):
        off = SUBLANES - (CONV_W - 1) + j
        y = y + cbuf_ref[off:off + tt, :] * wc_ref[j:j + 1, :]
    cbuf_ref[0:SUBLANES, :] = u_in[tt - SUBLANES:, :]
    qkv = _silu(y)

    beta_all, g_all = _gdn_gates(ba_ref[...], alog_ref, dtb_ref)
    ltri = _cumsum_mats(CHUNK)
    nch = tt // CHUNK
    gc_all = jnp.concatenate([_dot_f32(ltri, g_all[c * CHUNK:(c + 1) * CHUNK, :]) for c in range(nch)], axis=0)
    gc_t = gc_all.T
    gng = gng_ref[...]
    for h in range(H_GDN):
        lo = h * DK_GDN
        qn = _l2n(qkv[:, lo:lo + DK_GDN]) * (DK_GDN ** -0.5)
        kn = _l2n(qkv[:, GDN_QK + lo:GDN_QK + lo + DK_GDN])
        vv = qkv[:, 2 * GDN_QK + lo:2 * GDN_QK + lo + DV_GDN]
        s = s_ref[h]
        for c in range(nch):
            r0 = c * CHUNK
            o, s = _gdn_chunk(qn[r0:r0 + CHUNK], kn[r0:r0 + CHUNK], vv[r0:r0 + CHUNK],
                              beta_all[r0:r0 + CHUNK, h:h + 1], gc_all[r0:r0 + CHUNK, H_GDN + h:H_GDN + h + 1],
                              gc_t[H_GDN + h:H_GDN + h + 1, r0:r0 + CHUNK], s, CHUNK)
            z = z_ref[r0:r0 + CHUNK, lo:lo + DV_GDN].astype(F32)
            og_ref[r0:r0 + CHUNK, lo:lo + DV_GDN] = _gdn_out(o, z, gng).astype(og_ref.dtype)
        s_ref[h] = s


def _gdn_prompt(conv_in, z, ba, w_conv, alog_pad, dtb_pad, gng, tt):
    nb, t, _ = conv_in.shape
    row = lambda n: pl.BlockSpec((None, tt, n), lambda b, i: (b, i, 0))
    return pl.pallas_call(
        functools.partial(_gdn_prompt_body, tt=tt),
        grid=(nb, t // tt),
        in_specs=[row(GDN_CONV_CH), row(GDN_V), row(LANES),
                  _const_spec((CONV_W, GDN_CONV_CH)), _const_spec((1, LANES)), _const_spec((1, LANES)),
                  _const_spec((1, DV_GDN))],
        out_specs=[row(GDN_V), pl.BlockSpec((None, H_GDN, DK_GDN, DV_GDN), lambda b, i: (b, 0, 0, 0))],
        out_shape=[jax.ShapeDtypeStruct((nb, t, GDN_V), BF16),
                   jax.ShapeDtypeStruct((nb, H_GDN, DK_GDN, DV_GDN), F32)],
        scratch_shapes=[pltpu.VMEM((tt + SUBLANES, GDN_CONV_CH), F32)],
        compiler_params=_cparams(("arbitrary", "arbitrary")),
        name="gdn_prompt",
    )(conv_in, z, ba, w_conv, alog_pad, dtb_pad, gng)


def _gdn_sample_body(upx_ref, z_ref, ba_ref, s0_ref, wc_ref, alog_ref, dtb_ref, gng_ref, og_ref, s_ref,
                     *, nb, n_valid):
    c = SUBLANES
    valid = lax.broadcasted_iota(jnp.int32, (c, 1), 0) < n_valid
    ltri = _cumsum_mats(c)
    gng = gng_ref[...]

    def body(n, carry):
        y = upx_ref[n, pl.ds(SUBLANES, c), :] * wc_ref[CONV_W - 1:CONV_W, :]
        for j in range(CONV_W - 1):
            off = SUBLANES - (CONV_W - 1) + j
            y = y + upx_ref[n, pl.ds(off, c), :] * wc_ref[j:j + 1, :]
        qkv = _silu(y)
        beta_all, g_all = _gdn_gates(ba_ref[n], alog_ref, dtb_ref)
        beta_all = jnp.where(valid, beta_all, 0.0)
        g_all = jnp.where(valid, g_all, 0.0)
        gc_all = _dot_f32(ltri, g_all)
        gc_t = jnp.concatenate([gc_all, jnp.zeros((LANES - c, LANES), F32)], axis=0).T
        for h in range(H_GDN):
            lo = h * DK_GDN
            qn = _l2n(qkv[:, lo:lo + DK_GDN]) * (DK_GDN ** -0.5)
            kn = jnp.where(valid, _l2n(qkv[:, GDN_QK + lo:GDN_QK + lo + DK_GDN]), 0.0)
            vv = jnp.where(valid, qkv[:, 2 * GDN_QK + lo:2 * GDN_QK + lo + DV_GDN], 0.0)
            o, s = _gdn_chunk(qn, kn, vv, beta_all[:, h:h + 1], gc_all[:, H_GDN + h:H_GDN + h + 1],
                              gc_t[H_GDN + h:H_GDN + h + 1, 0:c], s0_ref[n, h], c)
            s_ref[n, h] = s
            og_ref[n, :, lo:lo + DV_GDN] = _gdn_out(o, z_ref[n, :, lo:lo + DV_GDN], gng)
        return carry

    lax.fori_loop(0, nb, body, 0)


def _gdn_sample(upx, z8, ba8, s0, w_conv, alog_pad, dtb_pad, gng, n_valid, nb):
    n = upx.shape[0]
    blk = lambda *shape: pl.BlockSpec((nb,) + shape, lambda i: (i,) + (0,) * len(shape))
    return pl.pallas_call(
        functools.partial(_gdn_sample_body, nb=nb, n_valid=n_valid),
        grid=(n // nb,),
        in_specs=[blk(2 * SUBLANES, GDN_CONV_CH), blk(SUBLANES, GDN_V), blk(SUBLANES, LANES),
                  blk(H_GDN, DK_GDN, DV_GDN),
                  _const_spec((CONV_W, GDN_CONV_CH)), _const_spec((1, LANES)), _const_spec((1, LANES)),
                  _const_spec((1, DV_GDN))],
        out_specs=[blk(SUBLANES, GDN_V), blk(H_GDN, DK_GDN, DV_GDN)],
        out_shape=[jax.ShapeDtypeStruct((n, SUBLANES, GDN_V), F32),
                   jax.ShapeDtypeStruct((n, H_GDN, DK_GDN, DV_GDN), F32)],
        compiler_params=_cparams(("arbitrary",)),
        name="gdn_sample",
    )(upx, z8, ba8, s0, w_conv, alog_pad, dtb_pad, gng)


def _diff_norm(o, g, lam_init):
    return o * lax.rsqrt(jnp.mean(o * o, axis=-1, keepdims=True) + DIFF_NORM_EPS) * g * (1.0 - lam_init)


def _attn_prompt_body(tbl_ref, lq1_ref, lk1_ref, lq2_ref, lk2_ref, g_ref, bias_ref, q_ref, k_ref, v_ref, o_ref,
                      *, lam_init, t):
    h = pl.program_id(1)
    far_bias = tbl_ref[N_BUCKETS - 1, h]
    lam = _lam(lq1_ref, lk1_ref, lq2_ref, lk2_ref, lam_init)
    kb = k_ref[...].astype(BF16)
    vb = v_ref[...].astype(BF16)
    g = g_ref[...]
    lane = lax.broadcasted_iota(jnp.int32, (1, HEAD_W), 1)
    zero = jnp.zeros((), BF16)
    for i in range(t // Q_TILE):
        r0 = i * Q_TILE
        qi = q_ref[r0:r0 + Q_TILE, :] * jnp.asarray(DH_DIFF ** -0.5, BF16)
        near0 = max(i - 1, 0) * Q_TILE
        k_near, v_near = kb[near0:r0 + Q_TILE], vb[near0:r0 + Q_TILE]
        bias_near = bias_ref[:, 2 * Q_TILE - (r0 + Q_TILE - near0):]
        has_far = near0 > 0
        es, rs = [], []
        for comp in range(2):
            qc = jnp.where((lane >= DH_DIFF) == (comp == 1), qi, zero)
            sn = _dot(qc, k_near, NT) + bias_near
            m = jnp.max(sn, axis=-1, keepdims=True)
            if has_far:
                sf = _dot(qc, kb[:near0], NT) + far_bias
                m = jnp.maximum(m, jnp.max(sf, axis=-1, keepdims=True))
            en = jnp.exp(sn - m)
            l = jnp.sum(en, axis=-1, keepdims=True)
            ef = None
            if has_far:
                ef = jnp.exp(sf - m)
                l = l + jnp.sum(ef, axis=-1, keepdims=True)
            es.append((en, ef))
            rs.append(1.0 / l)
        r1 = lam * rs[1]
        a_near = (es[0][0] * rs[0] - es[1][0] * r1).astype(BF16)
        o = _dot(a_near, v_near)
        if has_far:
            a_far = (es[0][1] * rs[0] - es[1][1] * r1).astype(BF16)
            o = o + _dot(a_far, vb[:near0])
        o_ref[r0:r0 + Q_TILE, :] = _diff_norm(o, g, lam_init).astype(o_ref.dtype)


def _attn_prompt(rel_bias, lam_w, diff_g, bias_p, q, k, v, lam_init):
    nb, t, _ = q.shape
    head = lambda: pl.BlockSpec((None, t, HEAD_W), lambda b, h: (b, 0, h))
    small = _const_spec((1, DH_DIFF))
    return pl.pallas_call(
        functools.partial(_attn_prompt_body, lam_init=lam_init, t=t),
        grid=(nb, H_DIFF),
        in_specs=[pl.BlockSpec(memory_space=pltpu.SMEM), small, small, small, small, _const_spec((1, HEAD_W)),
                  pl.BlockSpec((None, Q_TILE, 2 * Q_TILE), lambda b, h: (h, 0, 0)),
                  head(), head(), head()],
        out_specs=head(),
        out_shape=jax.ShapeDtypeStruct((nb, t, DIFF_V), BF16),
        compiler_params=_cparams(("arbitrary", "arbitrary")),
        name="attn_prompt",
    )(rel_bias, *lam_w, diff_g, bias_p, q, k, v)


def _attn_sample_body(pt_ref, lq1_ref, lk1_ref, lq2_ref, lk2_ref, g_ref, bl_ref, bn_ref, far_ref,
                      q_ref, kn_ref, vn_ref, ck_hbm, cv_hbm, o_ref,
                      kbuf, vbuf, sem, m_ref, l_ref, acc_ref, *, lam_init, n_chunks, dec_seq):
    c = pl.program_id(1)
    step = pl.program_id(0) * n_chunks + c
    total = pl.num_programs(0) * n_chunks
    nrow = dec_seq * 2 * H_DIFF

    def copies(stp, slot):
        out = []
        for p in range(PAGES_PER_STEP):
            page = pt_ref[stp * PAGES_PER_STEP + p]
            rows = pl.ds(p * PAGE_SIZE, PAGE_SIZE)
            out.append(pltpu.make_async_copy(ck_hbm.at[page], kbuf.at[slot, rows], sem.at[0, slot]))
            out.append(pltpu.make_async_copy(cv_hbm.at[page], vbuf.at[slot, rows], sem.at[1, slot]))
        return out

    @pl.when(step == 0)
    def _():
        for cp in copies(0, 0):
            cp.start()

    @pl.when(step + 1 < total)
    def _():
        for cp in copies(step + 1, (step + 1) % 2):
            cp.start()

    lane = lax.broadcasted_iota(jnp.int32, (2 * H_DIFF, DIFF_QK), 1)
    hc = lax.broadcasted_iota(jnp.int32, (2 * H_DIFF, DIFF_QK), 0)
    qmask = (lane // DH_DIFF) == hc
    qbd = jnp.concatenate(
        [jnp.where(qmask, jnp.broadcast_to(q_ref[i:i + 1, :], (2 * H_DIFF, DIFF_QK)), 0.0) for i in range(dec_seq)],
        axis=0)
    qbd = (qbd * (DH_DIFF ** -0.5)).astype(BF16)

    @pl.when(c == 0)
    def _():
        s = _dot(qbd, kn_ref[...].astype(BF16), NT) + bn_ref[...]
        m = jnp.max(s, axis=-1, keepdims=True)
        p = jnp.exp(s - m)
        m_ref[...] = m
        l_ref[...] = jnp.sum(p, axis=-1, keepdims=True)
        acc_ref[...] = _dot(p.astype(BF16), vn_ref[...].astype(BF16))

    slot = step % 2
    for cp in copies(step, slot):
        cp.wait()

    bias = jnp.where(c == n_chunks - 1, bl_ref[...], far_ref[:, 0:1])
    s = _dot(qbd, kbuf[slot].astype(BF16), NT) + bias
    m_old = m_ref[...]
    m_new = jnp.maximum(m_old, jnp.max(s, axis=-1, keepdims=True))
    alpha = jnp.exp(m_old - m_new)
    p = jnp.exp(s - m_new)
    m_ref[...] = m_new
    l_ref[...] = alpha * l_ref[...] + jnp.sum(p, axis=-1, keepdims=True)
    acc_ref[...] = alpha * acc_ref[...] + _dot(p.astype(BF16), vbuf[slot].astype(BF16))

    @pl.when(c == n_chunks - 1)
    def _():
        lam = _lam(lq1_ref, lk1_ref, lq2_ref, lk2_ref, lam_init)
        row = lax.broadcasted_iota(jnp.int32, (nrow, 1), 0)
        coef = jnp.where(row % 2 == 0, 1.0, -lam) / l_ref[...]
        lane_head = lax.broadcasted_iota(jnp.int32, (nrow, DIFF_V), 1) // HEAD_W
        wm = jnp.where(lane_head == (row % (2 * H_DIFF)) // 2, acc_ref[...] * coef, 0.0)
        sel = (lax.broadcasted_iota(jnp.int32, (SUBLANES, nrow), 1) // (2 * H_DIFF)
               == lax.broadcasted_iota(jnp.int32, (SUBLANES, nrow), 0)).astype(F32)
        o = _dot_f32(sel, wm)
        g = g_ref[...]
        o_ref[...] = jnp.concatenate(
            [_diff_norm(o[:, hh * HEAD_W:(hh + 1) * HEAD_W], g, lam_init) for hh in range(H_DIFF)], axis=1)


def _attn_sample(page_table, lam_w, diff_g, bias_last, bias_new, far, q8, kn8, vn8, cache_k, cache_v, lam_init,
                 dec_seq):
    n, n_pages = page_table.shape
    n_chunks = n_pages // PAGES_PER_STEP
    nrow = dec_seq * 2 * H_DIFF
    nkeys = PAGES_PER_STEP * PAGE_SIZE
    c2 = lambda shape: pl.BlockSpec(shape, lambda s, c, pt: (0,) * len(shape))
    per_seq = pl.BlockSpec((None, SUBLANES, DIFF_QK), lambda s, c, pt: (s, 0, 0))
    grid_spec = pltpu.PrefetchScalarGridSpec(
        num_scalar_prefetch=1,
        grid=(n, n_chunks),
        in_specs=[c2((1, DH_DIFF)), c2((1, DH_DIFF)), c2((1, DH_DIFF)), c2((1, DH_DIFF)), c2((1, HEAD_W)),
                  c2((nrow, nkeys)), c2((nrow, SUBLANES)), c2((nrow, LANES)),
                  per_seq, per_seq, per_seq,
                  pl.BlockSpec(memory_space=pl.ANY), pl.BlockSpec(memory_space=pl.ANY)],
        out_specs=per_seq,
        scratch_shapes=[pltpu.VMEM((2, nkeys, DIFF_QK), F32), pltpu.VMEM((2, nkeys, DIFF_V), F32),
                        pltpu.SemaphoreType.DMA((2, 2)),
                        pltpu.VMEM((nrow, 1), F32), pltpu.VMEM((nrow, 1), F32), pltpu.VMEM((nrow, DIFF_V), F32)],
    )
    return pl.pallas_call(
        functools.partial(_attn_sample_body, lam_init=lam_init, n_chunks=n_chunks, dec_seq=dec_seq),
        grid_spec=grid_spec,
        out_shape=jax.ShapeDtypeStruct((n, SUBLANES, DIFF_V), F32),
        compiler_params=_cparams(("arbitrary", "arbitrary")),
        name="attn_sample",
    )(page_table.reshape(-1), *lam_w, diff_g, bias_last, bias_new, far, q8, kn8, vn8, cache_k, cache_v)


def _post_body(x_ref, mod_ref, og_ref, od_ref, gg_ref, gd_ref, wbg_ref, wbd_ref, wo_ref, l1g_ref, l1b_ref,
               wup_ref, bup_ref, wdn_ref, bdn_ref, l2g_ref, l2b_ref, y_ref, *, per_token):
    md = lambda i: _mod(mod_ref, i, per_token)
    m = (_sigmoid(gg_ref[...].astype(F32)) * _dot(og_ref[...], wbg_ref[...])
         + _sigmoid(gd_ref[...].astype(F32)) * _dot(od_ref[...], wbd_ref[...]))
    mix = _dot(m.astype(BF16), wo_ref[...])
    x1 = _ln(ALPHA * x_ref[...] + md(2) * mix) * l1g_ref[...] + l1b_ref[...]
    h2 = _ln(x1) * (1.0 + md(4)) + md(3)
    up = _dot(h2.astype(BF16), wup_ref[...]) + bup_ref[...]
    act = jnp.square(jnp.maximum(up, 0.0))
    f = _dot(act.astype(BF16), wdn_ref[...]) + bdn_ref[...]
    y_ref[...] = _ln(ALPHA * x1 + md(5) * f) * l2g_ref[...] + l2b_ref[...]


def _post(x, mods, og, od, gg, gd, wts, tm, per_token):
    nb, t, _ = x.shape
    if per_token:
        mod_spec = pl.BlockSpec((6, tm, D_MODEL), lambda b, i: (0, i, 0))
    else:
        mod_spec = pl.BlockSpec((None, 6, D_MODEL), lambda b, i: (b, 0, 0))
    row = lambda n: pl.BlockSpec((None, tm, n), lambda b, i: (b, i, 0))
    return pl.pallas_call(
        functools.partial(_post_body, per_token=per_token),
        grid=(nb, t // tm),
        in_specs=[row(D_MODEL), mod_spec, row(GDN_V), row(DIFF_V), row(D_MODEL), row(D_MODEL)]
                 + [_const_spec(w.shape) for w in wts],
        out_specs=row(D_MODEL),
        out_shape=jax.ShapeDtypeStruct((nb, t, D_MODEL), F32),
        compiler_params=_cparams(("arbitrary", "arbitrary")),
        name="post",
    )(x, mods, og, od, gg, gd, *wts)


def kernel(x_prompt, x_sample, c_prompt, c_sample, cache_k, cache_v, page_table, state_conv, state_gdn, rel_bias,
           w_in, w_conv, a_log, dt_bias, gdn_norm_g, lam_q1, lam_k1, lam_q2, lam_k2, diff_norm_g, w_br_gdn,
           w_br_diff, w_o, w_ada, b_ada, ln1_g, ln1_b, ln2_g, ln2_b, w_up, b_up, w_down, b_down):
    assert DEPTH == 1 and w_in.shape[0] == 1
    l = 0
    lam_init = 0.8 - 0.6 * math.exp(-0.3 * l)
    nbp, seq, _ = x_prompt.shape
    nbs, dec_seq, _ = x_sample.shape
    n_tok_s = nbs * dec_seq
    past_len = page_table.shape[1] * PAGE_SIZE
    assert seq % Q_TILE == 0 and CONV_W - 1 <= dec_seq <= SUBLANES and page_table.shape[1] % PAGES_PER_STEP == 0

    offs = np.concatenate([[0], np.cumsum(IN_SPLITS)])
    col = lambda i: w_in[l][:, int(offs[i]):int(offs[i + 1])]
    w_main = jnp.concatenate([col(0), col(1), col(4), col(5), col(6), col(7), col(8)], axis=1).astype(BF16)
    w_small = jnp.concatenate([col(2), col(3), jnp.zeros((D_MODEL, LANES - 2 * H_GDN), F32)], axis=1).astype(BF16)
    pad_gate = lambda a: jnp.zeros((1, LANES), F32).at[0, H_GDN:2 * H_GDN].set(a)
    alog_pad, dtb_pad = pad_gate(a_log[l]), pad_gate(dt_bias[l])
    gng = gdn_norm_g[l].reshape(1, DV_GDN)
    diff_g = diff_norm_g[l].reshape(1, HEAD_W)
    lam_w = (lam_q1[l].reshape(1, -1), lam_k1[l].reshape(1, -1), lam_q2[l].reshape(1, -1), lam_k2[l].reshape(1, -1))
    r2 = lambda a: a.reshape(1, -1)
    post_w = (w_br_gdn[l].astype(BF16), w_br_diff[l].astype(BF16), w_o[l].astype(BF16), r2(ln1_g[l]), r2(ln1_b[l]),
              w_up[l].astype(BF16), r2(b_up[l]), w_down[l].astype(BF16), r2(b_down[l]), r2(ln2_g[l]), r2(ln2_b[l]))

    ada = _ada(jnp.concatenate([c_prompt, c_sample], axis=0), w_ada[l].astype(BF16), r2(b_ada[l]))
    mods_p = ada[:nbp].reshape(nbp, 6, D_MODEL)
    mods_s = jnp.repeat(ada[nbp:].reshape(nbs, 6, D_MODEL), dec_seq, axis=0).transpose(1, 0, 2)
    bias_p, bias_last, bias_new, far = _bias_prep(rel_bias, past_len, dec_seq)

    conv_p, z_p, q_p, k_p, v_p, gg_p, gd_p, ba_p = _inproj(x_prompt, mods_p, w_main, w_small, 256, False)
    og_p, s_p = _gdn_prompt(conv_p, z_p, ba_p, w_conv[l], alog_pad, dtb_pad, gng, 2 * CHUNK)
    od_p = _attn_prompt(rel_bias, lam_w, diff_g, bias_p, q_p, k_p, v_p, lam_init)
    y_p = _post(x_prompt, mods_p, og_p, od_p, gg_p, gd_p, post_w, 256, False)

    xs = x_sample.reshape(1, n_tok_s, D_MODEL)
    conv_s, z_s, q_s, k_s, v_s, gg_s, gd_s, ba_s = _inproj(xs, mods_s, w_main, w_small, n_tok_s, True)
    per_seq = lambda a: a.reshape(nbs, dec_seq, a.shape[-1])
    pad8 = lambda a: jnp.pad(a.astype(F32), ((0, 0), (0, SUBLANES - dec_seq), (0, 0)))
    conv_s3 = per_seq(conv_s)
    upx = jnp.concatenate([jnp.zeros((nbs, SUBLANES - (CONV_W - 1), GDN_CONV_CH), F32), state_conv[l], conv_s3,
                           jnp.zeros((nbs, SUBLANES - dec_seq, GDN_CONV_CH), F32)], axis=1)
    og_s8, s_s = _gdn_sample(upx, pad8(per_seq(z_s)), pad8(per_seq(ba_s)), state_gdn[l], w_conv[l], alog_pad,
                             dtb_pad, gng, dec_seq, 8)
    ck = cache_k[l].reshape(cache_k.shape[1], PAGE_SIZE, DIFF_QK)
    cv = cache_v[l].reshape(cache_v.shape[1], PAGE_SIZE, DIFF_V)
    od_s8 = _attn_sample(page_table, lam_w, diff_g, bias_last, bias_new, far, pad8(per_seq(q_s)),
                         pad8(per_seq(k_s)), pad8(per_seq(v_s)), ck, cv, lam_init, dec_seq)
    flat = lambda a8: a8[:, :dec_seq].reshape(1, n_tok_s, a8.shape[-1]).astype(BF16)
    y_s = _post(xs, mods_s, flat(og_s8), flat(od_s8), gg_s, gd_s, post_w, 256, True)

    hd = lambda a, b: a.reshape(1, b, -1, H_DIFF, 2, DH_DIFF)
    hv = lambda a, b: a.reshape(1, b, -1, H_DIFF, 2 * DH_DIFF)
    return (y_p, y_s.reshape(nbs, dec_seq, D_MODEL),
            hd(k_p, nbp), hv(v_p, nbp), conv_p[:, seq - (CONV_W - 1):][None], s_p[None],
            hd(k_s, nbs), hv(v_s, nbs), conv_s3[:, dec_seq - (CONV_W - 1):][None], s_s[None])
```

```python
import functools
import math

import numpy as np
import jax
import jax.numpy as jnp
from jax import lax
from jax.experimental import pallas as pl
from jax.experimental.pallas import tpu as pltpu

F32 = jnp.float32
BF16 = jnp.bfloat16

D_MODEL = 1024
DEPTH = 1
PAGE_SIZE = 128
H_GDN = 4
DK_GDN = 128
DV_GDN = 128
CONV_W = 4
CHUNK = 64
GDN_NORM_EPS = 1e-6
H_DIFF = 4
DH_DIFF = 64
DIFF_NORM_EPS = 1e-5
N_BUCKETS = 32
MAX_DISTANCE = 128
D_FF = 4 * D_MODEL
LN_EPS = 1e-5
GDN_QK = H_GDN * DK_GDN
GDN_V = H_GDN * DV_GDN
GDN_CONV_CH = 2 * GDN_QK + GDN_V
DIFF_QK = H_DIFF * 2 * DH_DIFF
DIFF_V = H_DIFF * 2 * DH_DIFF
HEAD_W = 2 * DH_DIFF
IN_SPLITS = (GDN_CONV_CH, GDN_V, H_GDN, H_GDN, DIFF_QK, DIFF_QK, DIFF_V, D_MODEL, D_MODEL)
ALPHA = (2.0 * DEPTH) ** 0.25

LANES = 128
SUBLANES = 8
VMEM_LIMIT = 56 * 1024 * 1024

Q_TILE = 256
PAGES_PER_STEP = 8
SEQ_PER_ITER = 2
MASKED = N_BUCKETS

NN = (((1,), (0,)), ((), ()))
NT = (((1,), (1,)), ((), ()))
TN = (((0,), (0,)), ((), ()))


def _cparams(sem):
    return pltpu.CompilerParams(dimension_semantics=sem, vmem_limit_bytes=VMEM_LIMIT)


def _const_spec(shape):
    nd = len(shape)
    return pl.BlockSpec(shape, lambda *_: (0,) * nd, pipeline_mode=pl.Buffered(1))


def _sigmoid(x):
    return 1.0 / (1.0 + jnp.exp(-x))


def _silu(x):
    return x * _sigmoid(x)


def _softplus(x):
    return jnp.maximum(x, 0.0) + jnp.log(1.0 + jnp.exp(-jnp.abs(x)))


def _ln(x):
    mu = jnp.mean(x, axis=-1, keepdims=True)
    xc = x - mu
    var = jnp.mean(xc * xc, axis=-1, keepdims=True)
    return xc * lax.rsqrt(var + LN_EPS)


def _dot(a, b, dims=NN):
    return lax.dot_general(a, b, dims, preferred_element_type=F32)


def _dot_bf(a, b, dims=NN):
    return _dot(a.astype(BF16), b.astype(BF16), dims)


def _split(a):
    hi = a.astype(BF16)
    lo = (a - hi.astype(F32)).astype(BF16)
    return hi, lo


def _dot3(a, b, dims=NN, fuse=True):
    ah, al = _split(a)
    bh, bl = _split(b)
    if not fuse:
        return _dot(ah, bh, dims) + (_dot(ah, bl, dims) + _dot(al, bh, dims))
    lhs = jnp.concatenate([ah, ah, al], axis=1)
    rhs = jnp.concatenate([bh, bl, bh], axis=1 if dims == NT else 0)
    return _dot(lhs, rhs, dims)


def _dot_f32(a, b, dims=NN):
    return lax.dot_general(a, b, dims, precision=lax.Precision.HIGHEST, preferred_element_type=F32)


def _mod(mod_ref, i, per_token):
    return mod_ref[i] if per_token else mod_ref[i:i + 1, :]


def _ada_body(c_ref, w_ref, b_ref, o_ref):
    s = _silu(c_ref[...])
    o_ref[...] = _dot(s.astype(BF16), w_ref[...]) + b_ref[...]


def _ada(c_all, w_ada, b_ada):
    n = c_all.shape[0]
    return pl.pallas_call(
        _ada_body,
        grid=(6,),
        in_specs=[pl.BlockSpec((n, D_MODEL), lambda j: (0, 0)),
                  pl.BlockSpec((D_MODEL, D_MODEL), lambda j: (0, j)),
                  pl.BlockSpec((1, D_MODEL), lambda j: (0, j))],
        out_specs=pl.BlockSpec((n, D_MODEL), lambda j: (0, j)),
        out_shape=jax.ShapeDtypeStruct((n, 6 * D_MODEL), F32),
        compiler_params=_cparams(("arbitrary",)),
        name="ada",
    )(c_all, w_ada, b_ada)


def _t5_bucket_np(rel):
    n = np.maximum(rel, 0)
    max_exact = N_BUCKETS // 2
    nf = np.maximum(n, 1).astype(np.float32)
    large = max_exact + (np.log(nf / np.float32(max_exact)) / np.float32(math.log(MAX_DISTANCE / max_exact))
                         * np.float32(N_BUCKETS - max_exact)).astype(np.int32)
    large = np.minimum(large, N_BUCKETS - 1)
    return np.where(n < max_exact, n, large).astype(np.int32)


def _bias_codes(past_len, dec_seq):
    r = np.arange(Q_TILE)[:, None]
    c = np.arange(Q_TILE)[None, :]
    prev_tile = _t5_bucket_np(Q_TILE + r - c)
    diag_tile = np.where(r >= c, _t5_bucket_np(r - c), MASKED)
    prompt = np.concatenate([prev_tile, diag_tile], axis=1).astype(np.int32)
    nrow = H_DIFF * SUBLANES
    qrow = (np.arange(nrow) % dec_seq)[:, None]
    nkeys = PAGES_PER_STEP * PAGE_SIZE
    kpos = past_len - nkeys + np.arange(nkeys)[None, :]
    last = _t5_bucket_np(past_len + qrow - kpos).astype(np.int32)
    j = np.arange(SUBLANES)[None, :]
    new = np.where((j <= qrow) & (j < dec_seq), _t5_bucket_np(qrow - j), MASKED).astype(np.int32)
    return prompt, last, new


def _bias_body(tbl_ref, cp_ref, cl_ref, cn_ref, bp_ref, bl_ref, bn_ref, far_ref):
    def lookup(codes, h):
        acc = jnp.full(codes.shape, -jnp.inf, F32)
        for b in range(N_BUCKETS):
            acc = jnp.where(codes == b, tbl_ref[b, h], acc)
        return acc

    cp = cp_ref[...]
    for h in range(H_DIFF):
        bp_ref[h] = lookup(cp, h)
    cl = cl_ref[...]
    cn = cn_ref[...]
    nrow = cl.shape[0]
    row_head = lax.broadcasted_iota(jnp.int32, (nrow, 1), 0) // SUBLANES
    bl = jnp.zeros(cl.shape, F32)
    bn = jnp.zeros(cn.shape, F32)
    far = jnp.zeros((nrow, LANES), F32)
    for h in range(H_DIFF):
        bl = jnp.where(row_head == h, lookup(cl, h), bl)
        bn = jnp.where(row_head == h, lookup(cn, h), bn)
        far = jnp.where(row_head == h, tbl_ref[N_BUCKETS - 1, h], far)
    bl_ref[...] = bl
    bn_ref[...] = bn
    far_ref[...] = far


def _bias_prep(rel_bias, past_len, dec_seq):
    cp, cl, cn = _bias_codes(past_len, dec_seq)
    nrow = cl.shape[0]
    vm = pl.BlockSpec(memory_space=pltpu.VMEM)
    return pl.pallas_call(
        _bias_body,
        in_specs=[pl.BlockSpec(memory_space=pltpu.SMEM), vm, vm, vm],
        out_specs=[vm, vm, vm, vm],
        out_shape=[jax.ShapeDtypeStruct((H_DIFF,) + cp.shape, F32),
                   jax.ShapeDtypeStruct(cl.shape, F32),
                   jax.ShapeDtypeStruct(cn.shape, F32),
                   jax.ShapeDtypeStruct((nrow, LANES), F32)],
        name="bias_prep",
    )(rel_bias, jnp.asarray(cp), jnp.asarray(cl), jnp.asarray(cn))


def _lam(lq1_ref, lk1_ref, lq2_ref, lk2_ref, lam_init):
    s1 = jnp.sum(lq1_ref[...] * lk1_ref[...], axis=-1, keepdims=True)
    s2 = jnp.sum(lq2_ref[...] * lk2_ref[...], axis=-1, keepdims=True)
    return jnp.exp(s1) - jnp.exp(s2) + lam_init


_SEG = {"conv": (0, 1536), "z": (1536, 2048), "q": (2048, 2560), "k": (2560, 3072), "v": (3072, 3584),
        "gg": (3584, 4608), "gd": (4608, 5632)}
N_MAIN = 5632


def _inproj_body(x_ref, mod_ref, wm_ref, ws_ref, conv_ref, z_ref, q_ref, k_ref, v_ref, gg_ref, gd_ref, ba_ref,
                 *, per_token):
    h = _ln(x_ref[...]) * (1.0 + _mod(mod_ref, 1, per_token)) + _mod(mod_ref, 0, per_token)
    hb = h.astype(BF16)

    def seg(name):
        a, b = _SEG[name]
        return _dot(hb, wm_ref[:, a:b])

    conv_ref[...] = seg("conv")
    z_ref[...] = seg("z").astype(BF16)
    q_ref[...] = seg("q").astype(BF16)
    k_ref[...] = seg("k")
    v_ref[...] = seg("v")
    gg_ref[...] = seg("gg").astype(BF16)
    gd_ref[...] = seg("gd").astype(BF16)
    ba_ref[...] = _dot(hb, ws_ref[...])


def _inproj(x, mods, w_main, w_small, tm, per_token):
    nb, t, _ = x.shape
    if per_token:
        mod_spec = pl.BlockSpec((6, tm, D_MODEL), lambda b, i: (0, i, 0))
    else:
        mod_spec = pl.BlockSpec((None, 6, D_MODEL), lambda b, i: (b, 0, 0))

    def out(n, dt):
        return pl.BlockSpec((None, tm, n), lambda b, i: (b, i, 0)), jax.ShapeDtypeStruct((nb, t, n), dt)

    outs = [out(GDN_CONV_CH, F32), out(GDN_V, BF16), out(DIFF_QK, BF16), out(DIFF_QK, F32), out(DIFF_V, F32),
            out(D_MODEL, BF16), out(D_MODEL, BF16), out(LANES, F32)]
    return pl.pallas_call(
        functools.partial(_inproj_body, per_token=per_token),
        grid=(nb, t // tm),
        in_specs=[pl.BlockSpec((None, tm, D_MODEL), lambda b, i: (b, i, 0)), mod_spec,
                  _const_spec((D_MODEL, N_MAIN)), _const_spec((D_MODEL, LANES))],
        out_specs=[o[0] for o in outs],
        out_shape=[o[1] for o in outs],
        compiler_params=_cparams(("arbitrary", "arbitrary")),
        name="inproj",
    )(x, mods, w_main, w_small)


def _gdn_solve(chains, c):
    ii = lax.broadcasted_iota(jnp.int32, (c, c), 0)
    jj = lax.broadcasted_iota(jnp.int32, (c, c), 1)
    incl = ii >= jj
    eye = (ii == jj).astype(F32)
    fuse = c % CHUNK == 0
    gamma = [jnp.where(incl, jnp.exp(jnp.where(incl, ch[4] - ch[5], 0.0)), 0.0) for ch in chains]
    kk = [_dot3(ch[1], ch[1], NT) for ch in chains]
    a = [jnp.where(ii > jj, ch[3] * k * g, 0.0) for ch, k, g in zip(chains, kk, gamma)]
    a8 = [jnp.where((ii >> 3) == (jj >> 3), m, 0.0) for m in a]
    p = [_dot3(m, m, fuse=fuse) for m in a8]
    x = [eye - m for m in a8]
    x = [xi + _dot3(xi, pi, fuse=fuse) for xi, pi in zip(x, p)]
    p = [_dot3(pi, pi, fuse=fuse) for pi in p]
    x = [xi + _dot3(xi, pi, fuse=fuse) for xi, pi in zip(x, p)]
    sh = 3
    while (1 << sh) < c:
        lvl = ((ii >> (sh + 1)) == (jj >> (sh + 1))) & ((ii >> sh) != (jj >> sh))
        t = [_dot3(jnp.where(lvl, m, 0.0), xi, fuse=fuse) for m, xi in zip(a, x)]
        x = [xi - _dot3(xi, ti, fuse=fuse) for xi, ti in zip(x, t)]
        sh += 1
    egc = [jnp.exp(ch[4]) for ch in chains]
    uw = [_dot3(xi, jnp.concatenate([ch[2] * ch[3], ch[1] * (ch[3] * e)], axis=1), fuse=fuse)
          for xi, ch, e in zip(x, chains, egc)]
    qk = [jnp.where(incl, _dot_bf(ch[0], ch[1], NT) * g, 0.0) for ch, g in zip(chains, gamma)]
    out = []
    for ch, e, uwi, qki in zip(chains, egc, uw, qk):
        g_last = ch[4][c - 1:c, :]
        wq = jnp.concatenate([uwi[:, DV_GDN:], ch[0] * e], axis=0).astype(BF16)
        k_dec = (ch[1] * jnp.exp(g_last - ch[4])).astype(BF16)
        out.append((uwi[:, :DV_GDN], wq, qki.astype(BF16), k_dec, jnp.exp(g_last)))
    return out


def _gdn_state_step(sols, states, c):
    sb = [s.astype(BF16) for s in states]
    wq = [_dot(sol[1], b) for sol, b in zip(sols, sb)]
    vb = [(sol[0] - m[:c]).astype(BF16) for sol, m in zip(sols, wq)]
    o = [m[c:] + _dot(sol[2], v) for sol, m, v in zip(sols, wq, vb)]
    s_new = [sol[4] * s + _dot(sol[3], v, TN) for sol, s, v in zip(sols, states, vb)]
    return o, s_new


def _l2n(x):
    return x * lax.rsqrt(jnp.sum(x * x, axis=-1, keepdims=True) + 1e-6)


def _gdn_gates(ba, alog_ref, dtb_ref):
    beta = _sigmoid(ba)
    g = -jnp.exp(alog_ref[...]) * _softplus(ba + dtb_ref[...])
    return beta, g


def _gdn_out(o, z, gng):
    o = o * lax.rsqrt(jnp.mean(o * o, axis=-1, keepdims=True) + GDN_NORM_EPS) * gng
    return o * _silu(z)


def _cumsum_mats(c):
    ii = lax.broadcasted_iota(jnp.int32, (c, c), 0)
    jj = lax.broadcasted_iota(jnp.int32, (c, c), 1)
    return (ii >= jj).astype(F32)


def _gdn_prompt_body(conv_ref, z_ref, ba_ref, wc_ref, alog_ref, dtb_ref, gng_ref, og_ref, s_ref, cbuf_ref, *, tt):
    t = pl.program_id(1)

    @pl.when(t == 0)
    def _():
        s_ref[...] = jnp.zeros(s_ref.shape, F32)
        cbuf_ref[0:SUBLANES, :] = jnp.zeros((SUBLANES, GDN_CONV_CH), F32)

    u_in = conv_ref[...]
    cbuf_ref[SUBLANES:, :] = u_in
    y = cbuf_ref[SUBLANES:, :] * wc_ref[CONV_W - 1:CONV_W, :]
    for j in range(CONV_W - 1):
        off = SUBLANES - (CONV_W - 1) + j
        y = y + cbuf_ref[off:off + tt, :] * wc_ref[j:j + 1, :]
    cbuf_ref[0:SUBLANES, :] = u_in[tt - SUBLANES:, :]
    qkv = _silu(y)

    beta_all, g_all = _gdn_gates(ba_ref[...], alog_ref, dtb_ref)
    ltri = _cumsum_mats(CHUNK)
    nch = tt // CHUNK
    gc_all = jnp.concatenate([_dot_f32(ltri, g_all[c * CHUNK:(c + 1) * CHUNK, :]) for c in range(nch)], axis=0)
    gc_t = gc_all.T
    gng = gng_ref[...]
    chains = []
    for c in range(nch):
        rows = slice(c * CHUNK, (c + 1) * CHUNK)
        for h in range(H_GDN):
            lo = h * DK_GDN
            chains.append((_l2n(qkv[rows, lo:lo + DK_GDN]) * (DK_GDN ** -0.5),
                           _l2n(qkv[rows, GDN_QK + lo:GDN_QK + lo + DK_GDN]),
                           qkv[rows, 2 * GDN_QK + lo:2 * GDN_QK + lo + DV_GDN],
                           beta_all[rows, h:h + 1], gc_all[rows, H_GDN + h:H_GDN + h + 1],
                           gc_t[H_GDN + h:H_GDN + h + 1, rows]))
    sols = _gdn_solve(chains, CHUNK)
    states = [s_ref[h] for h in range(H_GDN)]
    for c in range(nch):
        rows = slice(c * CHUNK, (c + 1) * CHUNK)
        o, states = _gdn_state_step(sols[c * H_GDN:(c + 1) * H_GDN], states, CHUNK)
        for h in range(H_GDN):
            cols = slice(h * DV_GDN, (h + 1) * DV_GDN)
            og_ref[rows, cols] = _gdn_out(o[h], z_ref[rows, cols].astype(F32), gng).astype(og_ref.dtype)
    for h in range(H_GDN):
        s_ref[h] = states[h]


def _gdn_prompt(conv_in, z, ba, w_conv, alog_pad, dtb_pad, gng, tt):
    nb, t, _ = conv_in.shape
    row = lambda n: pl.BlockSpec((None, tt, n), lambda b, i: (b, i, 0))
    return pl.pallas_call(
        functools.partial(_gdn_prompt_body, tt=tt),
        grid=(nb, t // tt),
        in_specs=[row(GDN_CONV_CH), row(GDN_V), row(LANES),
                  _const_spec((CONV_W, GDN_CONV_CH)), _const_spec((1, LANES)), _const_spec((1, LANES)),
                  _const_spec((1, DV_GDN))],
        out_specs=[row(GDN_V), pl.BlockSpec((None, H_GDN, DK_GDN, DV_GDN), lambda b, i: (b, 0, 0, 0))],
        out_shape=[jax.ShapeDtypeStruct((nb, t, GDN_V), BF16),
                   jax.ShapeDtypeStruct((nb, H_GDN, DK_GDN, DV_GDN), F32)],
        scratch_shapes=[pltpu.VMEM((tt + SUBLANES, GDN_CONV_CH), F32)],
        compiler_params=_cparams(("arbitrary", "arbitrary")),
        name="gdn_prompt",
    )(conv_in, z, ba, w_conv, alog_pad, dtb_pad, gng)


def _gdn_sample_body(upx_ref, z_ref, ba_ref, s0_ref, wc_ref, alog_ref, dtb_ref, gng_ref, og_ref, s_ref,
                     *, nb, n_valid):
    c = SUBLANES
    valid = lax.broadcasted_iota(jnp.int32, (c, 1), 0) < n_valid
    ltri = _cumsum_mats(c)
    gng = gng_ref[...]

    def body(i, carry):
        chains, states = [], []
        for n in (SEQ_PER_ITER * i + k for k in range(SEQ_PER_ITER)):
            y = upx_ref[n, pl.ds(SUBLANES, c), :] * wc_ref[CONV_W - 1:CONV_W, :]
            for j in range(CONV_W - 1):
                off = SUBLANES - (CONV_W - 1) + j
                y = y + upx_ref[n, pl.ds(off, c), :] * wc_ref[j:j + 1, :]
            qkv = _silu(y)
            beta_all, g_all = _gdn_gates(ba_ref[n], alog_ref, dtb_ref)
            beta_all = jnp.where(valid, beta_all, 0.0)
            g_all = jnp.where(valid, g_all, 0.0)
            gc_all = _dot_f32(ltri, g_all)
            gc_t = jnp.concatenate([gc_all, jnp.zeros((LANES - c, LANES), F32)], axis=0).T
            for h in range(H_GDN):
                lo = h * DK_GDN
                chains.append((_l2n(qkv[:, lo:lo + DK_GDN]) * (DK_GDN ** -0.5),
                               jnp.where(valid, _l2n(qkv[:, GDN_QK + lo:GDN_QK + lo + DK_GDN]), 0.0),
                               jnp.where(valid, qkv[:, 2 * GDN_QK + lo:2 * GDN_QK + lo + DV_GDN], 0.0),
                               beta_all[:, h:h + 1], gc_all[:, H_GDN + h:H_GDN + h + 1],
                               gc_t[H_GDN + h:H_GDN + h + 1, 0:c]))
                states.append(s0_ref[n, h])
        o, states = _gdn_state_step(_gdn_solve(chains, c), states, c)
        for k in range(SEQ_PER_ITER):
            n = SEQ_PER_ITER * i + k
            for h in range(H_GDN):
                cols = slice(h * DV_GDN, (h + 1) * DV_GDN)
                s_ref[n, h] = states[k * H_GDN + h]
                og_ref[n, :, cols] = _gdn_out(o[k * H_GDN + h], z_ref[n, :, cols], gng)
        return carry

    lax.fori_loop(0, nb // SEQ_PER_ITER, body, 0)


def _gdn_sample(upx, z8, ba8, s0, w_conv, alog_pad, dtb_pad, gng, n_valid, nb):
    n = upx.shape[0]
    blk = lambda *shape: pl.BlockSpec((nb,) + shape, lambda i: (i,) + (0,) * len(shape))
    return pl.pallas_call(
        functools.partial(_gdn_sample_body, nb=nb, n_valid=n_valid),
        grid=(n // nb,),
        in_specs=[blk(2 * SUBLANES, GDN_CONV_CH), blk(SUBLANES, GDN_V), blk(SUBLANES, LANES),
                  blk(H_GDN, DK_GDN, DV_GDN),
                  _const_spec((CONV_W, GDN_CONV_CH)), _const_spec((1, LANES)), _const_spec((1, LANES)),
                  _const_spec((1, DV_GDN))],
        out_specs=[blk(SUBLANES, GDN_V), blk(H_GDN, DK_GDN, DV_GDN)],
        out_shape=[jax.ShapeDtypeStruct((n, SUBLANES, GDN_V), F32),
                   jax.ShapeDtypeStruct((n, H_GDN, DK_GDN, DV_GDN), F32)],
        compiler_params=_cparams(("arbitrary",)),
        name="gdn_sample",
    )(upx, z8, ba8, s0, w_conv, alog_pad, dtb_pad, gng)


def _diff_norm(o, g, lam_init):
    return o * lax.rsqrt(jnp.mean(o * o, axis=-1, keepdims=True) + DIFF_NORM_EPS) * g * (1.0 - lam_init)


def _attn_prompt_body(tbl_ref, lq1_ref, lk1_ref, lq2_ref, lk2_ref, g_ref, bias_ref, q_ref, k_ref, v_ref, o_ref,
                      *, lam_init, t):
    h = pl.program_id(1)
    far_bias = tbl_ref[N_BUCKETS - 1, h]
    lam = _lam(lq1_ref, lk1_ref, lq2_ref, lk2_ref, lam_init)
    kb = k_ref[...].astype(BF16)
    vb = v_ref[...].astype(BF16)
    g = g_ref[...]
    lane = lax.broadcasted_iota(jnp.int32, (1, HEAD_W), 1)
    zero = jnp.zeros((), BF16)
    for i in range(t // Q_TILE):
        r0 = i * Q_TILE
        qi = q_ref[r0:r0 + Q_TILE, :] * jnp.asarray(DH_DIFF ** -0.5, BF16)
        near0 = max(i - 1, 0) * Q_TILE
        k_near, v_near = kb[near0:r0 + Q_TILE], vb[near0:r0 + Q_TILE]
        bias_near = bias_ref[:, 2 * Q_TILE - (r0 + Q_TILE - near0):]
        has_far = near0 > 0
        es, rs = [], []
        for comp in range(2):
            qc = jnp.where((lane >= DH_DIFF) == (comp == 1), qi, zero)
            sn = _dot(qc, k_near, NT) + bias_near
            m = jnp.max(sn, axis=-1, keepdims=True)
            if has_far:
                sf = _dot(qc, kb[:near0], NT) + far_bias
                m = jnp.maximum(m, jnp.max(sf, axis=-1, keepdims=True))
            en = jnp.exp(sn - m)
            l = jnp.sum(en, axis=-1, keepdims=True)
            ef = None
            if has_far:
                ef = jnp.exp(sf - m)
                l = l + jnp.sum(ef, axis=-1, keepdims=True)
            es.append((en, ef))
            rs.append(1.0 / l)
        r1 = lam * rs[1]
        a_near = (es[0][0] * rs[0] - es[1][0] * r1).astype(BF16)
        o = _dot(a_near, v_near)
        if has_far:
            a_far = (es[0][1] * rs[0] - es[1][1] * r1).astype(BF16)
            o = o + _dot(a_far, vb[:near0])
        o_ref[r0:r0 + Q_TILE, :] = _diff_norm(o, g, lam_init).astype(o_ref.dtype)


def _attn_prompt(rel_bias, lam_w, diff_g, bias_p, q, k, v, lam_init):
    nb, t, _ = q.shape
    head = lambda: pl.BlockSpec((None, t, HEAD_W), lambda b, h: (b, 0, h))
    small = _const_spec((1, DH_DIFF))
    return pl.pallas_call(
        functools.partial(_attn_prompt_body, lam_init=lam_init, t=t),
        grid=(nb, H_DIFF),
        in_specs=[pl.BlockSpec(memory_space=pltpu.SMEM), small, small, small, small, _const_spec((1, HEAD_W)),
                  pl.BlockSpec((None, Q_TILE, 2 * Q_TILE), lambda b, h: (h, 0, 0)),
                  head(), head(), head()],
        out_specs=head(),
        out_shape=jax.ShapeDtypeStruct((nb, t, DIFF_V), BF16),
        compiler_params=_cparams(("arbitrary", "arbitrary")),
        name="attn_prompt",
    )(rel_bias, *lam_w, diff_g, bias_p, q, k, v)


def _attn_sample_body(pt_ref, lq1_ref, lk1_ref, lq2_ref, lk2_ref, g_ref, bl_ref, bn_ref, far_ref,
                      q_ref, kn_ref, vn_ref, ck_hbm, cv_hbm, o_ref,
                      kbuf, vbuf, sem, m_ref, l_ref, acc_ref, *, lam_init, n_chunks, dec_seq):
    c = pl.program_id(1)
    step = pl.program_id(0) * n_chunks + c
    total = pl.num_programs(0) * n_chunks
    nrow = H_DIFF * SUBLANES
    nkeys = PAGES_PER_STEP * PAGE_SIZE
    vrows = PAGE_SIZE * H_DIFF

    def copies(stp, slot):
        out = []
        for p in range(PAGES_PER_STEP):
            page = pt_ref[stp * PAGES_PER_STEP + p]
            out.append(pltpu.make_async_copy(ck_hbm.at[page], kbuf.at[slot, :, pl.ds(p * PAGE_SIZE, PAGE_SIZE)],
                                             sem.at[0, slot]))
            out.append(pltpu.make_async_copy(cv_hbm.at[page], vbuf.at[slot, pl.ds(p * vrows, vrows)],
                                             sem.at[1, slot]))
        return out

    @pl.when(step == 0)
    def _():
        for cp in copies(0, 0):
            cp.start()

    @pl.when(step + 1 < total)
    def _():
        for cp in copies(step + 1, (step + 1) % 2):
            cp.start()

    lane_hc = lax.broadcasted_iota(jnp.int32, (SUBLANES, DIFF_QK), 1) // DH_DIFF
    row_c = lax.broadcasted_iota(jnp.int32, (SUBLANES, DIFF_QK), 0) // dec_seq
    q2 = q_ref[...] * (DH_DIFF ** -0.5)
    qbd = jnp.concatenate([jnp.where(lane_hc == 2 * h + row_c, q2, 0.0) for h in range(H_DIFF)],
                          axis=0).astype(BF16)
    heads = [slice(h * SUBLANES, (h + 1) * SUBLANES) for h in range(H_DIFF)]
    hcols = [slice(h * HEAD_W, (h + 1) * HEAD_W) for h in range(H_DIFF)]

    @pl.when(c == 0)
    def _():
        s = _dot(qbd, kn_ref[...].astype(BF16), NT) + bn_ref[...]
        m = jnp.max(s, axis=-1, keepdims=True)
        p = jnp.exp(s - m)
        m_ref[...] = m
        l_ref[...] = jnp.sum(p, axis=-1, keepdims=True)
        pb = p.astype(BF16)
        vn = vn_ref[...].astype(BF16)
        for h in range(H_DIFF):
            acc_ref[heads[h], :] = _dot(pb[heads[h]], vn[:, hcols[h]])

    slot = step % 2
    for cp in copies(step, slot):
        cp.wait()

    bias = jnp.where(c == n_chunks - 1, bl_ref[...], far_ref[:, 0:1])
    s = _dot(qbd, kbuf[slot].astype(BF16)) + bias
    m_old = m_ref[...]
    m_new = jnp.maximum(m_old, jnp.max(s, axis=-1, keepdims=True))
    alpha = jnp.exp(m_old - m_new)
    p = jnp.exp(s - m_new)
    m_ref[...] = m_new
    l_ref[...] = alpha * l_ref[...] + jnp.sum(p, axis=-1, keepdims=True)
    pb = p.astype(BF16)
    for h in range(H_DIFF):
        v_h = vbuf[slot, pl.ds(h, nkeys, stride=H_DIFF), :].astype(BF16)
        acc_ref[heads[h], :] = alpha[heads[h]] * acc_ref[heads[h], :] + _dot(pb[heads[h]], v_h)

    @pl.when(c == n_chunks - 1)
    def _():
        lam = _lam(lq1_ref, lk1_ref, lq2_ref, lk2_ref, lam_init)
        row = lax.broadcasted_iota(jnp.int32, (nrow, 1), 0)
        coef = jnp.where((row % SUBLANES) < dec_seq, 1.0, -lam) / l_ref[...]
        ri = lax.broadcasted_iota(jnp.int32, (nrow, nrow), 0)
        rj = lax.broadcasted_iota(jnp.int32, (nrow, nrow), 1)
        sel = ((ri // SUBLANES == rj // SUBLANES) & (ri % SUBLANES == rj % dec_seq)).astype(F32)
        o = _dot_f32(sel, acc_ref[...] * coef)
        g = g_ref[...]
        o_ref[...] = jnp.concatenate([_diff_norm(o[heads[h]], g, lam_init) for h in range(H_DIFF)], axis=1)


def _attn_sample(page_table, lam_w, diff_g, bias_last, bias_new, far, q8, kn8, vn8, cache_k, cache_v, lam_init,
                 dec_seq):
    n, n_pages = page_table.shape
    n_chunks = n_pages // PAGES_PER_STEP
    nrow = H_DIFF * SUBLANES
    nkeys = PAGES_PER_STEP * PAGE_SIZE
    c2 = lambda shape: pl.BlockSpec(shape, lambda s, c, pt: (0,) * len(shape))
    per_seq = pl.BlockSpec((None, SUBLANES, DIFF_QK), lambda s, c, pt: (s, 0, 0))
    grid_spec = pltpu.PrefetchScalarGridSpec(
        num_scalar_prefetch=1,
        grid=(n, n_chunks),
        in_specs=[c2((1, DH_DIFF)), c2((1, DH_DIFF)), c2((1, DH_DIFF)), c2((1, DH_DIFF)), c2((1, HEAD_W)),
                  c2((nrow, nkeys)), c2((nrow, SUBLANES)), c2((nrow, LANES)),
                  per_seq, per_seq, per_seq,
                  pl.BlockSpec(memory_space=pl.ANY), pl.BlockSpec(memory_space=pl.ANY)],
        out_specs=per_seq,
        scratch_shapes=[pltpu.VMEM((2, DIFF_QK, nkeys), F32), pltpu.VMEM((2, nkeys * H_DIFF, HEAD_W), F32),
                        pltpu.SemaphoreType.DMA((2, 2)),
                        pltpu.VMEM((nrow, 1), F32), pltpu.VMEM((nrow, 1), F32), pltpu.VMEM((nrow, HEAD_W), F32)],
    )
    return pl.pallas_call(
        functools.partial(_attn_sample_body, lam_init=lam_init, n_chunks=n_chunks, dec_seq=dec_seq),
        grid_spec=grid_spec,
        out_shape=jax.ShapeDtypeStruct((n, SUBLANES, DIFF_V), F32),
        compiler_params=_cparams(("arbitrary", "arbitrary")),
        name="attn_sample",
    )(page_table.reshape(-1), *lam_w, diff_g, bias_last, bias_new, far, q8, kn8, vn8, cache_k, cache_v)


def _post_body(x_ref, mod_ref, og_ref, od_ref, gg_ref, gd_ref, wbg_ref, wbd_ref, wo_ref, l1g_ref, l1b_ref,
               wup_ref, bup_ref, wdn_ref, bdn_ref, l2g_ref, l2b_ref, y_ref, *, per_token):
    md = lambda i: _mod(mod_ref, i, per_token)
    m = (_sigmoid(gg_ref[...].astype(F32)) * _dot(og_ref[...], wbg_ref[...])
         + _sigmoid(gd_ref[...].astype(F32)) * _dot(od_ref[...], wbd_ref[...]))
    mix = _dot(m.astype(BF16), wo_ref[...])
    x1 = _ln(ALPHA * x_ref[...] + md(2) * mix) * l1g_ref[...] + l1b_ref[...]
    h2 = _ln(x1) * (1.0 + md(4)) + md(3)
    up = _dot(h2.astype(BF16), wup_ref[...]) + bup_ref[...]
    act = jnp.square(jnp.maximum(up, 0.0))
    f = _dot(act.astype(BF16), wdn_ref[...]) + bdn_ref[...]
    y_ref[...] = _ln(ALPHA * x1 + md(5) * f) * l2g_ref[...] + l2b_ref[...]


def _post(x, mods, og, od, gg, gd, wts, tm, per_token):
    nb, t, _ = x.shape
    if per_token:
        mod_spec = pl.BlockSpec((6, tm, D_MODEL), lambda b, i: (0, i, 0))
    else:
        mod_spec = pl.BlockSpec((None, 6, D_MODEL), lambda b, i: (b, 0, 0))
    row = lambda n: pl.BlockSpec((None, tm, n), lambda b, i: (b, i, 0))
    return pl.pallas_call(
        functools.partial(_post_body, per_token=per_token),
        grid=(nb, t // tm),
        in_specs=[row(D_MODEL), mod_spec, row(GDN_V), row(DIFF_V), row(D_MODEL), row(D_MODEL)]
                 + [_const_spec(w.shape) for w in wts],
        out_specs=row(D_MODEL),
        out_shape=jax.ShapeDtypeStruct((nb, t, D_MODEL), F32),
        compiler_params=_cparams(("arbitrary", "arbitrary")),
        name="post",
    )(x, mods, og, od, gg, gd, *wts)


def kernel(x_prompt, x_sample, c_prompt, c_sample, cache_k, cache_v, page_table, state_conv, state_gdn, rel_bias,
           w_in, w_conv, a_log, dt_bias, gdn_norm_g, lam_q1, lam_k1, lam_q2, lam_k2, diff_norm_g, w_br_gdn,
           w_br_diff, w_o, w_ada, b_ada, ln1_g, ln1_b, ln2_g, ln2_b, w_up, b_up, w_down, b_down):
    assert DEPTH == 1 and w_in.shape[0] == 1
    l = 0
    lam_init = 0.8 - 0.6 * math.exp(-0.3 * l)
    nbp, seq, _ = x_prompt.shape
    nbs, dec_seq, _ = x_sample.shape
    n_tok_s = nbs * dec_seq
    past_len = page_table.shape[1] * PAGE_SIZE
    assert seq % Q_TILE == 0 and CONV_W - 1 <= dec_seq and 2 * dec_seq == SUBLANES
    assert page_table.shape[1] % PAGES_PER_STEP == 0 and nbs % (8 * SEQ_PER_ITER) == 0

    offs = np.concatenate([[0], np.cumsum(IN_SPLITS)])
    col = lambda i: w_in[l][:, int(offs[i]):int(offs[i + 1])]
    w_main = jnp.concatenate([col(0), col(1), col(4), col(5), col(6), col(7), col(8)], axis=1).astype(BF16)
    w_small = jnp.concatenate([col(2), col(3), jnp.zeros((D_MODEL, LANES - 2 * H_GDN), F32)], axis=1).astype(BF16)
    pad_gate = lambda a: jnp.zeros((1, LANES), F32).at[0, H_GDN:2 * H_GDN].set(a)
    alog_pad, dtb_pad = pad_gate(a_log[l]), pad_gate(dt_bias[l])
    gng = gdn_norm_g[l].reshape(1, DV_GDN)
    diff_g = diff_norm_g[l].reshape(1, HEAD_W)
    lam_w = (lam_q1[l].reshape(1, -1), lam_k1[l].reshape(1, -1), lam_q2[l].reshape(1, -1), lam_k2[l].reshape(1, -1))
    r2 = lambda a: a.reshape(1, -1)
    post_w = (w_br_gdn[l].astype(BF16), w_br_diff[l].astype(BF16), w_o[l].astype(BF16), r2(ln1_g[l]), r2(ln1_b[l]),
              w_up[l].astype(BF16), r2(b_up[l]), w_down[l].astype(BF16), r2(b_down[l]), r2(ln2_g[l]), r2(ln2_b[l]))

    ada = _ada(jnp.concatenate([c_prompt, c_sample], axis=0), w_ada[l].astype(BF16), r2(b_ada[l]))
    mods_p = ada[:nbp].reshape(nbp, 6, D_MODEL)
    mods_s = jnp.repeat(ada[nbp:].reshape(nbs, 6, D_MODEL), dec_seq, axis=0).transpose(1, 0, 2)
    bias_p, bias_last, bias_new, far = _bias_prep(rel_bias, past_len, dec_seq)

    conv_p, z_p, q_p, k_p, v_p, gg_p, gd_p, ba_p = _inproj(x_prompt, mods_p, w_main, w_small, 256, False)
    og_p, s_p = _gdn_prompt(conv_p, z_p, ba_p, w_conv[l], alog_pad, dtb_pad, gng, 4 * CHUNK)
    od_p = _attn_prompt(rel_bias, lam_w, diff_g, bias_p, q_p, k_p, v_p, lam_init)
    y_p = _post(x_prompt, mods_p, og_p, od_p, gg_p, gd_p, post_w, 256, False)

    xs = x_sample.reshape(1, n_tok_s, D_MODEL)
    conv_s, z_s, q_s, k_s, v_s, gg_s, gd_s, ba_s = _inproj(xs, mods_s, w_main, w_small, n_tok_s, True)
    per_seq = lambda a: a.reshape(nbs, dec_seq, a.shape[-1])
    pad8 = lambda a: jnp.pad(a.astype(F32), ((0, 0), (0, SUBLANES - dec_seq), (0, 0)))
    conv_s3 = per_seq(conv_s)
    upx = jnp.concatenate([jnp.zeros((nbs, SUBLANES - (CONV_W - 1), GDN_CONV_CH), F32), state_conv[l], conv_s3,
                           jnp.zeros((nbs, SUBLANES - dec_seq, GDN_CONV_CH), F32)], axis=1)
    og_s8, s_s = _gdn_sample(upx, pad8(per_seq(z_s)), pad8(per_seq(ba_s)), state_gdn[l], w_conv[l], alog_pad,
                             dtb_pad, gng, dec_seq, 8)
    ck = jnp.transpose(cache_k[l], (0, 2, 3, 4, 1)).reshape(cache_k.shape[1], DIFF_QK, PAGE_SIZE)
    cv = cache_v[l].reshape(cache_v.shape[1], PAGE_SIZE * H_DIFF, HEAD_W)
    q_s3 = per_seq(q_s).astype(F32)
    od_s8 = _attn_sample(page_table, lam_w, diff_g, bias_last, bias_new, far, jnp.concatenate([q_s3, q_s3], axis=1),
                         pad8(per_seq(k_s)), pad8(per_seq(v_s)), ck, cv, lam_init, dec_seq)
    flat = lambda a8: a8[:, :dec_seq].reshape(1, n_tok_s, a8.shape[-1]).astype(BF16)
    y_s = _post(xs, mods_s, flat(og_s8), flat(od_s8), gg_s, gd_s, post_w, 256, True)

    hd = lambda a, b: a.reshape(1, b, -1, H_DIFF, 2, DH_DIFF)
    hv = lambda a, b: a.reshape(1, b, -1, H_DIFF, 2 * DH_DIFF)
    return (y_p, y_s.reshape(nbs, dec_seq, D_MODEL),
            hd(k_p, nbp), hv(v_p, nbp), conv_p[:, seq - (CONV_W - 1):][None], s_p[None],
            hd(k_s, nbs), hv(v_s, nbs), conv_s3[:, dec_seq - (CONV_W - 1):][None], s_s[None])
```

```python
import functools
import math

import numpy as np
import jax
import jax.numpy as jnp
from jax import lax
from jax.experimental import pallas as pl
from jax.experimental.pallas import tpu as pltpu

F32 = jnp.float32
BF16 = jnp.bfloat16

D_MODEL = 1024
DEPTH = 1
PAGE_SIZE = 128
H_GDN = 4
DK_GDN = 128
DV_GDN = 128
CONV_W = 4
CHUNK = 64
GDN_NORM_EPS = 1e-6
H_DIFF = 4
DH_DIFF = 64
DIFF_NORM_EPS = 1e-5
N_BUCKETS = 32
MAX_DISTANCE = 128
D_FF = 4 * D_MODEL
LN_EPS = 1e-5
GDN_QK = H_GDN * DK_GDN
GDN_V = H_GDN * DV_GDN
GDN_CONV_CH = 2 * GDN_QK + GDN_V
DIFF_QK = H_DIFF * 2 * DH_DIFF
DIFF_V = H_DIFF * 2 * DH_DIFF
HEAD_W = 2 * DH_DIFF
IN_SPLITS = (GDN_CONV_CH, GDN_V, H_GDN, H_GDN, DIFF_QK, DIFF_QK, DIFF_V, D_MODEL, D_MODEL)
ALPHA = (2.0 * DEPTH) ** 0.25

LANES = 128
SUBLANES = 8
VMEM_LIMIT = 56 * 1024 * 1024

Q_TILE = 256
PAGES_PER_STEP = 16
SEQ_PER_ITER = 2
MASKED = N_BUCKETS

NN = (((1,), (0,)), ((), ()))
NT = (((1,), (1,)), ((), ()))
TN = (((0,), (0,)), ((), ()))


def _cparams(sem):
    return pltpu.CompilerParams(dimension_semantics=sem, vmem_limit_bytes=VMEM_LIMIT)


def _const_spec(shape):
    nd = len(shape)
    return pl.BlockSpec(shape, lambda *_: (0,) * nd, pipeline_mode=pl.Buffered(1))


def _sigmoid(x):
    return 1.0 / (1.0 + jnp.exp(-x))


def _silu(x):
    return x * _sigmoid(x)


def _softplus(x):
    return jnp.maximum(x, 0.0) + jnp.log(1.0 + jnp.exp(-jnp.abs(x)))


def _ln(x):
    mu = jnp.mean(x, axis=-1, keepdims=True)
    xc = x - mu
    var = jnp.mean(xc * xc, axis=-1, keepdims=True)
    return xc * lax.rsqrt(var + LN_EPS)


def _dot(a, b, dims=NN):
    return lax.dot_general(a, b, dims, preferred_element_type=F32)


def _dot_bf(a, b, dims=NN):
    return _dot(a.astype(BF16), b.astype(BF16), dims)


def _split(a):
    hi = a.astype(BF16)
    lo = (a - hi.astype(F32)).astype(BF16)
    return hi, lo


def _dot3(a, b, dims=NN, fuse=True):
    ah, al = _split(a)
    bh, bl = _split(b)
    if not fuse:
        return _dot(ah, bh, dims) + (_dot(ah, bl, dims) + _dot(al, bh, dims))
    lhs = jnp.concatenate([ah, ah, al], axis=1)
    rhs = jnp.concatenate([bh, bl, bh], axis=1 if dims == NT else 0)
    return _dot(lhs, rhs, dims)


def _dot_f32(a, b, dims=NN):
    return lax.dot_general(a, b, dims, precision=lax.Precision.HIGHEST, preferred_element_type=F32)


def _mod(mod_ref, i, per_token):
    return mod_ref[i] if per_token else mod_ref[i:i + 1, :]


def _ada_body(c_ref, w_ref, b_ref, o_ref):
    s = _silu(c_ref[...])
    o_ref[...] = _dot(s.astype(BF16), w_ref[...]) + b_ref[...]


def _ada(c_all, w_ada, b_ada):
    n = c_all.shape[0]
    return pl.pallas_call(
        _ada_body,
        grid=(6,),
        in_specs=[pl.BlockSpec((n, D_MODEL), lambda j: (0, 0)),
                  pl.BlockSpec((D_MODEL, D_MODEL), lambda j: (0, j)),
                  pl.BlockSpec((1, D_MODEL), lambda j: (0, j))],
        out_specs=pl.BlockSpec((n, D_MODEL), lambda j: (0, j)),
        out_shape=jax.ShapeDtypeStruct((n, 6 * D_MODEL), F32),
        compiler_params=_cparams(("arbitrary",)),
        name="ada",
    )(c_all, w_ada, b_ada)


def _t5_bucket_np(rel):
    n = np.maximum(rel, 0)
    max_exact = N_BUCKETS // 2
    nf = np.maximum(n, 1).astype(np.float32)
    large = max_exact + (np.log(nf / np.float32(max_exact)) / np.float32(math.log(MAX_DISTANCE / max_exact))
                         * np.float32(N_BUCKETS - max_exact)).astype(np.int32)
    large = np.minimum(large, N_BUCKETS - 1)
    return np.where(n < max_exact, n, large).astype(np.int32)


def _bias_codes(past_len, dec_seq):
    r = np.arange(Q_TILE)[:, None]
    c = np.arange(Q_TILE)[None, :]
    prev_tile = _t5_bucket_np(Q_TILE + r - c)
    diag_tile = np.where(r >= c, _t5_bucket_np(r - c), MASKED)
    prompt = np.concatenate([prev_tile, diag_tile], axis=1).astype(np.int32)
    prompt = np.concatenate([prompt, prompt], axis=0)
    nrow = H_DIFF * SUBLANES
    qrow = (np.arange(nrow) % dec_seq)[:, None]
    nkeys = PAGES_PER_STEP * PAGE_SIZE
    kpos = past_len - nkeys + np.arange(nkeys)[None, :]
    last = _t5_bucket_np(past_len + qrow - kpos).astype(np.int32)
    j = np.arange(SUBLANES)[None, :]
    new = np.where((j <= qrow) & (j < dec_seq), _t5_bucket_np(qrow - j), MASKED).astype(np.int32)
    return prompt, last, new


def _bias_body(tbl_ref, cp_ref, cl_ref, cn_ref, bp_ref, bl_ref, bn_ref, far_ref):
    def lookup(codes, h):
        acc = jnp.full(codes.shape, -jnp.inf, F32)
        for b in range(N_BUCKETS):
            acc = jnp.where(codes == b, tbl_ref[b, h], acc)
        return acc

    cp = cp_ref[...]
    for h in range(H_DIFF):
        bp_ref[h] = lookup(cp, h) - tbl_ref[N_BUCKETS - 1, h]
    cl = cl_ref[...]
    cn = cn_ref[...]
    nrow = cl.shape[0]
    row_head = lax.broadcasted_iota(jnp.int32, (nrow, 1), 0) // SUBLANES
    bl = jnp.zeros(cl.shape, F32)
    bn = jnp.zeros(cn.shape, F32)
    far = jnp.zeros((nrow, LANES), F32)
    for h in range(H_DIFF):
        bl = jnp.where(row_head == h, lookup(cl, h), bl)
        bn = jnp.where(row_head == h, lookup(cn, h), bn)
        far = jnp.where(row_head == h, tbl_ref[N_BUCKETS - 1, h], far)
    bl_ref[...] = bl
    bn_ref[...] = bn
    far_ref[...] = far


def _bias_prep(rel_bias, past_len, dec_seq):
    cp, cl, cn = _bias_codes(past_len, dec_seq)
    nrow = cl.shape[0]
    vm = pl.BlockSpec(memory_space=pltpu.VMEM)
    return pl.pallas_call(
        _bias_body,
        in_specs=[pl.BlockSpec(memory_space=pltpu.SMEM), vm, vm, vm],
        out_specs=[vm, vm, vm, vm],
        out_shape=[jax.ShapeDtypeStruct((H_DIFF,) + cp.shape, F32),
                   jax.ShapeDtypeStruct(cl.shape, F32),
                   jax.ShapeDtypeStruct(cn.shape, F32),
                   jax.ShapeDtypeStruct((nrow, LANES), F32)],
        name="bias_prep",
    )(rel_bias, jnp.asarray(cp), jnp.asarray(cl), jnp.asarray(cn))


def _lam(lq1_ref, lk1_ref, lq2_ref, lk2_ref, lam_init):
    s1 = jnp.sum(lq1_ref[...] * lk1_ref[...], axis=-1, keepdims=True)
    s2 = jnp.sum(lq2_ref[...] * lk2_ref[...], axis=-1, keepdims=True)
    return jnp.exp(s1) - jnp.exp(s2) + lam_init


_SEG = {"conv": (0, 1536), "z": (1536, 2048), "q": (2048, 2560), "k": (2560, 3072), "v": (3072, 3584),
        "gg": (3584, 4608), "gd": (4608, 5632)}
N_MAIN = 5632


def _inproj_body(x_ref, mod_ref, wm_ref, ws_ref, conv_ref, z_ref, q_ref, k_ref, v_ref, gg_ref, gd_ref, ba_ref,
                 *, per_token):
    h = _ln(x_ref[...]) * (1.0 + _mod(mod_ref, 1, per_token)) + _mod(mod_ref, 0, per_token)
    hb = h.astype(BF16)

    def seg(name):
        a, b = _SEG[name]
        return _dot(hb, wm_ref[:, a:b])

    conv_ref[...] = seg("conv")
    z_ref[...] = seg("z").astype(BF16)
    q_ref[...] = seg("q").astype(BF16)
    k_ref[...] = seg("k")
    v_ref[...] = seg("v")
    gg_ref[...] = seg("gg").astype(BF16)
    gd_ref[...] = seg("gd").astype(BF16)
    ba_ref[...] = _dot(hb, ws_ref[...])


def _inproj(x, mods, w_main, w_small, tm, per_token):
    nb, t, _ = x.shape
    if per_token:
        mod_spec = pl.BlockSpec((6, tm, D_MODEL), lambda b, i: (0, i, 0))
    else:
        mod_spec = pl.BlockSpec((None, 6, D_MODEL), lambda b, i: (b, 0, 0))

    def out(n, dt):
        return pl.BlockSpec((None, tm, n), lambda b, i: (b, i, 0)), jax.ShapeDtypeStruct((nb, t, n), dt)

    outs = [out(GDN_CONV_CH, F32), out(GDN_V, BF16), out(DIFF_QK, BF16), out(DIFF_QK, F32), out(DIFF_V, F32),
            out(D_MODEL, BF16), out(D_MODEL, BF16), out(LANES, F32)]
    return pl.pallas_call(
        functools.partial(_inproj_body, per_token=per_token),
        grid=(nb, t // tm),
        in_specs=[pl.BlockSpec((None, tm, D_MODEL), lambda b, i: (b, i, 0)), mod_spec,
                  _const_spec((D_MODEL, N_MAIN)), _const_spec((D_MODEL, LANES))],
        out_specs=[o[0] for o in outs],
        out_shape=[o[1] for o in outs],
        compiler_params=_cparams(("arbitrary", "arbitrary")),
        name="inproj",
    )(x, mods, w_main, w_small)


def _gdn_solve(chains, c):
    ii = lax.broadcasted_iota(jnp.int32, (c, c), 0)
    jj = lax.broadcasted_iota(jnp.int32, (c, c), 1)
    incl = ii >= jj
    eye = (ii == jj).astype(F32)
    fuse = c % CHUNK == 0
    gamma = [jnp.where(incl, jnp.exp(jnp.where(incl, ch[4] - ch[5], 0.0)), 0.0) for ch in chains]
    kk = [_dot3(ch[1], ch[1], NT) for ch in chains]
    a = [jnp.where(ii > jj, ch[3] * k * g, 0.0) for ch, k, g in zip(chains, kk, gamma)]
    a8 = [jnp.where((ii >> 3) == (jj >> 3), m, 0.0) for m in a]
    p = [_dot3(m, m, fuse=fuse) for m in a8]
    x = [eye - m for m in a8]
    x = [xi + _dot3(xi, pi, fuse=fuse) for xi, pi in zip(x, p)]
    p = [_dot3(pi, pi, fuse=fuse) for pi in p]
    x = [xi + _dot3(xi, pi, fuse=fuse) for xi, pi in zip(x, p)]
    sh = 3
    while (1 << sh) < c:
        lvl = ((ii >> (sh + 1)) == (jj >> (sh + 1))) & ((ii >> sh) != (jj >> sh))
        t = [_dot3(jnp.where(lvl, m, 0.0), xi, fuse=fuse) for m, xi in zip(a, x)]
        x = [xi - _dot3(xi, ti, fuse=fuse) for xi, ti in zip(x, t)]
        sh += 1
    egc = [jnp.exp(ch[4]) for ch in chains]
    uw = [_dot3(xi, jnp.concatenate([ch[2] * ch[3], ch[1] * (ch[3] * e)], axis=1), fuse=fuse)
          for xi, ch, e in zip(x, chains, egc)]
    qk = [jnp.where(incl, _dot_bf(ch[0], ch[1], NT) * g, 0.0) for ch, g in zip(chains, gamma)]
    out = []
    for ch, e, uwi, qki in zip(chains, egc, uw, qk):
        g_last = ch[4][c - 1:c, :]
        wq = jnp.concatenate([uwi[:, DV_GDN:], ch[0] * e], axis=0).astype(BF16)
        k_dec = (ch[1] * jnp.exp(g_last - ch[4])).astype(BF16)
        out.append((uwi[:, :DV_GDN], wq, qki.astype(BF16), k_dec, jnp.exp(g_last)))
    return out


def _gdn_state_step(sols, states, c):
    sb = [s.astype(BF16) for s in states]
    wq = [_dot(sol[1], b) for sol, b in zip(sols, sb)]
    vb = [(sol[0] - m[:c]).astype(BF16) for sol, m in zip(sols, wq)]
    o = [m[c:] + _dot(sol[2], v) for sol, m, v in zip(sols, wq, vb)]
    s_new = [sol[4] * s + _dot(sol[3], v, TN) for sol, s, v in zip(sols, states, vb)]
    return o, s_new


def _l2n(x):
    return x * lax.rsqrt(jnp.sum(x * x, axis=-1, keepdims=True) + 1e-6)


def _gdn_gates(ba, alog_ref, dtb_ref):
    beta = _sigmoid(ba)
    g = -jnp.exp(alog_ref[...]) * _softplus(ba + dtb_ref[...])
    return beta, g


def _gdn_out(o, z, gng):
    o = o * lax.rsqrt(jnp.mean(o * o, axis=-1, keepdims=True) + GDN_NORM_EPS) * gng
    return o * _silu(z)


def _cumsum_mats(c):
    ii = lax.broadcasted_iota(jnp.int32, (c, c), 0)
    jj = lax.broadcasted_iota(jnp.int32, (c, c), 1)
    return (ii >= jj).astype(F32)


def _gdn_prompt_body(conv_ref, z_ref, ba_ref, wc_ref, alog_ref, dtb_ref, gng_ref, og_ref, s_ref, cbuf_ref, *, tt):
    t = pl.program_id(1)

    @pl.when(t == 0)
    def _():
        s_ref[...] = jnp.zeros(s_ref.shape, F32)
        cbuf_ref[0:SUBLANES, :] = jnp.zeros((SUBLANES, GDN_CONV_CH), F32)

    u_in = conv_ref[...]
    cbuf_ref[SUBLANES:, :] = u_in
    y = cbuf_ref[SUBLANES:, :] * wc_ref[CONV_W - 1:CONV_W, :]
    for j in range(CONV_W - 1):
        off = SUBLANES - (CONV_W - 1) + j
        y = y + cbuf_ref[off:off + tt, :] * wc_ref[j:j + 1, :]
    cbuf_ref[0:SUBLANES, :] = u_in[tt - SUBLANES:, :]
    qkv = _silu(y)

    beta_all, g_all = _gdn_gates(ba_ref[...], alog_ref, dtb_ref)
    ltri = _cumsum_mats(CHUNK)
    nch = tt // CHUNK
    gc_all = jnp.concatenate([_dot_f32(ltri, g_all[c * CHUNK:(c + 1) * CHUNK, :]) for c in range(nch)], axis=0)
    gc_t = gc_all.T
    gng = gng_ref[...]
    chains = []
    for c in range(nch):
        rows = slice(c * CHUNK, (c + 1) * CHUNK)
        for h in range(H_GDN):
            lo = h * DK_GDN
            chains.append((_l2n(qkv[rows, lo:lo + DK_GDN]) * (DK_GDN ** -0.5),
                           _l2n(qkv[rows, GDN_QK + lo:GDN_QK + lo + DK_GDN]),
                           qkv[rows, 2 * GDN_QK + lo:2 * GDN_QK + lo + DV_GDN],
                           beta_all[rows, h:h + 1], gc_all[rows, H_GDN + h:H_GDN + h + 1],
                           gc_t[H_GDN + h:H_GDN + h + 1, rows]))
    sols = _gdn_solve(chains, CHUNK)
    states = [s_ref[h] for h in range(H_GDN)]
    for c in range(nch):
        rows = slice(c * CHUNK, (c + 1) * CHUNK)
        o, states = _gdn_state_step(sols[c * H_GDN:(c + 1) * H_GDN], states, CHUNK)
        for h in range(H_GDN):
            cols = slice(h * DV_GDN, (h + 1) * DV_GDN)
            og_ref[rows, cols] = _gdn_out(o[h], z_ref[rows, cols].astype(F32), gng).astype(og_ref.dtype)
    for h in range(H_GDN):
        s_ref[h] = states[h]


def _gdn_prompt(conv_in, z, ba, w_conv, alog_pad, dtb_pad, gng, tt):
    nb, t, _ = conv_in.shape
    row = lambda n: pl.BlockSpec((None, tt, n), lambda b, i: (b, i, 0))
    return pl.pallas_call(
        functools.partial(_gdn_prompt_body, tt=tt),
        grid=(nb, t // tt),
        in_specs=[row(GDN_CONV_CH), row(GDN_V), row(LANES),
                  _const_spec((CONV_W, GDN_CONV_CH)), _const_spec((1, LANES)), _const_spec((1, LANES)),
                  _const_spec((1, DV_GDN))],
        out_specs=[row(GDN_V), pl.BlockSpec((None, H_GDN, DK_GDN, DV_GDN), lambda b, i: (b, 0, 0, 0))],
        out_shape=[jax.ShapeDtypeStruct((nb, t, GDN_V), BF16),
                   jax.ShapeDtypeStruct((nb, H_GDN, DK_GDN, DV_GDN), F32)],
        scratch_shapes=[pltpu.VMEM((tt + SUBLANES, GDN_CONV_CH), F32)],
        compiler_params=_cparams(("arbitrary", "arbitrary")),
        name="gdn_prompt",
    )(conv_in, z, ba, w_conv, alog_pad, dtb_pad, gng)


def _gdn_sample_body(upx_ref, z_ref, ba_ref, s0_ref, wc_ref, alog_ref, dtb_ref, gng_ref, og_ref, s_ref,
                     *, nb, n_valid):
    c = SUBLANES
    valid = lax.broadcasted_iota(jnp.int32, (c, 1), 0) < n_valid
    ltri = _cumsum_mats(c)
    gng = gng_ref[...]

    def body(i, carry):
        chains, states = [], []
        for n in (SEQ_PER_ITER * i + k for k in range(SEQ_PER_ITER)):
            y = upx_ref[n, pl.ds(SUBLANES, c), :] * wc_ref[CONV_W - 1:CONV_W, :]
            for j in range(CONV_W - 1):
                off = SUBLANES - (CONV_W - 1) + j
                y = y + upx_ref[n, pl.ds(off, c), :] * wc_ref[j:j + 1, :]
            qkv = _silu(y)
            beta_all, g_all = _gdn_gates(ba_ref[n], alog_ref, dtb_ref)
            beta_all = jnp.where(valid, beta_all, 0.0)
            g_all = jnp.where(valid, g_all, 0.0)
            gc_all = _dot_f32(ltri, g_all)
            gc_t = jnp.concatenate([gc_all, jnp.zeros((LANES - c, LANES), F32)], axis=0).T
            for h in range(H_GDN):
                lo = h * DK_GDN
                chains.append((_l2n(qkv[:, lo:lo + DK_GDN]) * (DK_GDN ** -0.5),
                               jnp.where(valid, _l2n(qkv[:, GDN_QK + lo:GDN_QK + lo + DK_GDN]), 0.0),
                               jnp.where(valid, qkv[:, 2 * GDN_QK + lo:2 * GDN_QK + lo + DV_GDN], 0.0),
                               beta_all[:, h:h + 1], gc_all[:, H_GDN + h:H_GDN + h + 1],
                               gc_t[H_GDN + h:H_GDN + h + 1, 0:c]))
                states.append(s0_ref[n, h])
        o, states = _gdn_state_step(_gdn_solve(chains, c), states, c)
        for k in range(SEQ_PER_ITER):
            n = SEQ_PER_ITER * i + k
            for h in range(H_GDN):
                cols = slice(h * DV_GDN, (h + 1) * DV_GDN)
                s_ref[n, h] = states[k * H_GDN + h]
                og_ref[n, :, cols] = _gdn_out(o[k * H_GDN + h], z_ref[n, :, cols], gng)
        return carry

    lax.fori_loop(0, nb // SEQ_PER_ITER, body, 0)


def _gdn_sample(upx, z8, ba8, s0, w_conv, alog_pad, dtb_pad, gng, n_valid, nb):
    n = upx.shape[0]
    blk = lambda *shape: pl.BlockSpec((nb,) + shape, lambda i: (i,) + (0,) * len(shape))
    return pl.pallas_call(
        functools.partial(_gdn_sample_body, nb=nb, n_valid=n_valid),
        grid=(n // nb,),
        in_specs=[blk(2 * SUBLANES, GDN_CONV_CH), blk(SUBLANES, GDN_V), blk(SUBLANES, LANES),
                  blk(H_GDN, DK_GDN, DV_GDN),
                  _const_spec((CONV_W, GDN_CONV_CH)), _const_spec((1, LANES)), _const_spec((1, LANES)),
                  _const_spec((1, DV_GDN))],
        out_specs=[blk(SUBLANES, GDN_V), blk(H_GDN, DK_GDN, DV_GDN)],
        out_shape=[jax.ShapeDtypeStruct((n, SUBLANES, GDN_V), F32),
                   jax.ShapeDtypeStruct((n, H_GDN, DK_GDN, DV_GDN), F32)],
        compiler_params=_cparams(("arbitrary",)),
        name="gdn_sample",
    )(upx, z8, ba8, s0, w_conv, alog_pad, dtb_pad, gng)


def _diff_norm(o, g, lam_init):
    return o * lax.rsqrt(jnp.mean(o * o, axis=-1, keepdims=True) + DIFF_NORM_EPS) * g * (1.0 - lam_init)


def _attn_prompt_body(lq1_ref, lk1_ref, lq2_ref, lk2_ref, g_ref, bias_ref, q_ref, k_ref, v_ref, o_ref,
                      *, lam_init, t):
    lam = _lam(lq1_ref, lk1_ref, lq2_ref, lk2_ref, lam_init)
    kb = k_ref[...].astype(BF16)
    vb = v_ref[...].astype(BF16)
    g = g_ref[...]
    lane = lax.broadcasted_iota(jnp.int32, (1, HEAD_W), 1)
    zero = jnp.zeros((), BF16)
    for i in range(t // Q_TILE):
        r0 = i * Q_TILE
        qi = q_ref[r0:r0 + Q_TILE, :] * jnp.asarray(DH_DIFF ** -0.5, BF16)
        q2 = jnp.concatenate([jnp.where(lane < DH_DIFF, qi, zero), jnp.where(lane >= DH_DIFF, qi, zero)], axis=0)
        near0 = max(i - 1, 0) * Q_TILE
        sn = _dot(q2, kb[near0:r0 + Q_TILE], NT) + bias_ref[:, 2 * Q_TILE - (r0 + Q_TILE - near0):]
        m = jnp.max(sn, axis=-1, keepdims=True)
        if near0 > 0:
            sf = _dot(q2, kb[:near0], NT)
            m = jnp.maximum(m, jnp.max(sf, axis=-1, keepdims=True))
        en = jnp.exp(sn - m)
        l = jnp.sum(en, axis=-1, keepdims=True)
        o2 = _dot(en.astype(BF16), vb[near0:r0 + Q_TILE])
        if near0 > 0:
            ef = jnp.exp(sf - m)
            l = l + jnp.sum(ef, axis=-1, keepdims=True)
            o2 = o2 + _dot(ef.astype(BF16), vb[:near0])
        o2 = o2 * (1.0 / l)
        o = o2[:Q_TILE] - lam * o2[Q_TILE:]
        o_ref[r0:r0 + Q_TILE, :] = _diff_norm(o, g, lam_init).astype(o_ref.dtype)


def _attn_prompt(lam_w, diff_g, bias_p, q, k, v, lam_init):
    nb, t, _ = q.shape
    head = lambda: pl.BlockSpec((None, t, HEAD_W), lambda b, h: (b, 0, h))
    small = _const_spec((1, DH_DIFF))
    return pl.pallas_call(
        functools.partial(_attn_prompt_body, lam_init=lam_init, t=t),
        grid=(nb, H_DIFF),
        in_specs=[small, small, small, small, _const_spec((1, HEAD_W)),
                  pl.BlockSpec((None, 2 * Q_TILE, 2 * Q_TILE), lambda b, h: (h, 0, 0)),
                  head(), head(), head()],
        out_specs=head(),
        out_shape=jax.ShapeDtypeStruct((nb, t, DIFF_V), BF16),
        compiler_params=_cparams(("arbitrary", "arbitrary")),
        name="attn_prompt",
    )(*lam_w, diff_g, bias_p, q, k, v)


def _attn_sample_body(pt_ref, lq1_ref, lk1_ref, lq2_ref, lk2_ref, g_ref, bl_ref, bn_ref, far_ref,
                      q_ref, kn_ref, vn_ref, ck_hbm, cv_hbm, o_ref,
                      kbuf, vbuf, sem, m_ref, l_ref, acc_ref, *, lam_init, n_chunks, dec_seq):
    c = pl.program_id(1)
    step = pl.program_id(0) * n_chunks + c
    total = pl.num_programs(0) * n_chunks
    nrow = H_DIFF * SUBLANES
    nkeys = PAGES_PER_STEP * PAGE_SIZE
    vrows = PAGE_SIZE * H_DIFF

    def copies(stp, slot):
        out = []
        for p in range(PAGES_PER_STEP):
            page = pt_ref[stp * PAGES_PER_STEP + p]
            out.append(pltpu.make_async_copy(ck_hbm.at[page], kbuf.at[slot, :, pl.ds(p * PAGE_SIZE, PAGE_SIZE)],
                                             sem.at[0, slot]))
            out.append(pltpu.make_async_copy(cv_hbm.at[page], vbuf.at[slot, pl.ds(p * vrows, vrows)],
                                             sem.at[1, slot]))
        return out

    @pl.when(step == 0)
    def _():
        for cp in copies(0, 0):
            cp.start()

    @pl.when(step + 1 < total)
    def _():
        for cp in copies(step + 1, (step + 1) % 2):
            cp.start()

    lane_hc = lax.broadcasted_iota(jnp.int32, (SUBLANES, DIFF_QK), 1) // DH_DIFF
    row_c = lax.broadcasted_iota(jnp.int32, (SUBLANES, DIFF_QK), 0) // dec_seq
    q2 = q_ref[...] * (DH_DIFF ** -0.5)
    qbd = jnp.concatenate([jnp.where(lane_hc == 2 * h + row_c, q2, 0.0) for h in range(H_DIFF)],
                          axis=0).astype(BF16)
    heads = [slice(h * SUBLANES, (h + 1) * SUBLANES) for h in range(H_DIFF)]
    hcols = [slice(h * HEAD_W, (h + 1) * HEAD_W) for h in range(H_DIFF)]

    @pl.when(c == 0)
    def _():
        s = _dot(qbd, kn_ref[...].astype(BF16), NT) + bn_ref[...]
        m = jnp.max(s, axis=-1, keepdims=True)
        p = jnp.exp(s - m)
        m_ref[...] = m
        l_ref[...] = jnp.sum(p, axis=-1, keepdims=True)
        pb = p.astype(BF16)
        vn = vn_ref[...].astype(BF16)
        for h in range(H_DIFF):
            acc_ref[heads[h], :] = _dot(pb[heads[h]], vn[:, hcols[h]])

    slot = step % 2
    for cp in copies(step, slot):
        cp.wait()

    bias = jnp.where(c == n_chunks - 1, bl_ref[...], far_ref[:, 0:1])
    s = _dot(qbd, kbuf[slot].astype(BF16)) + bias
    m_old = m_ref[...]
    m_new = jnp.maximum(m_old, jnp.max(s, axis=-1, keepdims=True))
    alpha = jnp.exp(m_old - m_new)
    p = jnp.exp(s - m_new)
    m_ref[...] = m_new
    l_ref[...] = alpha * l_ref[...] + jnp.sum(p, axis=-1, keepdims=True)
    pb = p.astype(BF16)
    for h in range(H_DIFF):
        v_h = vbuf[slot, pl.ds(h, nkeys, stride=H_DIFF), :].astype(BF16)
        acc_ref[heads[h], :] = alpha[heads[h]] * acc_ref[heads[h], :] + _dot(pb[heads[h]], v_h)

    @pl.when(c == n_chunks - 1)
    def _():
        lam = _lam(lq1_ref, lk1_ref, lq2_ref, lk2_ref, lam_init)
        row = lax.broadcasted_iota(jnp.int32, (nrow, 1), 0)
        coef = jnp.where((row % SUBLANES) < dec_seq, 1.0, -lam) / l_ref[...]
        ri = lax.broadcasted_iota(jnp.int32, (nrow, nrow), 0)
        rj = lax.broadcasted_iota(jnp.int32, (nrow, nrow), 1)
        sel = ((ri // SUBLANES == rj // SUBLANES) & (ri % SUBLANES == rj % dec_seq)).astype(F32)
        o = _dot_f32(sel, acc_ref[...] * coef)
        g = g_ref[...]
        o_ref[...] = jnp.concatenate([_diff_norm(o[heads[h]], g, lam_init) for h in range(H_DIFF)], axis=1)


def _attn_sample(page_table, lam_w, diff_g, bias_last, bias_new, far, q8, kn8, vn8, cache_k, cache_v, lam_init,
                 dec_seq):
    n, n_pages = page_table.shape
    n_chunks = n_pages // PAGES_PER_STEP
    nrow = H_DIFF * SUBLANES
    nkeys = PAGES_PER_STEP * PAGE_SIZE
    c2 = lambda shape: pl.BlockSpec(shape, lambda s, c, pt: (0,) * len(shape))
    per_seq = pl.BlockSpec((None, SUBLANES, DIFF_QK), lambda s, c, pt: (s, 0, 0))
    grid_spec = pltpu.PrefetchScalarGridSpec(
        num_scalar_prefetch=1,
        grid=(n, n_chunks),
        in_specs=[c2((1, DH_DIFF)), c2((1, DH_DIFF)), c2((1, DH_DIFF)), c2((1, DH_DIFF)), c2((1, HEAD_W)),
                  c2((nrow, nkeys)), c2((nrow, SUBLANES)), c2((nrow, LANES)),
                  per_seq, per_seq, per_seq,
                  pl.BlockSpec(memory_space=pl.ANY), pl.BlockSpec(memory_space=pl.ANY)],
        out_specs=per_seq,
        scratch_shapes=[pltpu.VMEM((2, DIFF_QK, nkeys), F32), pltpu.VMEM((2, nkeys * H_DIFF, HEAD_W), F32),
                        pltpu.SemaphoreType.DMA((2, 2)),
                        pltpu.VMEM((nrow, 1), F32), pltpu.VMEM((nrow, 1), F32), pltpu.VMEM((nrow, HEAD_W), F32)],
    )
    return pl.pallas_call(
        functools.partial(_attn_sample_body, lam_init=lam_init, n_chunks=n_chunks, dec_seq=dec_seq),
        grid_spec=grid_spec,
        out_shape=jax.ShapeDtypeStruct((n, SUBLANES, DIFF_V), F32),
        compiler_params=_cparams(("arbitrary", "arbitrary")),
        name="attn_sample",
    )(page_table.reshape(-1), *lam_w, diff_g, bias_last, bias_new, far, q8, kn8, vn8, cache_k, cache_v)


def _post_body(x_ref, mod_ref, og_ref, od_ref, gg_ref, gd_ref, wbg_ref, wbd_ref, wo_ref, l1g_ref, l1b_ref,
               wup_ref, bup_ref, wdn_ref, bdn_ref, l2g_ref, l2b_ref, y_ref, *, per_token):
    md = lambda i: _mod(mod_ref, i, per_token)
    m = (_sigmoid(gg_ref[...].astype(F32)) * _dot(og_ref[...], wbg_ref[...])
         + _sigmoid(gd_ref[...].astype(F32)) * _dot(od_ref[...], wbd_ref[...]))
    mix = _dot(m.astype(BF16), wo_ref[...])
    x1 = _ln(ALPHA * x_ref[...] + md(2) * mix) * l1g_ref[...] + l1b_ref[...]
    h2 = _ln(x1) * (1.0 + md(4)) + md(3)
    up = _dot(h2.astype(BF16), wup_ref[...]) + bup_ref[...]
    act = jnp.square(jnp.maximum(up, 0.0))
    f = _dot(act.astype(BF16), wdn_ref[...]) + bdn_ref[...]
    y_ref[...] = _ln(ALPHA * x1 + md(5) * f) * l2g_ref[...] + l2b_ref[...]


def _post(x, mods, og, od, gg, gd, wts, tm, per_token):
    nb, t, _ = x.shape
    if per_token:
        mod_spec = pl.BlockSpec((6, tm, D_MODEL), lambda b, i: (0, i, 0))
    else:
        mod_spec = pl.BlockSpec((None, 6, D_MODEL), lambda b, i: (b, 0, 0))
    row = lambda n: pl.BlockSpec((None, tm, n), lambda b, i: (b, i, 0))
    return pl.pallas_call(
        functools.partial(_post_body, per_token=per_token),
        grid=(nb, t // tm),
        in_specs=[row(D_MODEL), mod_spec, row(GDN_V), row(DIFF_V), row(D_MODEL), row(D_MODEL)]
                 + [_const_spec(w.shape) for w in wts],
        out_specs=row(D_MODEL),
        out_shape=jax.ShapeDtypeStruct((nb, t, D_MODEL), F32),
        compiler_params=_cparams(("arbitrary", "arbitrary")),
        name="post",
    )(x, mods, og, od, gg, gd, *wts)


def kernel(x_prompt, x_sample, c_prompt, c_sample, cache_k, cache_v, page_table, state_conv, state_gdn, rel_bias,
           w_in, w_conv, a_log, dt_bias, gdn_norm_g, lam_q1, lam_k1, lam_q2, lam_k2, diff_norm_g, w_br_gdn,
           w_br_diff, w_o, w_ada, b_ada, ln1_g, ln1_b, ln2_g, ln2_b, w_up, b_up, w_down, b_down):
    assert DEPTH == 1 and w_in.shape[0] == 1
    l = 0
    lam_init = 0.8 - 0.6 * math.exp(-0.3 * l)
    nbp, seq, _ = x_prompt.shape
    nbs, dec_seq, _ = x_sample.shape
    n_tok_s = nbs * dec_seq
    past_len = page_table.shape[1] * PAGE_SIZE
    assert seq % Q_TILE == 0 and CONV_W - 1 <= dec_seq and 2 * dec_seq == SUBLANES
    assert page_table.shape[1] % PAGES_PER_STEP == 0 and nbs % (8 * SEQ_PER_ITER) == 0

    offs = np.concatenate([[0], np.cumsum(IN_SPLITS)])
    col = lambda i: w_in[l][:, int(offs[i]):int(offs[i + 1])]
    w_main = jnp.concatenate([col(0), col(1), col(4), col(5), col(6), col(7), col(8)], axis=1).astype(BF16)
    w_small = jnp.concatenate([col(2), col(3), jnp.zeros((D_MODEL, LANES - 2 * H_GDN), F32)], axis=1).astype(BF16)
    pad_gate = lambda a: jnp.zeros((1, LANES), F32).at[0, H_GDN:2 * H_GDN].set(a)
    alog_pad, dtb_pad = pad_gate(a_log[l]), pad_gate(dt_bias[l])
    gng = gdn_norm_g[l].reshape(1, DV_GDN)
    diff_g = diff_norm_g[l].reshape(1, HEAD_W)
    lam_w = (lam_q1[l].reshape(1, -1), lam_k1[l].reshape(1, -1), lam_q2[l].reshape(1, -1), lam_k2[l].reshape(1, -1))
    r2 = lambda a: a.reshape(1, -1)
    post_w = (w_br_gdn[l].astype(BF16), w_br_diff[l].astype(BF16), w_o[l].astype(BF16), r2(ln1_g[l]), r2(ln1_b[l]),
              w_up[l].astype(BF16), r2(b_up[l]), w_down[l].astype(BF16), r2(b_down[l]), r2(ln2_g[l]), r2(ln2_b[l]))

    ada = _ada(jnp.concatenate([c_prompt, c_sample], axis=0), w_ada[l].astype(BF16), r2(b_ada[l]))
    mods_p = ada[:nbp].reshape(nbp, 6, D_MODEL)
    mods_s = jnp.repeat(ada[nbp:].reshape(nbs, 6, D_MODEL), dec_seq, axis=0).transpose(1, 0, 2)
    bias_p, bias_last, bias_new, far = _bias_prep(rel_bias, past_len, dec_seq)

    conv_p, z_p, q_p, k_p, v_p, gg_p, gd_p, ba_p = _inproj(x_prompt, mods_p, w_main, w_small, 512, False)
    og_p, s_p = _gdn_prompt(conv_p, z_p, ba_p, w_conv[l], alog_pad, dtb_pad, gng, 4 * CHUNK)
    od_p = _attn_prompt(lam_w, diff_g, bias_p, q_p, k_p, v_p, lam_init)
    y_p = _post(x_prompt, mods_p, og_p, od_p, gg_p, gd_p, post_w, 512, False)

    xs = x_sample.reshape(1, n_tok_s, D_MODEL)
    conv_s, z_s, q_s, k_s, v_s, gg_s, gd_s, ba_s = _inproj(xs, mods_s, w_main, w_small, n_tok_s, True)
    per_seq = lambda a: a.reshape(nbs, dec_seq, a.shape[-1])
    pad8 = lambda a: jnp.pad(a.astype(F32), ((0, 0), (0, SUBLANES - dec_seq), (0, 0)))
    conv_s3 = per_seq(conv_s)
    upx = jnp.concatenate([jnp.zeros((nbs, SUBLANES - (CONV_W - 1), GDN_CONV_CH), F32), state_conv[l], conv_s3,
                           jnp.zeros((nbs, SUBLANES - dec_seq, GDN_CONV_CH), F32)], axis=1)
    og_s8, s_s = _gdn_sample(upx, pad8(per_seq(z_s)), pad8(per_seq(ba_s)), state_gdn[l], w_conv[l], alog_pad,
                             dtb_pad, gng, dec_seq, 8)
    ck = jnp.transpose(cache_k[l], (0, 2, 3, 4, 1)).reshape(cache_k.shape[1], DIFF_QK, PAGE_SIZE)
    cv = cache_v[l].reshape(cache_v.shape[1], PAGE_SIZE * H_DIFF, HEAD_W)
    q_s3 = per_seq(q_s).astype(F32)
    od_s8 = _attn_sample(page_table, lam_w, diff_g, bias_last, bias_new, far, jnp.concatenate([q_s3, q_s3], axis=1),
                         pad8(per_seq(k_s)), pad8(per_seq(v_s)), ck, cv, lam_init, dec_seq)
    flat = lambda a8: a8[:, :dec_seq].reshape(1, n_tok_s, a8.shape[-1]).astype(BF16)
    y_s = _post(xs, mods_s, flat(og_s8), flat(od_s8), gg_s, gd_s, post_w, 256, True)

    hd = lambda a, b: a.reshape(1, b, -1, H_DIFF, 2, DH_DIFF)
    hv = lambda a, b: a.reshape(1, b, -1, H_DIFF, 2 * DH_DIFF)
    return (y_p, y_s.reshape(nbs, dec_seq, D_MODEL),
            hd(k_p, nbp), hv(v_p, nbp), conv_p[:, seq - (CONV_W - 1):][None], s_p[None],
            hd(k_s, nbs), hv(v_s, nbs), conv_s3[:, dec_seq - (CONV_W - 1):][None], s_s[None])
```

```python
import functools
import math

import numpy as np
import jax
import jax.numpy as jnp
from jax import lax
from jax.experimental import pallas as pl
from jax.experimental.pallas import tpu as pltpu

F32 = jnp.float32
BF16 = jnp.bfloat16

D_MODEL = 1024
DEPTH = 1
PAGE_SIZE = 128
H_GDN = 4
DK_GDN = 128
DV_GDN = 128
CONV_W = 4
CHUNK = 64
GDN_NORM_EPS = 1e-6
H_DIFF = 4
DH_DIFF = 64
DIFF_NORM_EPS = 1e-5
N_BUCKETS = 32
MAX_DISTANCE = 128
D_FF = 4 * D_MODEL
LN_EPS = 1e-5
GDN_QK = H_GDN * DK_GDN
GDN_V = H_GDN * DV_GDN
GDN_CONV_CH = 2 * GDN_QK + GDN_V
DIFF_QK = H_DIFF * 2 * DH_DIFF
DIFF_V = H_DIFF * 2 * DH_DIFF
HEAD_W = 2 * DH_DIFF
IN_SPLITS = (GDN_CONV_CH, GDN_V, H_GDN, H_GDN, DIFF_QK, DIFF_QK, DIFF_V, D_MODEL, D_MODEL)
ALPHA = (2.0 * DEPTH) ** 0.25

LANES = 128
SUBLANES = 8
VMEM_LIMIT = 56 * 1024 * 1024

Q_TILE = 256
PAGES_PER_STEP = 8
SEQ_PER_ITER = 2
MASKED = N_BUCKETS

NN = (((1,), (0,)), ((), ()))
NT = (((1,), (1,)), ((), ()))
TN = (((0,), (0,)), ((), ()))


def _cparams(sem):
    return pltpu.CompilerParams(dimension_semantics=sem, vmem_limit_bytes=VMEM_LIMIT)


def _const_spec(shape):
    nd = len(shape)
    return pl.BlockSpec(shape, lambda *_: (0,) * nd, pipeline_mode=pl.Buffered(1))


def _sigmoid(x):
    return 1.0 / (1.0 + jnp.exp(-x))


def _silu(x):
    return x * _sigmoid(x)


def _softplus(x):
    return jnp.maximum(x, 0.0) + jnp.log(1.0 + jnp.exp(-jnp.abs(x)))


def _ln(x):
    mu = jnp.mean(x, axis=-1, keepdims=True)
    xc = x - mu
    var = jnp.mean(xc * xc, axis=-1, keepdims=True)
    return xc * lax.rsqrt(var + LN_EPS)


def _dot(a, b, dims=NN):
    return lax.dot_general(a, b, dims, preferred_element_type=F32)


def _dot_bf(a, b, dims=NN):
    return _dot(a.astype(BF16), b.astype(BF16), dims)


def _split(a):
    hi = a.astype(BF16)
    lo = (a - hi.astype(F32)).astype(BF16)
    return hi, lo


def _dot3(a, b, dims=NN, fuse=True):
    ah, al = _split(a)
    bh, bl = _split(b)
    if not fuse:
        return _dot(ah, bh, dims) + (_dot(ah, bl, dims) + _dot(al, bh, dims))
    lhs = jnp.concatenate([ah, ah, al], axis=1)
    rhs = jnp.concatenate([bh, bl, bh], axis=1 if dims == NT else 0)
    return _dot(lhs, rhs, dims)


def _dot_f32(a, b, dims=NN):
    return lax.dot_general(a, b, dims, precision=lax.Precision.HIGHEST, preferred_element_type=F32)


def _mod(mod_ref, i, per_token):
    return mod_ref[i] if per_token else mod_ref[i:i + 1, :]


def _ada_body(c_ref, w_ref, b_ref, o_ref):
    s = _silu(c_ref[...])
    o_ref[...] = _dot(s.astype(BF16), w_ref[...]) + b_ref[...]


def _ada(c_all, w_ada, b_ada):
    n = c_all.shape[0]
    return pl.pallas_call(
        _ada_body,
        grid=(6,),
        in_specs=[pl.BlockSpec((n, D_MODEL), lambda j: (0, 0)),
                  pl.BlockSpec((D_MODEL, D_MODEL), lambda j: (0, j)),
                  pl.BlockSpec((1, D_MODEL), lambda j: (0, j))],
        out_specs=pl.BlockSpec((n, D_MODEL), lambda j: (0, j)),
        out_shape=jax.ShapeDtypeStruct((n, 6 * D_MODEL), F32),
        compiler_params=_cparams(("arbitrary",)),
        name="ada",
    )(c_all, w_ada, b_ada)


def _t5_bucket_np(rel):
    n = np.maximum(rel, 0)
    max_exact = N_BUCKETS // 2
    nf = np.maximum(n, 1).astype(np.float32)
    large = max_exact + (np.log(nf / np.float32(max_exact)) / np.float32(math.log(MAX_DISTANCE / max_exact))
                         * np.float32(N_BUCKETS - max_exact)).astype(np.int32)
    large = np.minimum(large, N_BUCKETS - 1)
    return np.where(n < max_exact, n, large).astype(np.int32)


def _bias_codes(past_len, dec_seq):
    r = np.arange(Q_TILE)[:, None]
    c = np.arange(Q_TILE)[None, :]
    prev_tile = _t5_bucket_np(Q_TILE + r - c)
    diag_tile = np.where(r >= c, _t5_bucket_np(r - c), MASKED)
    prompt = np.concatenate([prev_tile, diag_tile], axis=1).astype(np.int32)
    prompt = np.concatenate([prompt, prompt], axis=0)
    nrow = H_DIFF * SUBLANES
    qrow = (np.arange(nrow) % dec_seq)[:, None]
    nkeys = PAGES_PER_STEP * PAGE_SIZE
    kpos = past_len - nkeys + np.arange(nkeys)[None, :]
    last = _t5_bucket_np(past_len + qrow - kpos).astype(np.int32)
    j = np.arange(SUBLANES)[None, :]
    new = np.where((j <= qrow) & (j < dec_seq), _t5_bucket_np(qrow - j), MASKED).astype(np.int32)
    return prompt, last, new


def _bias_body(tbl_ref, cp_ref, cl_ref, cn_ref, bp_ref, bl_ref, bn_ref, far_ref):
    def lookup(codes, h):
        acc = jnp.full(codes.shape, -jnp.inf, F32)
        for b in range(N_BUCKETS):
            acc = jnp.where(codes == b, tbl_ref[b, h], acc)
        return acc

    cp = cp_ref[...]
    for h in range(H_DIFF):
        bp_ref[h] = lookup(cp, h) - tbl_ref[N_BUCKETS - 1, h]
    cl = cl_ref[...]
    cn = cn_ref[...]
    nrow = cl.shape[0]
    row_head = lax.broadcasted_iota(jnp.int32, (nrow, 1), 0) // SUBLANES
    bl = jnp.zeros(cl.shape, F32)
    bn = jnp.zeros(cn.shape, F32)
    far = jnp.zeros((nrow, LANES), F32)
    for h in range(H_DIFF):
        bl = jnp.where(row_head == h, lookup(cl, h), bl)
        bn = jnp.where(row_head == h, lookup(cn, h), bn)
        far = jnp.where(row_head == h, tbl_ref[N_BUCKETS - 1, h], far)
    bl_ref[...] = bl
    bn_ref[...] = bn
    far_ref[...] = far


def _bias_prep(rel_bias, past_len, dec_seq):
    cp, cl, cn = _bias_codes(past_len, dec_seq)
    nrow = cl.shape[0]
    vm = pl.BlockSpec(memory_space=pltpu.VMEM)
    return pl.pallas_call(
        _bias_body,
        in_specs=[pl.BlockSpec(memory_space=pltpu.SMEM), vm, vm, vm],
        out_specs=[vm, vm, vm, vm],
        out_shape=[jax.ShapeDtypeStruct((H_DIFF,) + cp.shape, F32),
                   jax.ShapeDtypeStruct(cl.shape, F32),
                   jax.ShapeDtypeStruct(cn.shape, F32),
                   jax.ShapeDtypeStruct((nrow, LANES), F32)],
        name="bias_prep",
    )(rel_bias, jnp.asarray(cp), jnp.asarray(cl), jnp.asarray(cn))


def _lam(lq1_ref, lk1_ref, lq2_ref, lk2_ref, lam_init):
    s1 = jnp.sum(lq1_ref[...] * lk1_ref[...], axis=-1, keepdims=True)
    s2 = jnp.sum(lq2_ref[...] * lk2_ref[...], axis=-1, keepdims=True)
    return jnp.exp(s1) - jnp.exp(s2) + lam_init


_SEG = {"conv": (0, 1536), "z": (1536, 2048), "q": (2048, 2560), "k": (2560, 3072), "v": (3072, 3584),
        "gg": (3584, 4608), "gd": (4608, 5632)}
N_MAIN = 5632


def _inproj_body(x_ref, mod_ref, wm_ref, ws_ref, conv_ref, z_ref, q_ref, k_ref, v_ref, gg_ref, gd_ref, ba_ref,
                 *, per_token):
    h = _ln(x_ref[...]) * (1.0 + _mod(mod_ref, 1, per_token)) + _mod(mod_ref, 0, per_token)
    hb = h.astype(BF16)

    def seg(name):
        a, b = _SEG[name]
        return _dot(hb, wm_ref[:, a:b])

    conv_ref[...] = seg("conv")
    z_ref[...] = seg("z").astype(BF16)
    q_ref[...] = seg("q").astype(BF16)
    k_ref[...] = seg("k")
    sv = seg("v")
    for hh in range(H_DIFF):
        v_ref[pl.ds(hh, sv.shape[0], stride=H_DIFF), :] = sv[:, hh * HEAD_W:(hh + 1) * HEAD_W]
    gg_ref[...] = seg("gg").astype(BF16)
    gd_ref[...] = seg("gd").astype(BF16)
    ba_ref[...] = _dot(hb, ws_ref[...])


def _inproj(x, mods, w_main, w_small, tm, per_token):
    nb, t, _ = x.shape
    if per_token:
        mod_spec = pl.BlockSpec((6, tm, D_MODEL), lambda b, i: (0, i, 0))
    else:
        mod_spec = pl.BlockSpec((None, 6, D_MODEL), lambda b, i: (b, 0, 0))

    def out(n, dt):
        return pl.BlockSpec((None, tm, n), lambda b, i: (b, i, 0)), jax.ShapeDtypeStruct((nb, t, n), dt)

    v_out = (pl.BlockSpec((None, tm * H_DIFF, HEAD_W), lambda b, i: (b, i, 0)),
             jax.ShapeDtypeStruct((nb, t * H_DIFF, HEAD_W), F32))
    outs = [out(GDN_CONV_CH, F32), out(GDN_V, BF16), out(DIFF_QK, BF16), out(DIFF_QK, F32), v_out,
            out(D_MODEL, BF16), out(D_MODEL, BF16), out(LANES, F32)]
    return pl.pallas_call(
        functools.partial(_inproj_body, per_token=per_token),
        grid=(nb, t // tm),
        in_specs=[pl.BlockSpec((None, tm, D_MODEL), lambda b, i: (b, i, 0)), mod_spec,
                  _const_spec((D_MODEL, N_MAIN)), _const_spec((D_MODEL, LANES))],
        out_specs=[o[0] for o in outs],
        out_shape=[o[1] for o in outs],
        compiler_params=_cparams(("arbitrary", "arbitrary")),
        name="inproj",
    )(x, mods, w_main, w_small)


def _gdn_solve(chains, c):
    ii = lax.broadcasted_iota(jnp.int32, (c, c), 0)
    jj = lax.broadcasted_iota(jnp.int32, (c, c), 1)
    incl = ii >= jj
    eye = (ii == jj).astype(F32)
    fuse = c % CHUNK == 0
    gamma = [jnp.where(incl, jnp.exp(jnp.where(incl, ch[4] - ch[5], 0.0)), 0.0) for ch in chains]
    kk = [_dot3(ch[1], ch[1], NT) for ch in chains]
    a = [jnp.where(ii > jj, ch[3] * k * g, 0.0) for ch, k, g in zip(chains, kk, gamma)]
    a8 = [jnp.where((ii >> 3) == (jj >> 3), m, 0.0) for m in a]
    p = [_dot3(m, m, fuse=fuse) for m in a8]
    x = [eye - m for m in a8]
    x = [xi + _dot3(xi, pi, fuse=fuse) for xi, pi in zip(x, p)]
    p = [_dot3(pi, pi, fuse=fuse) for pi in p]
    x = [xi + _dot3(xi, pi, fuse=fuse) for xi, pi in zip(x, p)]
    sh = 3
    while (1 << sh) < c:
        lvl = ((ii >> (sh + 1)) == (jj >> (sh + 1))) & ((ii >> sh) != (jj >> sh))
        t = [_dot3(jnp.where(lvl, m, 0.0), xi, fuse=fuse) for m, xi in zip(a, x)]
        x = [xi - _dot3(xi, ti, fuse=fuse) for xi, ti in zip(x, t)]
        sh += 1
    egc = [jnp.exp(ch[4]) for ch in chains]
    uw = [_dot3(xi, jnp.concatenate([ch[2] * ch[3], ch[1] * (ch[3] * e)], axis=1), fuse=fuse)
          for xi, ch, e in zip(x, chains, egc)]
    qk = [jnp.where(incl, _dot_bf(ch[0], ch[1], NT) * g, 0.0) for ch, g in zip(chains, gamma)]
    out = []
    for ch, e, uwi, qki in zip(chains, egc, uw, qk):
        g_last = ch[4][c - 1:c, :]
        wq = jnp.concatenate([uwi[:, DV_GDN:], ch[0] * e], axis=0).astype(BF16)
        k_dec = (ch[1] * jnp.exp(g_last - ch[4])).astype(BF16)
        out.append((uwi[:, :DV_GDN], wq, qki.astype(BF16), k_dec, jnp.exp(g_last)))
    return out


def _gdn_state_step(sols, states, c):
    sb = [s.astype(BF16) for s in states]
    wq = [_dot(sol[1], b) for sol, b in zip(sols, sb)]
    vb = [(sol[0] - m[:c]).astype(BF16) for sol, m in zip(sols, wq)]
    o = [m[c:] + _dot(sol[2], v) for sol, m, v in zip(sols, wq, vb)]
    s_new = [sol[4] * s + _dot(sol[3], v, TN) for sol, s, v in zip(sols, states, vb)]
    return o, s_new


def _l2n(x):
    return x * lax.rsqrt(jnp.sum(x * x, axis=-1, keepdims=True) + 1e-6)


def _gdn_gates(ba, alog_ref, dtb_ref):
    beta = _sigmoid(ba)
    g = -jnp.exp(alog_ref[...]) * _softplus(ba + dtb_ref[...])
    return beta, g


def _gdn_out(o, z, gng):
    o = o * lax.rsqrt(jnp.mean(o * o, axis=-1, keepdims=True) + GDN_NORM_EPS) * gng
    return o * _silu(z)


def _cumsum_mats(c):
    ii = lax.broadcasted_iota(jnp.int32, (c, c), 0)
    jj = lax.broadcasted_iota(jnp.int32, (c, c), 1)
    return (ii >= jj).astype(F32)


def _gdn_prompt_body(conv_ref, z_ref, ba_ref, wc_ref, alog_ref, dtb_ref, gng_ref, og_ref, s_ref, cbuf_ref, *, tt):
    t = pl.program_id(1)

    @pl.when(t == 0)
    def _():
        s_ref[...] = jnp.zeros(s_ref.shape, F32)
        cbuf_ref[0:SUBLANES, :] = jnp.zeros((SUBLANES, GDN_CONV_CH), F32)

    u_in = conv_ref[...]
    cbuf_ref[SUBLANES:, :] = u_in
    y = cbuf_ref[SUBLANES:, :] * wc_ref[CONV_W - 1:CONV_W, :]
    for j in range(CONV_W - 1):
        off = SUBLANES - (CONV_W - 1) + j
        y = y + cbuf_ref[off:off + tt, :] * wc_ref[j:j + 1, :]
    cbuf_ref[0:SUBLANES, :] = u_in[tt - SUBLANES:, :]
    qkv = _silu(y)

    beta_all, g_all = _gdn_gates(ba_ref[...], alog_ref, dtb_ref)
    ltri = _cumsum_mats(CHUNK)
    nch = tt // CHUNK
    gc_all = jnp.concatenate([_dot_f32(ltri, g_all[c * CHUNK:(c + 1) * CHUNK, :]) for c in range(nch)], axis=0)
    gc_t = gc_all.T
    gng = gng_ref[...]
    chains = []
    for c in range(nch):
        rows = slice(c * CHUNK, (c + 1) * CHUNK)
        for h in range(H_GDN):
            lo = h * DK_GDN
            chains.append((_l2n(qkv[rows, lo:lo + DK_GDN]) * (DK_GDN ** -0.5),
                           _l2n(qkv[rows, GDN_QK + lo:GDN_QK + lo + DK_GDN]),
                           qkv[rows, 2 * GDN_QK + lo:2 * GDN_QK + lo + DV_GDN],
                           beta_all[rows, h:h + 1], gc_all[rows, H_GDN + h:H_GDN + h + 1],
                           gc_t[H_GDN + h:H_GDN + h + 1, rows]))
    sols = _gdn_solve(chains, CHUNK)
    states = [s_ref[h] for h in range(H_GDN)]
    for c in range(nch):
        rows = slice(c * CHUNK, (c + 1) * CHUNK)
        o, states = _gdn_state_step(sols[c * H_GDN:(c + 1) * H_GDN], states, CHUNK)
        for h in range(H_GDN):
            cols = slice(h * DV_GDN, (h + 1) * DV_GDN)
            og_ref[rows, cols] = _gdn_out(o[h], z_ref[rows, cols].astype(F32), gng).astype(og_ref.dtype)
    for h in range(H_GDN):
        s_ref[h] = states[h]


def _gdn_prompt(conv_in, z, ba, w_conv, alog_pad, dtb_pad, gng, tt):
    nb, t, _ = conv_in.shape
    row = lambda n: pl.BlockSpec((None, tt, n), lambda b, i: (b, i, 0))
    return pl.pallas_call(
        functools.partial(_gdn_prompt_body, tt=tt),
        grid=(nb, t // tt),
        in_specs=[row(GDN_CONV_CH), row(GDN_V), row(LANES),
                  _const_spec((CONV_W, GDN_CONV_CH)), _const_spec((1, LANES)), _const_spec((1, LANES)),
                  _const_spec((1, DV_GDN))],
        out_specs=[row(GDN_V), pl.BlockSpec((None, H_GDN, DK_GDN, DV_GDN), lambda b, i: (b, 0, 0, 0))],
        out_shape=[jax.ShapeDtypeStruct((nb, t, GDN_V), BF16),
                   jax.ShapeDtypeStruct((nb, H_GDN, DK_GDN, DV_GDN), F32)],
        scratch_shapes=[pltpu.VMEM((tt + SUBLANES, GDN_CONV_CH), F32)],
        compiler_params=_cparams(("arbitrary", "arbitrary")),
        name="gdn_prompt",
    )(conv_in, z, ba, w_conv, alog_pad, dtb_pad, gng)


def _gdn_sample_body(upx_ref, z_ref, ba_ref, s0_ref, wc_ref, alog_ref, dtb_ref, gng_ref, og_ref, s_ref,
                     *, nb, n_valid):
    c = SUBLANES
    valid = lax.broadcasted_iota(jnp.int32, (c, 1), 0) < n_valid
    ltri = _cumsum_mats(c)
    gng = gng_ref[...]

    def body(i, carry):
        chains, states = [], []
        for n in (SEQ_PER_ITER * i + k for k in range(SEQ_PER_ITER)):
            y = upx_ref[n, pl.ds(SUBLANES, c), :] * wc_ref[CONV_W - 1:CONV_W, :]
            for j in range(CONV_W - 1):
                off = SUBLANES - (CONV_W - 1) + j
                y = y + upx_ref[n, pl.ds(off, c), :] * wc_ref[j:j + 1, :]
            qkv = _silu(y)
            beta_all, g_all = _gdn_gates(ba_ref[n], alog_ref, dtb_ref)
            beta_all = jnp.where(valid, beta_all, 0.0)
            g_all = jnp.where(valid, g_all, 0.0)
            gc_all = _dot_f32(ltri, g_all)
            gc_t = jnp.concatenate([gc_all, jnp.zeros((LANES - c, LANES), F32)], axis=0).T
            for h in range(H_GDN):
                lo = h * DK_GDN
                chains.append((_l2n(qkv[:, lo:lo + DK_GDN]) * (DK_GDN ** -0.5),
                               jnp.where(valid, _l2n(qkv[:, GDN_QK + lo:GDN_QK + lo + DK_GDN]), 0.0),
                               jnp.where(valid, qkv[:, 2 * GDN_QK + lo:2 * GDN_QK + lo + DV_GDN], 0.0),
                               beta_all[:, h:h + 1], gc_all[:, H_GDN + h:H_GDN + h + 1],
                               gc_t[H_GDN + h:H_GDN + h + 1, 0:c]))
                states.append(s0_ref[n, h])
        o, states = _gdn_state_step(_gdn_solve(chains, c), states, c)
        for k in range(SEQ_PER_ITER):
            n = SEQ_PER_ITER * i + k
            for h in range(H_GDN):
                cols = slice(h * DV_GDN, (h + 1) * DV_GDN)
                s_ref[n, h] = states[k * H_GDN + h]
                og_ref[n, :, cols] = _gdn_out(o[k * H_GDN + h], z_ref[n, :, cols], gng)
        return carry

    lax.fori_loop(0, nb // SEQ_PER_ITER, body, 0)


def _gdn_sample(upx, z8, ba8, s0, w_conv, alog_pad, dtb_pad, gng, n_valid, nb):
    n = upx.shape[0]
    blk = lambda *shape: pl.BlockSpec((nb,) + shape, lambda i: (i,) + (0,) * len(shape))
    return pl.pallas_call(
        functools.partial(_gdn_sample_body, nb=nb, n_valid=n_valid),
        grid=(n // nb,),
        in_specs=[blk(2 * SUBLANES, GDN_CONV_CH), blk(SUBLANES, GDN_V), blk(SUBLANES, LANES),
                  blk(H_GDN, DK_GDN, DV_GDN),
                  _const_spec((CONV_W, GDN_CONV_CH)), _const_spec((1, LANES)), _const_spec((1, LANES)),
                  _const_spec((1, DV_GDN))],
        out_specs=[blk(SUBLANES, GDN_V), blk(H_GDN, DK_GDN, DV_GDN)],
        out_shape=[jax.ShapeDtypeStruct((n, SUBLANES, GDN_V), F32),
                   jax.ShapeDtypeStruct((n, H_GDN, DK_GDN, DV_GDN), F32)],
        compiler_params=_cparams(("arbitrary",)),
        name="gdn_sample",
    )(upx, z8, ba8, s0, w_conv, alog_pad, dtb_pad, gng)


def _diff_norm(o, g, lam_init):
    return o * lax.rsqrt(jnp.mean(o * o, axis=-1, keepdims=True) + DIFF_NORM_EPS) * g * (1.0 - lam_init)


def _attn_prompt_body(lq1_ref, lk1_ref, lq2_ref, lk2_ref, g_ref, bias_ref, q_ref, k_ref, v_ref, o_ref,
                      *, lam_init, t):
    lam = _lam(lq1_ref, lk1_ref, lq2_ref, lk2_ref, lam_init)
    kb = k_ref[...].astype(BF16)
    vb = v_ref[pl.ds(pl.program_id(1), t, stride=H_DIFF), :].astype(BF16)
    g = g_ref[...]
    lane = lax.broadcasted_iota(jnp.int32, (1, HEAD_W), 1)
    zero = jnp.zeros((), BF16)
    for i in range(t // Q_TILE):
        r0 = i * Q_TILE
        qi = q_ref[r0:r0 + Q_TILE, :] * jnp.asarray(DH_DIFF ** -0.5, BF16)
        q2 = jnp.concatenate([jnp.where(lane < DH_DIFF, qi, zero), jnp.where(lane >= DH_DIFF, qi, zero)], axis=0)
        near0 = max(i - 1, 0) * Q_TILE
        sn = _dot(q2, kb[near0:r0 + Q_TILE], NT) + bias_ref[:, 2 * Q_TILE - (r0 + Q_TILE - near0):]
        m = jnp.max(sn, axis=-1, keepdims=True)
        if near0 > 0:
            sf = _dot(q2, kb[:near0], NT)
            m = jnp.maximum(m, jnp.max(sf, axis=-1, keepdims=True))
        en = jnp.exp(sn - m)
        l = jnp.sum(en, axis=-1, keepdims=True)
        o2 = _dot(en.astype(BF16), vb[near0:r0 + Q_TILE])
        if near0 > 0:
            ef = jnp.exp(sf - m)
            l = l + jnp.sum(ef, axis=-1, keepdims=True)
            o2 = o2 + _dot(ef.astype(BF16), vb[:near0])
        o2 = o2 * (1.0 / l)
        o = o2[:Q_TILE] - lam * o2[Q_TILE:]
        o_ref[r0:r0 + Q_TILE, :] = _diff_norm(o, g, lam_init).astype(o_ref.dtype)


def _attn_prompt(lam_w, diff_g, bias_p, q, k, v, lam_init):
    nb, t, _ = q.shape
    head = lambda: pl.BlockSpec((None, t, HEAD_W), lambda b, h: (b, 0, h))
    small = _const_spec((1, DH_DIFF))
    return pl.pallas_call(
        functools.partial(_attn_prompt_body, lam_init=lam_init, t=t),
        grid=(nb, H_DIFF),
        in_specs=[small, small, small, small, _const_spec((1, HEAD_W)),
                  pl.BlockSpec((None, 2 * Q_TILE, 2 * Q_TILE), lambda b, h: (h, 0, 0)),
                  head(), head(), pl.BlockSpec((None, t * H_DIFF, HEAD_W), lambda b, h: (b, 0, 0))],
        out_specs=head(),
        out_shape=jax.ShapeDtypeStruct((nb, t, DIFF_V), BF16),
        compiler_params=_cparams(("arbitrary", "arbitrary")),
        name="attn_prompt",
    )(*lam_w, diff_g, bias_p, q, k, v)


NROW_S = H_DIFF * SUBLANES
CHUNK_KEYS = PAGES_PER_STEP * PAGE_SIZE
_HEAD_ROWS = [slice(h * SUBLANES, (h + 1) * SUBLANES) for h in range(H_DIFF)]


def _page_copies(pt_ref, ck_hbm, cv_hbm, kbuf, vbuf, sem, chunk, slot):
    vrows = PAGE_SIZE * H_DIFF
    out = []
    for p in range(PAGES_PER_STEP):
        page = pt_ref[chunk * PAGES_PER_STEP + p]
        out.append(pltpu.make_async_copy(ck_hbm.at[page], kbuf.at[slot, :, pl.ds(p * PAGE_SIZE, PAGE_SIZE)],
                                         sem.at[0, slot]))
        out.append(pltpu.make_async_copy(cv_hbm.at[page], vbuf.at[slot, pl.ds(p * vrows, vrows)], sem.at[1, slot]))
    return out


def _sample_queries(q_ref, dec_seq):
    lane_hc = lax.broadcasted_iota(jnp.int32, (SUBLANES, DIFF_QK), 1) // DH_DIFF
    row_c = lax.broadcasted_iota(jnp.int32, (SUBLANES, DIFF_QK), 0) // dec_seq
    q2 = q_ref[...] * (DH_DIFF ** -0.5)
    return jnp.concatenate([jnp.where(lane_hc == 2 * h + row_c, q2, 0.0) for h in range(H_DIFF)],
                           axis=0).astype(BF16)


def _sample_attn_init(qbd, kn_ref, vn_ref, bn_ref):
    s = _dot(qbd, kn_ref[...].astype(BF16), NT) + bn_ref[...]
    m = jnp.max(s, axis=-1, keepdims=True)
    p = jnp.exp(s - m)
    pb = p.astype(BF16)
    vn = vn_ref[...].astype(BF16)
    acc = jnp.concatenate([_dot(pb[_HEAD_ROWS[h]], vn[:, h * HEAD_W:(h + 1) * HEAD_W]) for h in range(H_DIFF)],
                          axis=0)
    return m, jnp.sum(p, axis=-1, keepdims=True), acc


def _sample_attn_update(state, s, vbuf, slot):
    m_old, l_old, acc = state
    m_new = jnp.maximum(m_old, jnp.max(s, axis=-1, keepdims=True))
    alpha = jnp.exp(m_old - m_new)
    p = jnp.exp(s - m_new)
    pb = p.astype(BF16)
    pv = jnp.concatenate(
        [_dot(pb[_HEAD_ROWS[h]], vbuf[slot, pl.ds(h, CHUNK_KEYS, stride=H_DIFF), :].astype(BF16))
         for h in range(H_DIFF)], axis=0)
    return m_new, alpha * l_old + jnp.sum(p, axis=-1, keepdims=True), alpha * acc + pv


def _sample_attn_finish(state, lam, g, lam_init, dec_seq):
    _, l, acc = state
    row = lax.broadcasted_iota(jnp.int32, (NROW_S, 1), 0)
    coef = jnp.where((row % SUBLANES) < dec_seq, 1.0, -lam) / l
    ri = lax.broadcasted_iota(jnp.int32, (NROW_S, NROW_S), 0)
    rj = lax.broadcasted_iota(jnp.int32, (NROW_S, NROW_S), 1)
    sel = ((ri // SUBLANES == rj // SUBLANES) & (ri % SUBLANES == rj % dec_seq)).astype(F32)
    o = _dot_f32(sel, acc * coef)
    return jnp.concatenate([_diff_norm(o[_HEAD_ROWS[h]], g, lam_init) for h in range(H_DIFF)], axis=1)


def _post_body(x_ref, mod_ref, og_ref, od_ref, gg_ref, gd_ref, wbg_ref, wbd_ref, wo_ref, l1g_ref, l1b_ref,
               wup_ref, bup_ref, wdn_ref, bdn_ref, l2g_ref, l2b_ref, y_ref, *, per_token):
    md = lambda i: _mod(mod_ref, i, per_token)
    m = (_sigmoid(gg_ref[...].astype(F32)) * _dot(og_ref[...], wbg_ref[...])
         + _sigmoid(gd_ref[...].astype(F32)) * _dot(od_ref[...], wbd_ref[...]))
    mix = _dot(m.astype(BF16), wo_ref[...])
    x1 = _ln(ALPHA * x_ref[...] + md(2) * mix) * l1g_ref[...] + l1b_ref[...]
    h2 = _ln(x1) * (1.0 + md(4)) + md(3)
    up = _dot(h2.astype(BF16), wup_ref[...]) + bup_ref[...]
    act = jnp.square(jnp.maximum(up, 0.0))
    f = _dot(act.astype(BF16), wdn_ref[...]) + bdn_ref[...]
    y_ref[...] = _ln(ALPHA * x1 + md(5) * f) * l2g_ref[...] + l2b_ref[...]


def _post(x, mods, og, od, gg, gd, wts, tm, per_token):
    nb, t, _ = x.shape
    if per_token:
        mod_spec = pl.BlockSpec((6, tm, D_MODEL), lambda b, i: (0, i, 0))
    else:
        mod_spec = pl.BlockSpec((None, 6, D_MODEL), lambda b, i: (b, 0, 0))
    row = lambda n: pl.BlockSpec((None, tm, n), lambda b, i: (b, i, 0))
    return pl.pallas_call(
        functools.partial(_post_body, per_token=per_token),
        grid=(nb, t // tm),
        in_specs=[row(D_MODEL), mod_spec, row(GDN_V), row(DIFF_V), row(D_MODEL), row(D_MODEL)]
                 + [_const_spec(w.shape) for w in wts],
        out_specs=row(D_MODEL),
        out_shape=jax.ShapeDtypeStruct((nb, t, D_MODEL), F32),
        compiler_params=_cparams(("arbitrary", "arbitrary")),
        name="post",
    )(x, mods, og, od, gg, gd, *wts)


def _post_attn_body(pt_ref, x_ref, mod_ref, og_ref, od_ref, gg_ref, gd_ref, wbg_ref, wbd_ref, wo_ref, l1g_ref, l1b_ref,
                    wup_ref, bup_ref, wdn_ref, bdn_ref, l2g_ref, l2b_ref,
                    lq1_ref, lk1_ref, lq2_ref, lk2_ref, g_ref, bl_ref, bn_ref, far_ref, q_ref, kn_ref, vn_ref,
                    ck_hbm, cv_hbm, y_ref, o_ref, kbuf, vbuf, sem, *, lam_init, n_chunks, dec_seq):
    seq = pl.program_id(0) * pl.num_programs(1) + pl.program_id(1)
    n_seq = pl.num_programs(0) * pl.num_programs(1)
    copies = functools.partial(_page_copies, pt_ref, ck_hbm, cv_hbm, kbuf, vbuf, sem)

    @pl.when(seq == 0)
    def _():
        for cp in copies(0, 0):
            cp.start()

    md = lambda i: mod_ref[i:i + 1, :]
    m = (_sigmoid(gg_ref[...].astype(F32)) * _dot(og_ref[...], wbg_ref[...])
         + _sigmoid(gd_ref[...].astype(F32)) * _dot(od_ref[...], wbd_ref[...]))
    mix = _dot(m.astype(BF16), wo_ref[...])
    x1 = _ln(ALPHA * x_ref[...] + md(2) * mix) * l1g_ref[...] + l1b_ref[...]
    h2b = (_ln(x1) * (1.0 + md(4)) + md(3)).astype(BF16)

    qbd = _sample_queries(q_ref, dec_seq)
    state = _sample_attn_init(qbd, kn_ref, vn_ref, bn_ref)
    ff = D_FF // n_chunks
    f = None
    for j in range(n_chunks):
        slot = j % 2
        if j + 1 < n_chunks:
            for cp in copies(seq * n_chunks + j + 1, 1 - slot):
                cp.start()
        else:
            @pl.when(seq + 1 < n_seq)
            def _():
                for cp in copies((seq + 1) * n_chunks, 1 - slot):
                    cp.start()
        for cp in copies(seq * n_chunks + j, slot):
            cp.wait()
        s = _dot(qbd, kbuf[slot].astype(BF16)) + (bl_ref[...] if j == n_chunks - 1 else far_ref[:, 0:1])
        cols = slice(j * ff, (j + 1) * ff)
        up = _dot(h2b, wup_ref[:, cols]) + bup_ref[:, cols]
        part = _dot(jnp.square(jnp.maximum(up, 0.0)).astype(BF16), wdn_ref[cols, :])
        f = part if f is None else f + part
        state = _sample_attn_update(state, s, vbuf, slot)

    y_ref[...] = _ln(ALPHA * x1 + md(5) * (f + bdn_ref[...])) * l2g_ref[...] + l2b_ref[...]
    lam = _lam(lq1_ref, lk1_ref, lq2_ref, lk2_ref, lam_init)
    o_ref[...] = _sample_attn_finish(state, lam, g_ref[...], lam_init, dec_seq)


def _post_attn(x, mods, og, od, gg, gd, wts, tm, page_table, lam_w, diff_g, bias_last, bias_new, far, q8, kn8, vn8,
               cache_k, cache_v, lam_init, dec_seq):
    nb, t, _ = x.shape
    n_seq, n_pages = page_table.shape
    n_chunks = n_pages // PAGES_PER_STEP
    steps = t // tm
    assert nb * steps == n_seq and n_chunks % 2 == 0 and D_FF % n_chunks == 0
    row = lambda n: pl.BlockSpec((None, tm, n), lambda b, i, pt: (b, i, 0))
    const = lambda shape: pl.BlockSpec(shape, lambda b, i, pt: (0,) * len(shape), pipeline_mode=pl.Buffered(1))
    per_seq = pl.BlockSpec((None, SUBLANES, DIFF_QK), lambda b, i, pt: (b * steps + i, 0, 0))
    grid_spec = pltpu.PrefetchScalarGridSpec(
        num_scalar_prefetch=1,
        grid=(nb, steps),
        in_specs=[row(D_MODEL), pl.BlockSpec((None, 6, D_MODEL), lambda b, i, pt: (b, 0, 0)),
                  row(GDN_V), row(DIFF_V), row(D_MODEL), row(D_MODEL)]
                 + [const(w.shape) for w in wts]
                 + [const((1, DH_DIFF))] * 4
                 + [const((1, HEAD_W)), const((NROW_S, CHUNK_KEYS)), const((NROW_S, SUBLANES)), const((NROW_S, LANES)),
                    per_seq, per_seq, per_seq,
                    pl.BlockSpec(memory_space=pl.ANY), pl.BlockSpec(memory_space=pl.ANY)],
        out_specs=[row(D_MODEL), per_seq],
        scratch_shapes=[pltpu.VMEM((2, DIFF_QK, CHUNK_KEYS), F32), pltpu.VMEM((2, CHUNK_KEYS * H_DIFF, HEAD_W), F32),
                        pltpu.SemaphoreType.DMA((2, 2))],
    )
    return pl.pallas_call(
        functools.partial(_post_attn_body, lam_init=lam_init, n_chunks=n_chunks, dec_seq=dec_seq),
        grid_spec=grid_spec,
        out_shape=[jax.ShapeDtypeStruct((nb, t, D_MODEL), F32), jax.ShapeDtypeStruct((n_seq, SUBLANES, DIFF_V), F32)],
        compiler_params=_cparams(("arbitrary", "arbitrary")),
        name="post_attn",
    )(page_table.reshape(-1), x, mods, og, od, gg, gd, *wts, *lam_w, diff_g, bias_last, bias_new, far, q8, kn8, vn8,
      cache_k, cache_v)


def kernel(x_prompt, x_sample, c_prompt, c_sample, cache_k, cache_v, page_table, state_conv, state_gdn, rel_bias,
           w_in, w_conv, a_log, dt_bias, gdn_norm_g, lam_q1, lam_k1, lam_q2, lam_k2, diff_norm_g, w_br_gdn,
           w_br_diff, w_o, w_ada, b_ada, ln1_g, ln1_b, ln2_g, ln2_b, w_up, b_up, w_down, b_down):
    assert DEPTH == 1 and w_in.shape[0] == 1
    l = 0
    lam_init = 0.8 - 0.6 * math.exp(-0.3 * l)
    nbp, seq, _ = x_prompt.shape
    nbs, dec_seq, _ = x_sample.shape
    n_tok_s = nbs * dec_seq
    past_len = page_table.shape[1] * PAGE_SIZE
    assert seq % Q_TILE == 0 and CONV_W - 1 <= dec_seq and 2 * dec_seq == SUBLANES
    assert page_table.shape[1] % PAGES_PER_STEP == 0 and nbs % (8 * SEQ_PER_ITER) == 0

    offs = np.concatenate([[0], np.cumsum(IN_SPLITS)])
    col = lambda i: w_in[l][:, int(offs[i]):int(offs[i + 1])]
    w_main = jnp.concatenate([col(0), col(1), col(4), col(5), col(6), col(7), col(8)], axis=1).astype(BF16)
    w_small = jnp.concatenate([col(2), col(3), jnp.zeros((D_MODEL, LANES - 2 * H_GDN), F32)], axis=1).astype(BF16)
    pad_gate = lambda a: jnp.zeros((1, LANES), F32).at[0, H_GDN:2 * H_GDN].set(a)
    alog_pad, dtb_pad = pad_gate(a_log[l]), pad_gate(dt_bias[l])
    gng = gdn_norm_g[l].reshape(1, DV_GDN)
    diff_g = diff_norm_g[l].reshape(1, HEAD_W)
    lam_w = (lam_q1[l].reshape(1, -1), lam_k1[l].reshape(1, -1), lam_q2[l].reshape(1, -1), lam_k2[l].reshape(1, -1))
    r2 = lambda a: a.reshape(1, -1)
    post_w = (w_br_gdn[l].astype(BF16), w_br_diff[l].astype(BF16), w_o[l].astype(BF16), r2(ln1_g[l]), r2(ln1_b[l]),
              w_up[l].astype(BF16), r2(b_up[l]), w_down[l].astype(BF16), r2(b_down[l]), r2(ln2_g[l]), r2(ln2_b[l]))

    ada = _ada(jnp.concatenate([c_prompt, c_sample], axis=0), w_ada[l].astype(BF16), r2(b_ada[l]))
    mods_p = ada[:nbp].reshape(nbp, 6, D_MODEL)
    mods_s = jnp.repeat(ada[nbp:].reshape(nbs, 6, D_MODEL), dec_seq, axis=0).transpose(1, 0, 2)
    bias_p, bias_last, bias_new, far = _bias_prep(rel_bias, past_len, dec_seq)

    conv_p, z_p, q_p, k_p, v_p, gg_p, gd_p, ba_p = _inproj(x_prompt, mods_p, w_main, w_small, 512, False)
    og_p, s_p = _gdn_prompt(conv_p, z_p, ba_p, w_conv[l], alog_pad, dtb_pad, gng, 4 * CHUNK)
    od_p = _attn_prompt(lam_w, diff_g, bias_p, q_p, k_p, v_p, lam_init)

    xs = x_sample.reshape(1, n_tok_s, D_MODEL)
    conv_s, z_s, q_s, k_s, v_s, gg_s, gd_s, ba_s = _inproj(xs, mods_s, w_main, w_small, n_tok_s, True)
    per_seq = lambda a: a.reshape(nbs, dec_seq, a.shape[-1])
    pad8 = lambda a: jnp.pad(a.astype(F32), ((0, 0), (0, SUBLANES - dec_seq), (0, 0)))
    conv_s3 = per_seq(conv_s)
    upx = jnp.concatenate([jnp.zeros((nbs, SUBLANES - (CONV_W - 1), GDN_CONV_CH), F32), state_conv[l], conv_s3,
                           jnp.zeros((nbs, SUBLANES - dec_seq, GDN_CONV_CH), F32)], axis=1)
    og_s8, s_s = _gdn_sample(upx, pad8(per_seq(z_s)), pad8(per_seq(ba_s)), state_gdn[l], w_conv[l], alog_pad,
                             dtb_pad, gng, dec_seq, 8)
    ck = jnp.transpose(cache_k[l], (0, 2, 3, 4, 1)).reshape(cache_k.shape[1], DIFF_QK, PAGE_SIZE)
    cv = cache_v[l].reshape(cache_v.shape[1], PAGE_SIZE * H_DIFF, HEAD_W)
    q_s3 = per_seq(q_s).astype(F32)
    y_p, od_s8 = _post_attn(x_prompt, mods_p, og_p, od_p, gg_p, gd_p, post_w, seq * nbp // nbs, page_table, lam_w,
                            diff_g, bias_last, bias_new, far, jnp.concatenate([q_s3, q_s3], axis=1),
                            pad8(per_seq(k_s)), pad8(v_s.reshape(nbs, dec_seq, DIFF_V)), ck, cv, lam_init, dec_seq)
    flat = lambda a8: a8[:, :dec_seq].reshape(1, n_tok_s, a8.shape[-1]).astype(BF16)
    y_s = _post(xs, mods_s, flat(og_s8), flat(od_s8), gg_s, gd_s, post_w, 256, True)

    hd = lambda a, b: a.reshape(1, b, -1, H_DIFF, 2, DH_DIFF)
    hv = lambda a, b: a.reshape(1, b, -1, H_DIFF, 2 * DH_DIFF)
    return (y_p, y_s.reshape(nbs, dec_seq, D_MODEL),
            hd(k_p, nbp), hv(v_p, nbp), conv_p[:, seq - (CONV_W - 1):][None], s_p[None],
            hd(k_s, nbs), hv(v_s, nbs), conv_s3[:, dec_seq - (CONV_W - 1):][None], s_s[None])
```

```python
import functools
import math

import numpy as np
import jax
import jax.numpy as jnp
from jax import lax
from jax.experimental import pallas as pl
from jax.experimental.pallas import tpu as pltpu

F32 = jnp.float32
BF16 = jnp.bfloat16

D_MODEL = 1024
DEPTH = 1
PAGE_SIZE = 128
H_GDN = 4
DK_GDN = 128
DV_GDN = 128
CONV_W = 4
CHUNK = 64
GDN_NORM_EPS = 1e-6
H_DIFF = 4
DH_DIFF = 64
DIFF_NORM_EPS = 1e-5
N_BUCKETS = 32
MAX_DISTANCE = 128
D_FF = 4 * D_MODEL
LN_EPS = 1e-5
GDN_QK = H_GDN * DK_GDN
GDN_V = H_GDN * DV_GDN
GDN_CONV_CH = 2 * GDN_QK + GDN_V
DIFF_QK = H_DIFF * 2 * DH_DIFF
DIFF_V = H_DIFF * 2 * DH_DIFF
HEAD_W = 2 * DH_DIFF
IN_SPLITS = (GDN_CONV_CH, GDN_V, H_GDN, H_GDN, DIFF_QK, DIFF_QK, DIFF_V, D_MODEL, D_MODEL)
ALPHA = (2.0 * DEPTH) ** 0.25

LANES = 128
SUBLANES = 8
VMEM_LIMIT = 56 * 1024 * 1024

Q_TILE = 256
PAGES_PER_STEP = 8
N_SLOTS = 4
SEQ_PER_ITER = 2
MASKED = N_BUCKETS

NN = (((1,), (0,)), ((), ()))
NT = (((1,), (1,)), ((), ()))
TN = (((0,), (0,)), ((), ()))


def _cparams(sem):
    return pltpu.CompilerParams(dimension_semantics=sem, vmem_limit_bytes=VMEM_LIMIT)


def _const_spec(shape):
    nd = len(shape)
    return pl.BlockSpec(shape, lambda *_: (0,) * nd, pipeline_mode=pl.Buffered(1))


def _sigmoid(x):
    return 1.0 / (1.0 + jnp.exp(-x))


def _silu(x):
    return x * _sigmoid(x)


def _softplus(x):
    return jnp.maximum(x, 0.0) + jnp.log(1.0 + jnp.exp(-jnp.abs(x)))


def _ln(x):
    mu = jnp.mean(x, axis=-1, keepdims=True)
    xc = x - mu
    var = jnp.mean(xc * xc, axis=-1, keepdims=True)
    return xc * lax.rsqrt(var + LN_EPS)


def _dot(a, b, dims=NN):
    return lax.dot_general(a, b, dims, preferred_element_type=F32)


def _dot_bf(a, b, dims=NN):
    return _dot(a.astype(BF16), b.astype(BF16), dims)


def _split(a):
    hi = a.astype(BF16)
    lo = (a - hi.astype(F32)).astype(BF16)
    return hi, lo


def _dot3(a, b, dims=NN, fuse=True):
    ah, al = _split(a)
    bh, bl = _split(b)
    if not fuse:
        return _dot(ah, bh, dims) + (_dot(ah, bl, dims) + _dot(al, bh, dims))
    lhs = jnp.concatenate([ah, ah, al], axis=1)
    rhs = jnp.concatenate([bh, bl, bh], axis=1 if dims == NT else 0)
    return _dot(lhs, rhs, dims)


def _dot_f32(a, b, dims=NN):
    return lax.dot_general(a, b, dims, precision=lax.Precision.HIGHEST, preferred_element_type=F32)


def _mod(mod_ref, i, per_token):
    return mod_ref[i] if per_token else mod_ref[i:i + 1, :]


def _ada_body(c_ref, w_ref, b_ref, o_ref):
    s = _silu(c_ref[...])
    o_ref[...] = _dot(s.astype(BF16), w_ref[...]) + b_ref[...]


def _ada(c_all, w_ada, b_ada):
    n = c_all.shape[0]
    return pl.pallas_call(
        _ada_body,
        grid=(6,),
        in_specs=[pl.BlockSpec((n, D_MODEL), lambda j: (0, 0)),
                  pl.BlockSpec((D_MODEL, D_MODEL), lambda j: (0, j)),
                  pl.BlockSpec((1, D_MODEL), lambda j: (0, j))],
        out_specs=pl.BlockSpec((n, D_MODEL), lambda j: (0, j)),
        out_shape=jax.ShapeDtypeStruct((n, 6 * D_MODEL), F32),
        compiler_params=_cparams(("arbitrary",)),
        name="ada",
    )(c_all, w_ada, b_ada)


def _t5_bucket_np(rel):
    n = np.maximum(rel, 0)
    max_exact = N_BUCKETS // 2
    nf = np.maximum(n, 1).astype(np.float32)
    large = max_exact + (np.log(nf / np.float32(max_exact)) / np.float32(math.log(MAX_DISTANCE / max_exact))
                         * np.float32(N_BUCKETS - max_exact)).astype(np.int32)
    large = np.minimum(large, N_BUCKETS - 1)
    return np.where(n < max_exact, n, large).astype(np.int32)


def _bias_codes(past_len, dec_seq):
    r = np.arange(Q_TILE)[:, None]
    c = np.arange(Q_TILE)[None, :]
    prev_tile = _t5_bucket_np(Q_TILE + r - c)
    diag_tile = np.where(r >= c, _t5_bucket_np(r - c), MASKED)
    prompt = np.concatenate([prev_tile, diag_tile], axis=1).astype(np.int32)
    prompt = np.concatenate([prompt, prompt], axis=0)
    nrow = H_DIFF * SUBLANES
    qrow = (np.arange(nrow) % dec_seq)[:, None]
    nkeys = PAGES_PER_STEP * PAGE_SIZE
    kpos = past_len - nkeys + np.arange(nkeys)[None, :]
    last = _t5_bucket_np(past_len + qrow - kpos).astype(np.int32)
    j = np.arange(SUBLANES)[None, :]
    new = np.where((j <= qrow) & (j < dec_seq), _t5_bucket_np(qrow - j), MASKED).astype(np.int32)
    return prompt, last, new


def _bias_body(tbl_ref, cp_ref, cl_ref, cn_ref, bp_ref, bl_ref, bn_ref, far_ref):
    def lookup(codes, h):
        acc = jnp.full(codes.shape, -jnp.inf, F32)
        for b in range(N_BUCKETS):
            acc = jnp.where(codes == b, tbl_ref[b, h], acc)
        return acc

    cp = cp_ref[...]
    for h in range(H_DIFF):
        bp_ref[h] = lookup(cp, h) - tbl_ref[N_BUCKETS - 1, h]
    cl = cl_ref[...]
    cn = cn_ref[...]
    nrow = cl.shape[0]
    row_head = lax.broadcasted_iota(jnp.int32, (nrow, 1), 0) // SUBLANES
    bl = jnp.zeros(cl.shape, F32)
    bn = jnp.zeros(cn.shape, F32)
    far = jnp.zeros((nrow, LANES), F32)
    for h in range(H_DIFF):
        bl = jnp.where(row_head == h, lookup(cl, h), bl)
        bn = jnp.where(row_head == h, lookup(cn, h), bn)
        far = jnp.where(row_head == h, tbl_ref[N_BUCKETS - 1, h], far)
    bl_ref[...] = bl
    bn_ref[...] = bn
    far_ref[...] = far


def _bias_prep(rel_bias, past_len, dec_seq):
    cp, cl, cn = _bias_codes(past_len, dec_seq)
    nrow = cl.shape[0]
    vm = pl.BlockSpec(memory_space=pltpu.VMEM)
    return pl.pallas_call(
        _bias_body,
        in_specs=[pl.BlockSpec(memory_space=pltpu.SMEM), vm, vm, vm],
        out_specs=[vm, vm, vm, vm],
        out_shape=[jax.ShapeDtypeStruct((H_DIFF,) + cp.shape, F32),
                   jax.ShapeDtypeStruct(cl.shape, F32),
                   jax.ShapeDtypeStruct(cn.shape, F32),
                   jax.ShapeDtypeStruct((nrow, LANES), F32)],
        name="bias_prep",
    )(rel_bias, jnp.asarray(cp), jnp.asarray(cl), jnp.asarray(cn))


def _lam(lq1_ref, lk1_ref, lq2_ref, lk2_ref, lam_init):
    s1 = jnp.sum(lq1_ref[...] * lk1_ref[...], axis=-1, keepdims=True)
    s2 = jnp.sum(lq2_ref[...] * lk2_ref[...], axis=-1, keepdims=True)
    return jnp.exp(s1) - jnp.exp(s2) + lam_init


_SEG = {"conv": (0, 1536), "z": (1536, 2048), "q": (2048, 2560), "k": (2560, 3072), "v": (3072, 3584),
        "gg": (3584, 4608), "gd": (4608, 5632)}
N_MAIN = 5632


def _inproj_h(x, mod_ref, per_token):
    h = _ln(x) * (1.0 + _mod(mod_ref, 1, per_token)) + _mod(mod_ref, 0, per_token)
    return h.astype(BF16)


def _inproj_seg(hb, wm_ref, name):
    a, b = _SEG[name]
    return _dot(hb, wm_ref[:, a:b])


def _inproj_rest(hb, r0, wm_ref, ws_ref, z_ref, q_ref, k_ref, v_ref, gg_ref, gd_ref, ba_ref):
    seg = functools.partial(_inproj_seg, hb, wm_ref)
    n = hb.shape[0]
    rows = slice(r0, r0 + n)
    z_ref[rows, :] = seg("z").astype(BF16)
    q_ref[rows, :] = seg("q").astype(BF16)
    k_ref[rows, :] = seg("k")
    sv = seg("v")
    for hh in range(H_DIFF):
        v_ref[pl.ds(r0 * H_DIFF + hh, n, stride=H_DIFF), :] = sv[:, hh * HEAD_W:(hh + 1) * HEAD_W]
    gg_ref[rows, :] = seg("gg").astype(BF16)
    gd_ref[rows, :] = seg("gd").astype(BF16)
    ba_ref[rows, :] = _dot(hb, ws_ref[...])


def _inproj_body(x_ref, mod_ref, wm_ref, ws_ref, conv_ref, z_ref, q_ref, k_ref, v_ref, gg_ref, gd_ref, ba_ref):
    hb = _inproj_h(x_ref[...], mod_ref, True)
    conv_ref[...] = _inproj_seg(hb, wm_ref, "conv")
    _inproj_rest(hb, 0, wm_ref, ws_ref, z_ref, q_ref, k_ref, v_ref, gg_ref, gd_ref, ba_ref)


def _inproj_conv_body(x_ref, mod_ref, wm_ref, ws_ref, wc_ref, qkv_ref, z_ref, q_ref, k_ref, v_ref, gg_ref, gd_ref,
                      ba_ref, tail_ref, cbuf_ref, *, tm):
    @pl.when(pl.program_id(1) == 0)
    def _():
        cbuf_ref[0:SUBLANES, :] = jnp.zeros((SUBLANES, GDN_CONV_CH), F32)

    sub = tm // 2
    hbs = [_inproj_h(x_ref[s * sub:(s + 1) * sub, :], mod_ref, False) for s in range(2)]
    for s in range(2):
        r0 = s * sub
        rows = slice(r0, r0 + sub)
        u_in = _inproj_seg(hbs[s], wm_ref, "conv")
        _inproj_rest(hbs[s], r0, wm_ref, ws_ref, z_ref, q_ref, k_ref, v_ref, gg_ref, gd_ref, ba_ref)
        cbuf_ref[SUBLANES + r0:SUBLANES + r0 + sub, :] = u_in
        y = u_in * wc_ref[CONV_W - 1:CONV_W, :]
        for j in range(CONV_W - 1):
            off = SUBLANES - (CONV_W - 1) + j + r0
            y = y + cbuf_ref[off:off + sub, :] * wc_ref[j:j + 1, :]
        qkv = _silu(y)
        for h in range(H_GDN):
            lo = h * DK_GDN
            qkv_ref[rows, lo:lo + DK_GDN] = _l2n(qkv[:, lo:lo + DK_GDN]) * (DK_GDN ** -0.5)
            qkv_ref[rows, GDN_QK + lo:GDN_QK + lo + DK_GDN] = _l2n(qkv[:, GDN_QK + lo:GDN_QK + lo + DK_GDN])
        qkv_ref[rows, 2 * GDN_QK:] = qkv[:, 2 * GDN_QK:]
    last = cbuf_ref[tm:tm + SUBLANES, :]
    cbuf_ref[0:SUBLANES, :] = last
    tail_ref[...] = last


def _inproj(x, mods, w_main, w_small, tm, w_conv=None):
    nb, t, _ = x.shape
    fuse = w_conv is not None

    def out(n, dt):
        return pl.BlockSpec((None, tm, n), lambda b, i: (b, i, 0)), jax.ShapeDtypeStruct((nb, t, n), dt)

    v_out = (pl.BlockSpec((None, tm * H_DIFF, HEAD_W), lambda b, i: (b, i, 0)),
             jax.ShapeDtypeStruct((nb, t * H_DIFF, HEAD_W), F32))
    outs = [out(GDN_CONV_CH, F32), out(GDN_V, BF16), out(DIFF_QK, BF16), out(DIFF_QK, F32), v_out,
            out(D_MODEL, BF16), out(D_MODEL, BF16), out(LANES, F32)]
    in_specs = [pl.BlockSpec((None, tm, D_MODEL), lambda b, i: (b, i, 0)),
                pl.BlockSpec((None, 6, D_MODEL), lambda b, i: (b, 0, 0)) if fuse
                else pl.BlockSpec((6, tm, D_MODEL), lambda b, i: (0, i, 0)),
                _const_spec((D_MODEL, N_MAIN)), _const_spec((D_MODEL, LANES))]
    args = [x, mods, w_main, w_small]
    scratch = []
    if fuse:
        in_specs.append(_const_spec((CONV_W, GDN_CONV_CH)))
        args.append(w_conv)
        outs.append((pl.BlockSpec((None, SUBLANES, GDN_CONV_CH), lambda b, i: (b, 0, 0)),
                     jax.ShapeDtypeStruct((nb, SUBLANES, GDN_CONV_CH), F32)))
        scratch.append(pltpu.VMEM((tm + SUBLANES, GDN_CONV_CH), F32))
    return pl.pallas_call(
        functools.partial(_inproj_conv_body, tm=tm) if fuse else _inproj_body,
        grid=(nb, t // tm),
        in_specs=in_specs,
        out_specs=[o[0] for o in outs],
        out_shape=[o[1] for o in outs],
        scratch_shapes=scratch,
        compiler_params=_cparams(("arbitrary", "arbitrary")),
        name="inproj_conv" if fuse else "inproj",
    )(*args)


def _gdn_solve(chains, c):
    ii = lax.broadcasted_iota(jnp.int32, (c, c), 0)
    jj = lax.broadcasted_iota(jnp.int32, (c, c), 1)
    incl = ii >= jj
    eye = (ii == jj).astype(F32)
    fuse = c % CHUNK == 0
    gamma = [jnp.where(incl, jnp.exp(jnp.where(incl, ch[4] - ch[5], 0.0)), 0.0) for ch in chains]
    kk = [_dot3(ch[1], ch[1], NT) for ch in chains]
    a = [jnp.where(ii > jj, ch[3] * k * g, 0.0) for ch, k, g in zip(chains, kk, gamma)]
    a8 = [jnp.where((ii >> 3) == (jj >> 3), m, 0.0) for m in a]
    p = [_dot3(m, m, fuse=fuse) for m in a8]
    x = [eye - m for m in a8]
    x = [xi + _dot3(xi, pi, fuse=fuse) for xi, pi in zip(x, p)]
    p = [_dot3(pi, pi, fuse=fuse) for pi in p]
    x = [xi + _dot3(xi, pi, fuse=fuse) for xi, pi in zip(x, p)]
    sh = 3
    while (1 << sh) < c:
        lvl = ((ii >> (sh + 1)) == (jj >> (sh + 1))) & ((ii >> sh) != (jj >> sh))
        t = [_dot3(jnp.where(lvl, m, 0.0), xi, fuse=fuse) for m, xi in zip(a, x)]
        x = [xi - _dot3(xi, ti, fuse=fuse) for xi, ti in zip(x, t)]
        sh += 1
    egc = [jnp.exp(ch[4]) for ch in chains]
    uw = [_dot3(xi, jnp.concatenate([ch[2] * ch[3], ch[1] * (ch[3] * e)], axis=1), fuse=fuse)
          for xi, ch, e in zip(x, chains, egc)]
    qk = [jnp.where(incl, _dot_bf(ch[0], ch[1], NT) * g, 0.0) for ch, g in zip(chains, gamma)]
    out = []
    for ch, e, uwi, qki in zip(chains, egc, uw, qk):
        g_last = ch[4][c - 1:c, :]
        wq = jnp.concatenate([uwi[:, DV_GDN:], ch[0] * e], axis=0).astype(BF16)
        k_dec = (ch[1] * jnp.exp(g_last - ch[4])).astype(BF16)
        out.append((uwi[:, :DV_GDN], wq, qki.astype(BF16), k_dec, jnp.exp(g_last)))
    return out


def _gdn_state_step(sols, states, c):
    sb = [s.astype(BF16) for s in states]
    wq = [_dot(sol[1], b) for sol, b in zip(sols, sb)]
    vb = [(sol[0] - m[:c]).astype(BF16) for sol, m in zip(sols, wq)]
    o = [m[c:] + _dot(sol[2], v) for sol, m, v in zip(sols, wq, vb)]
    s_new = [sol[4] * s + _dot(sol[3], v, TN) for sol, s, v in zip(sols, states, vb)]
    return o, s_new


def _l2n(x):
    return x * lax.rsqrt(jnp.sum(x * x, axis=-1, keepdims=True) + 1e-6)


def _gdn_gates(ba, alog_ref, dtb_ref):
    beta = _sigmoid(ba)
    g = -jnp.exp(alog_ref[...]) * _softplus(ba + dtb_ref[...])
    return beta, g


def _gdn_out(o, z, gng):
    o = o * lax.rsqrt(jnp.mean(o * o, axis=-1, keepdims=True) + GDN_NORM_EPS) * gng
    return o * _silu(z)


def _cumsum_mats(c):
    ii = lax.broadcasted_iota(jnp.int32, (c, c), 0)
    jj = lax.broadcasted_iota(jnp.int32, (c, c), 1)
    return (ii >= jj).astype(F32)


def _gdn_prompt_body(qkv_ref, z_ref, ba_ref, alog_ref, dtb_ref, gng_ref, og_ref, s_ref, *, tt):
    @pl.when(pl.program_id(1) == 0)
    def _():
        s_ref[...] = jnp.zeros(s_ref.shape, F32)

    beta_all, g_all = _gdn_gates(ba_ref[...], alog_ref, dtb_ref)
    ltri = _cumsum_mats(CHUNK)
    nch = tt // CHUNK
    gc_all = jnp.concatenate([_dot_f32(ltri, g_all[c * CHUNK:(c + 1) * CHUNK, :]) for c in range(nch)], axis=0)
    gc_t = gc_all.T
    gng = gng_ref[...]
    chains = []
    for c in range(nch):
        rows = slice(c * CHUNK, (c + 1) * CHUNK)
        for h in range(H_GDN):
            lo = h * DK_GDN
            chains.append((qkv_ref[rows, lo:lo + DK_GDN],
                           qkv_ref[rows, GDN_QK + lo:GDN_QK + lo + DK_GDN],
                           qkv_ref[rows, 2 * GDN_QK + lo:2 * GDN_QK + lo + DV_GDN],
                           beta_all[rows, h:h + 1], gc_all[rows, H_GDN + h:H_GDN + h + 1],
                           gc_t[H_GDN + h:H_GDN + h + 1, rows]))
    sols = _gdn_solve(chains, CHUNK)
    states = [s_ref[h] for h in range(H_GDN)]
    for c in range(nch):
        rows = slice(c * CHUNK, (c + 1) * CHUNK)
        o, states = _gdn_state_step(sols[c * H_GDN:(c + 1) * H_GDN], states, CHUNK)
        for h in range(H_GDN):
            cols = slice(h * DV_GDN, (h + 1) * DV_GDN)
            og_ref[rows, cols] = _gdn_out(o[h], z_ref[rows, cols].astype(F32), gng).astype(og_ref.dtype)
    for h in range(H_GDN):
        s_ref[h] = states[h]


def _gdn_prompt(qkv_act, z, ba, alog_pad, dtb_pad, gng, tt):
    nb, t, _ = qkv_act.shape
    row = lambda n: pl.BlockSpec((None, tt, n), lambda b, i: (b, i, 0))
    return pl.pallas_call(
        functools.partial(_gdn_prompt_body, tt=tt),
        grid=(nb, t // tt),
        in_specs=[row(GDN_CONV_CH), row(GDN_V), row(LANES),
                  _const_spec((1, LANES)), _const_spec((1, LANES)), _const_spec((1, DV_GDN))],
        out_specs=[row(GDN_V), pl.BlockSpec((None, H_GDN, DK_GDN, DV_GDN), lambda b, i: (b, 0, 0, 0))],
        out_shape=[jax.ShapeDtypeStruct((nb, t, GDN_V), BF16),
                   jax.ShapeDtypeStruct((nb, H_GDN, DK_GDN, DV_GDN), F32)],
        compiler_params=_cparams(("arbitrary", "arbitrary")),
        name="gdn_prompt",
    )(qkv_act, z, ba, alog_pad, dtb_pad, gng)


def _gdn_sample_body(upx_ref, z_ref, ba_ref, s0_ref, wc_ref, alog_ref, dtb_ref, gng_ref, og_ref, s_ref,
                     *, nb, n_valid):
    c = SUBLANES
    valid = lax.broadcasted_iota(jnp.int32, (c, 1), 0) < n_valid
    ltri = _cumsum_mats(c)
    gng = gng_ref[...]

    def body(i, carry):
        chains, states = [], []
        for n in (SEQ_PER_ITER * i + k for k in range(SEQ_PER_ITER)):
            y = upx_ref[n, pl.ds(SUBLANES, c), :] * wc_ref[CONV_W - 1:CONV_W, :]
            for j in range(CONV_W - 1):
                off = SUBLANES - (CONV_W - 1) + j
                y = y + upx_ref[n, pl.ds(off, c), :] * wc_ref[j:j + 1, :]
            qkv = _silu(y)
            beta_all, g_all = _gdn_gates(ba_ref[n], alog_ref, dtb_ref)
            beta_all = jnp.where(valid, beta_all, 0.0)
            g_all = jnp.where(valid, g_all, 0.0)
            gc_all = _dot_f32(ltri, g_all)
            gc_t = jnp.concatenate([gc_all, jnp.zeros((LANES - c, LANES), F32)], axis=0).T
            for h in range(H_GDN):
                lo = h * DK_GDN
                chains.append((_l2n(qkv[:, lo:lo + DK_GDN]) * (DK_GDN ** -0.5),
                               jnp.where(valid, _l2n(qkv[:, GDN_QK + lo:GDN_QK + lo + DK_GDN]), 0.0),
                               jnp.where(valid, qkv[:, 2 * GDN_QK + lo:2 * GDN_QK + lo + DV_GDN], 0.0),
                               beta_all[:, h:h + 1], gc_all[:, H_GDN + h:H_GDN + h + 1],
                               gc_t[H_GDN + h:H_GDN + h + 1, 0:c]))
                states.append(s0_ref[n, h])
        o, states = _gdn_state_step(_gdn_solve(chains, c), states, c)
        for k in range(SEQ_PER_ITER):
            n = SEQ_PER_ITER * i + k
            for h in range(H_GDN):
                cols = slice(h * DV_GDN, (h + 1) * DV_GDN)
                s_ref[n, h] = states[k * H_GDN + h]
                og_ref[n, :, cols] = _gdn_out(o[k * H_GDN + h], z_ref[n, :, cols], gng)
        return carry

    lax.fori_loop(0, nb // SEQ_PER_ITER, body, 0)


def _gdn_sample(upx, z8, ba8, s0, w_conv, alog_pad, dtb_pad, gng, n_valid, nb):
    n = upx.shape[0]
    blk = lambda *shape: pl.BlockSpec((nb,) + shape, lambda i: (i,) + (0,) * len(shape))
    return pl.pallas_call(
        functools.partial(_gdn_sample_body, nb=nb, n_valid=n_valid),
        grid=(n // nb,),
        in_specs=[blk(2 * SUBLANES, GDN_CONV_CH), blk(SUBLANES, GDN_V), blk(SUBLANES, LANES),
                  blk(H_GDN, DK_GDN, DV_GDN),
                  _const_spec((CONV_W, GDN_CONV_CH)), _const_spec((1, LANES)), _const_spec((1, LANES)),
                  _const_spec((1, DV_GDN))],
        out_specs=[blk(SUBLANES, GDN_V), blk(H_GDN, DK_GDN, DV_GDN)],
        out_shape=[jax.ShapeDtypeStruct((n, SUBLANES, GDN_V), F32),
                   jax.ShapeDtypeStruct((n, H_GDN, DK_GDN, DV_GDN), F32)],
        compiler_params=_cparams(("arbitrary",)),
        name="gdn_sample",
    )(upx, z8, ba8, s0, w_conv, alog_pad, dtb_pad, gng)


def _diff_norm(o, g, lam_init):
    return o * lax.rsqrt(jnp.mean(o * o, axis=-1, keepdims=True) + DIFF_NORM_EPS) * g * (1.0 - lam_init)


def _attn_prompt_body(lq1_ref, lk1_ref, lq2_ref, lk2_ref, g_ref, bias_ref, q_ref, k_ref, v_ref, o_ref,
                      *, lam_init, t):
    lam = _lam(lq1_ref, lk1_ref, lq2_ref, lk2_ref, lam_init)
    kb = k_ref[...].astype(BF16)
    vb = v_ref[pl.ds(pl.program_id(1), t, stride=H_DIFF), :].astype(BF16)
    g = g_ref[...]
    lane = lax.broadcasted_iota(jnp.int32, (1, HEAD_W), 1)
    zero = jnp.zeros((), BF16)
    for i in range(t // Q_TILE):
        r0 = i * Q_TILE
        qi = q_ref[r0:r0 + Q_TILE, :] * jnp.asarray(DH_DIFF ** -0.5, BF16)
        q2 = jnp.concatenate([jnp.where(lane < DH_DIFF, qi, zero), jnp.where(lane >= DH_DIFF, qi, zero)], axis=0)
        near0 = max(i - 1, 0) * Q_TILE
        sn = _dot(q2, kb[near0:r0 + Q_TILE], NT) + bias_ref[:, 2 * Q_TILE - (r0 + Q_TILE - near0):]
        m = jnp.max(sn, axis=-1, keepdims=True)
        if near0 > 0:
            sf = _dot(q2, kb[:near0], NT)
            m = jnp.maximum(m, jnp.max(sf, axis=-1, keepdims=True))
        en = jnp.exp(sn - m)
        l = jnp.sum(en, axis=-1, keepdims=True)
        o2 = _dot(en.astype(BF16), vb[near0:r0 + Q_TILE])
        if near0 > 0:
            ef = jnp.exp(sf - m)
            l = l + jnp.sum(ef, axis=-1, keepdims=True)
            o2 = o2 + _dot(ef.astype(BF16), vb[:near0])
        o2 = o2 * (1.0 / l)
        o = o2[:Q_TILE] - lam * o2[Q_TILE:]
        o_ref[r0:r0 + Q_TILE, :] = _diff_norm(o, g, lam_init).astype(o_ref.dtype)


def _attn_prompt(lam_w, diff_g, bias_p, q, k, v, lam_init):
    nb, t, _ = q.shape
    head = lambda: pl.BlockSpec((None, t, HEAD_W), lambda b, h: (b, 0, h))
    small = _const_spec((1, DH_DIFF))
    return pl.pallas_call(
        functools.partial(_attn_prompt_body, lam_init=lam_init, t=t),
        grid=(nb, H_DIFF),
        in_specs=[small, small, small, small, _const_spec((1, HEAD_W)),
                  pl.BlockSpec((None, 2 * Q_TILE, 2 * Q_TILE), lambda b, h: (h, 0, 0)),
                  head(), head(), pl.BlockSpec((None, t * H_DIFF, HEAD_W), lambda b, h: (b, 0, 0))],
        out_specs=head(),
        out_shape=jax.ShapeDtypeStruct((nb, t, DIFF_V), BF16),
        compiler_params=_cparams(("arbitrary", "arbitrary")),
        name="attn_prompt",
    )(*lam_w, diff_g, bias_p, q, k, v)


NROW_S = H_DIFF * SUBLANES
CHUNK_KEYS = PAGES_PER_STEP * PAGE_SIZE
_HEAD_ROWS = [slice(h * SUBLANES, (h + 1) * SUBLANES) for h in range(H_DIFF)]


def _page_copies(pt_ref, ck_hbm, cv_hbm, kbuf, vbuf, sem, chunk, slot):
    vrows = PAGE_SIZE * H_DIFF
    out = []
    for p in range(PAGES_PER_STEP):
        page = pt_ref[chunk * PAGES_PER_STEP + p]
        out.append(pltpu.make_async_copy(ck_hbm.at[page], kbuf.at[slot, :, pl.ds(p * PAGE_SIZE, PAGE_SIZE)],
                                         sem.at[0, slot]))
        out.append(pltpu.make_async_copy(cv_hbm.at[page], vbuf.at[slot, pl.ds(p * vrows, vrows)], sem.at[1, slot]))
    return out


def _sample_queries(q_ref, dec_seq):
    lane_hc = lax.broadcasted_iota(jnp.int32, (SUBLANES, DIFF_QK), 1) // DH_DIFF
    row_c = lax.broadcasted_iota(jnp.int32, (SUBLANES, DIFF_QK), 0) // dec_seq
    q2 = q_ref[...] * (DH_DIFF ** -0.5)
    return jnp.concatenate([jnp.where(lane_hc == 2 * h + row_c, q2, 0.0) for h in range(H_DIFF)],
                           axis=0).astype(BF16)


def _sample_attn_init(qbd, kn_ref, vn_ref, bn_ref):
    s = _dot(qbd, kn_ref[...].astype(BF16), NT) + bn_ref[...]
    m = jnp.max(s, axis=-1, keepdims=True)
    p = jnp.exp(s - m)
    pb = p.astype(BF16)
    vn = vn_ref[...].astype(BF16)
    acc = jnp.concatenate([_dot(pb[_HEAD_ROWS[h]], vn[:, h * HEAD_W:(h + 1) * HEAD_W]) for h in range(H_DIFF)],
                          axis=0)
    return m, jnp.sum(p, axis=-1, keepdims=True), acc


def _sample_attn_update(state, s, vbuf, slot):
    m_old, l_old, acc = state
    m_new = jnp.maximum(m_old, jnp.max(s, axis=-1, keepdims=True))
    alpha = jnp.exp(m_old - m_new)
    p = jnp.exp(s - m_new)
    pb = p.astype(BF16)
    pv = jnp.concatenate(
        [_dot(pb[_HEAD_ROWS[h]], vbuf[slot, pl.ds(h, CHUNK_KEYS, stride=H_DIFF), :].astype(BF16))
         for h in range(H_DIFF)], axis=0)
    return m_new, alpha * l_old + jnp.sum(p, axis=-1, keepdims=True), alpha * acc + pv


def _sample_attn_finish(state, lam, g, lam_init, dec_seq):
    _, l, acc = state
    row = lax.broadcasted_iota(jnp.int32, (NROW_S, 1), 0)
    coef = jnp.where((row % SUBLANES) < dec_seq, 1.0, -lam) / l
    ri = lax.broadcasted_iota(jnp.int32, (NROW_S, NROW_S), 0)
    rj = lax.broadcasted_iota(jnp.int32, (NROW_S, NROW_S), 1)
    sel = ((ri // SUBLANES == rj // SUBLANES) & (ri % SUBLANES == rj % dec_seq)).astype(F32)
    o = _dot_f32(sel, acc * coef)
    return jnp.concatenate([_diff_norm(o[_HEAD_ROWS[h]], g, lam_init) for h in range(H_DIFF)], axis=1)


def _post_body(x_ref, mod_ref, og_ref, od_ref, gg_ref, gd_ref, wbg_ref, wbd_ref, wo_ref, l1g_ref, l1b_ref,
               wup_ref, bup_ref, wdn_ref, bdn_ref, l2g_ref, l2b_ref, y_ref, *, per_token):
    md = lambda i: _mod(mod_ref, i, per_token)
    m = (_sigmoid(gg_ref[...].astype(F32)) * _dot(og_ref[...], wbg_ref[...])
         + _sigmoid(gd_ref[...].astype(F32)) * _dot(od_ref[...], wbd_ref[...]))
    mix = _dot(m.astype(BF16), wo_ref[...])
    x1 = _ln(ALPHA * x_ref[...] + md(2) * mix) * l1g_ref[...] + l1b_ref[...]
    h2 = _ln(x1) * (1.0 + md(4)) + md(3)
    up = _dot(h2.astype(BF16), wup_ref[...]) + bup_ref[...]
    act = jnp.square(jnp.maximum(up, 0.0))
    f = _dot(act.astype(BF16), wdn_ref[...]) + bdn_ref[...]
    y_ref[...] = _ln(ALPHA * x1 + md(5) * f) * l2g_ref[...] + l2b_ref[...]


def _post(x, mods, og, od, gg, gd, wts, tm, per_token):
    nb, t, _ = x.shape
    if per_token:
        mod_spec = pl.BlockSpec((6, tm, D_MODEL), lambda b, i: (0, i, 0))
    else:
        mod_spec = pl.BlockSpec((None, 6, D_MODEL), lambda b, i: (b, 0, 0))
    row = lambda n: pl.BlockSpec((None, tm, n), lambda b, i: (b, i, 0))
    return pl.pallas_call(
        functools.partial(_post_body, per_token=per_token),
        grid=(nb, t // tm),
        in_specs=[row(D_MODEL), mod_spec, row(GDN_V), row(DIFF_V), row(D_MODEL), row(D_MODEL)]
                 + [_const_spec(w.shape) for w in wts],
        out_specs=row(D_MODEL),
        out_shape=jax.ShapeDtypeStruct((nb, t, D_MODEL), F32),
        compiler_params=_cparams(("arbitrary", "arbitrary")),
        name="post",
    )(x, mods, og, od, gg, gd, *wts)


def _post_attn_body(pt_ref, x_ref, mod_ref, og_ref, od_ref, gg_ref, gd_ref, wbg_ref, wbd_ref, wo_ref, l1g_ref, l1b_ref,
                    wup_ref, bup_ref, wdn_ref, bdn_ref, l2g_ref, l2b_ref,
                    lq1_ref, lk1_ref, lq2_ref, lk2_ref, g_ref, bl_ref, bn_ref, far_ref, q_ref, kn_ref, vn_ref,
                    ck_hbm, cv_hbm, y_ref, o_ref, kbuf, vbuf, sem, *, lam_init, n_chunks, dec_seq):
    seq = pl.program_id(0) * pl.num_programs(1) + pl.program_id(1)
    n_seq = pl.num_programs(0) * pl.num_programs(1)
    copies = functools.partial(_page_copies, pt_ref, ck_hbm, cv_hbm, kbuf, vbuf, sem)

    @pl.when(seq == 0)
    def _():
        for ahead in range(N_SLOTS - 1):
            for cp in copies(ahead, ahead):
                cp.start()

    md = lambda i: mod_ref[i:i + 1, :]
    m = (_sigmoid(gg_ref[...].astype(F32)) * _dot(og_ref[...], wbg_ref[...])
         + _sigmoid(gd_ref[...].astype(F32)) * _dot(od_ref[...], wbd_ref[...]))
    mix = _dot(m.astype(BF16), wo_ref[...])
    x1 = _ln(ALPHA * x_ref[...] + md(2) * mix) * l1g_ref[...] + l1b_ref[...]
    h2b = (_ln(x1) * (1.0 + md(4)) + md(3)).astype(BF16)

    qbd = _sample_queries(q_ref, dec_seq)
    state = _sample_attn_init(qbd, kn_ref, vn_ref, bn_ref)
    ff = D_FF // n_chunks
    f = None
    for j in range(n_chunks):
        slot = j % N_SLOTS
        nxt = j + N_SLOTS - 1
        if nxt < n_chunks:
            for cp in copies(seq * n_chunks + nxt, nxt % N_SLOTS):
                cp.start()
        else:
            @pl.when(seq + 1 < n_seq)
            def _():
                for cp in copies((seq + 1) * n_chunks + nxt - n_chunks, nxt % N_SLOTS):
                    cp.start()
        for cp in copies(seq * n_chunks + j, slot):
            cp.wait()
        s = _dot(qbd, kbuf[slot].astype(BF16)) + (bl_ref[...] if j == n_chunks - 1 else far_ref[:, 0:1])
        cols = slice(j * ff, (j + 1) * ff)
        up = _dot(h2b, wup_ref[:, cols]) + bup_ref[:, cols]
        part = _dot(jnp.square(jnp.maximum(up, 0.0)).astype(BF16), wdn_ref[cols, :])
        f = part if f is None else f + part
        state = _sample_attn_update(state, s, vbuf, slot)

    y_ref[...] = _ln(ALPHA * x1 + md(5) * (f + bdn_ref[...])) * l2g_ref[...] + l2b_ref[...]
    lam = _lam(lq1_ref, lk1_ref, lq2_ref, lk2_ref, lam_init)
    o_ref[...] = _sample_attn_finish(state, lam, g_ref[...], lam_init, dec_seq)


def _post_attn(x, mods, og, od, gg, gd, wts, tm, page_table, lam_w, diff_g, bias_last, bias_new, far, q8, kn8, vn8,
               cache_k, cache_v, lam_init, dec_seq):
    nb, t, _ = x.shape
    n_seq, n_pages = page_table.shape
    n_chunks = n_pages // PAGES_PER_STEP
    steps = t // tm
    assert nb * steps == n_seq and n_chunks % N_SLOTS == 0 and D_FF % n_chunks == 0 and n_seq * n_chunks >= N_SLOTS
    row = lambda n: pl.BlockSpec((None, tm, n), lambda b, i, pt: (b, i, 0))
    const = lambda shape: pl.BlockSpec(shape, lambda b, i, pt: (0,) * len(shape), pipeline_mode=pl.Buffered(1))
    per_seq = pl.BlockSpec((None, SUBLANES, DIFF_QK), lambda b, i, pt: (b * steps + i, 0, 0))
    grid_spec = pltpu.PrefetchScalarGridSpec(
        num_scalar_prefetch=1,
        grid=(nb, steps),
        in_specs=[row(D_MODEL), pl.BlockSpec((None, 6, D_MODEL), lambda b, i, pt: (b, 0, 0)),
                  row(GDN_V), row(DIFF_V), row(D_MODEL), row(D_MODEL)]
                 + [const(w.shape) for w in wts]
                 + [const((1, DH_DIFF))] * 4
                 + [const((1, HEAD_W)), const((NROW_S, CHUNK_KEYS)), const((NROW_S, SUBLANES)), const((NROW_S, LANES)),
                    per_seq, per_seq, per_seq,
                    pl.BlockSpec(memory_space=pl.ANY), pl.BlockSpec(memory_space=pl.ANY)],
        out_specs=[row(D_MODEL), per_seq],
        scratch_shapes=[pltpu.VMEM((N_SLOTS, DIFF_QK, CHUNK_KEYS), F32),
                        pltpu.VMEM((N_SLOTS, CHUNK_KEYS * H_DIFF, HEAD_W), F32),
                        pltpu.SemaphoreType.DMA((2, N_SLOTS))],
    )
    return pl.pallas_call(
        functools.partial(_post_attn_body, lam_init=lam_init, n_chunks=n_chunks, dec_seq=dec_seq),
        grid_spec=grid_spec,
        out_shape=[jax.ShapeDtypeStruct((nb, t, D_MODEL), F32), jax.ShapeDtypeStruct((n_seq, SUBLANES, DIFF_V), F32)],
        compiler_params=_cparams(("arbitrary", "arbitrary")),
        name="post_attn",
    )(page_table.reshape(-1), x, mods, og, od, gg, gd, *wts, *lam_w, diff_g, bias_last, bias_new, far, q8, kn8, vn8,
      cache_k, cache_v)


def kernel(x_prompt, x_sample, c_prompt, c_sample, cache_k, cache_v, page_table, state_conv, state_gdn, rel_bias,
           w_in, w_conv, a_log, dt_bias, gdn_norm_g, lam_q1, lam_k1, lam_q2, lam_k2, diff_norm_g, w_br_gdn,
           w_br_diff, w_o, w_ada, b_ada, ln1_g, ln1_b, ln2_g, ln2_b, w_up, b_up, w_down, b_down):
    assert DEPTH == 1 and w_in.shape[0] == 1
    l = 0
    lam_init = 0.8 - 0.6 * math.exp(-0.3 * l)
    nbp, seq, _ = x_prompt.shape
    nbs, dec_seq, _ = x_sample.shape
    n_tok_s = nbs * dec_seq
    past_len = page_table.shape[1] * PAGE_SIZE
    assert seq % Q_TILE == 0 and CONV_W - 1 <= dec_seq and 2 * dec_seq == SUBLANES
    assert page_table.shape[1] % PAGES_PER_STEP == 0 and nbs % (8 * SEQ_PER_ITER) == 0

    offs = np.concatenate([[0], np.cumsum(IN_SPLITS)])
    col = lambda i: w_in[l][:, int(offs[i]):int(offs[i + 1])]
    w_main = jnp.concatenate([col(0), col(1), col(4), col(5), col(6), col(7), col(8)], axis=1).astype(BF16)
    w_small = jnp.concatenate([col(2), col(3), jnp.zeros((D_MODEL, LANES - 2 * H_GDN), F32)], axis=1).astype(BF16)
    pad_gate = lambda a: jnp.zeros((1, LANES), F32).at[0, H_GDN:2 * H_GDN].set(a)
    alog_pad, dtb_pad = pad_gate(a_log[l]), pad_gate(dt_bias[l])
    gng = gdn_norm_g[l].reshape(1, DV_GDN)
    diff_g = diff_norm_g[l].reshape(1, HEAD_W)
    lam_w = (lam_q1[l].reshape(1, -1), lam_k1[l].reshape(1, -1), lam_q2[l].reshape(1, -1), lam_k2[l].reshape(1, -1))
    r2 = lambda a: a.reshape(1, -1)
    post_w = (w_br_gdn[l].astype(BF16), w_br_diff[l].astype(BF16), w_o[l].astype(BF16), r2(ln1_g[l]), r2(ln1_b[l]),
              w_up[l].astype(BF16), r2(b_up[l]), w_down[l].astype(BF16), r2(b_down[l]), r2(ln2_g[l]), r2(ln2_b[l]))

    ada = _ada(jnp.concatenate([c_prompt, c_sample], axis=0), w_ada[l].astype(BF16), r2(b_ada[l]))
    mods_p = ada[:nbp].reshape(nbp, 6, D_MODEL)
    mods_s = jnp.repeat(ada[nbp:].reshape(nbs, 6, D_MODEL), dec_seq, axis=0).transpose(1, 0, 2)
    bias_p, bias_last, bias_new, far = _bias_prep(rel_bias, past_len, dec_seq)

    qkv_p, z_p, q_p, k_p, v_p, gg_p, gd_p, ba_p, tail_p = _inproj(x_prompt, mods_p, w_main, w_small, 512, w_conv[l])
    og_p, s_p = _gdn_prompt(qkv_p, z_p, ba_p, alog_pad, dtb_pad, gng, 4 * CHUNK)
    od_p = _attn_prompt(lam_w, diff_g, bias_p, q_p, k_p, v_p, lam_init)

    xs = x_sample.reshape(1, n_tok_s, D_MODEL)
    conv_s, z_s, q_s, k_s, v_s, gg_s, gd_s, ba_s = _inproj(xs, mods_s, w_main, w_small, n_tok_s)
    per_seq = lambda a: a.reshape(nbs, dec_seq, a.shape[-1])
    pad8 = lambda a: jnp.pad(a.astype(F32), ((0, 0), (0, SUBLANES - dec_seq), (0, 0)))
    conv_s3 = per_seq(conv_s)
    upx = jnp.concatenate([jnp.zeros((nbs, SUBLANES - (CONV_W - 1), GDN_CONV_CH), F32), state_conv[l], conv_s3,
                           jnp.zeros((nbs, SUBLANES - dec_seq, GDN_CONV_CH), F32)], axis=1)
    og_s8, s_s = _gdn_sample(upx, pad8(per_seq(z_s)), pad8(per_seq(ba_s)), state_gdn[l], w_conv[l], alog_pad,
                             dtb_pad, gng, dec_seq, 8)
    ck = jnp.transpose(cache_k[l], (0, 2, 3, 4, 1)).reshape(cache_k.shape[1], DIFF_QK, PAGE_SIZE)
    cv = cache_v[l].reshape(cache_v.shape[1], PAGE_SIZE * H_DIFF, HEAD_W)
    q_s3 = per_seq(q_s).astype(F32)
    y_p, od_s8 = _post_attn(x_prompt, mods_p, og_p, od_p, gg_p, gd_p, post_w, seq * nbp // nbs, page_table, lam_w,
                            diff_g, bias_last, bias_new, far, jnp.concatenate([q_s3, q_s3], axis=1),
                            pad8(per_seq(k_s)), pad8(v_s.reshape(nbs, dec_seq, DIFF_V)), ck, cv, lam_init, dec_seq)
    flat = lambda a8: a8[:, :dec_seq].reshape(1, n_tok_s, a8.shape[-1]).astype(BF16)
    y_s = _post(xs, mods_s, flat(og_s8), flat(od_s8), gg_s, gd_s, post_w, 256, True)

    hd = lambda a, b: a.reshape(1, b, -1, H_DIFF, 2, DH_DIFF)
    hv = lambda a, b: a.reshape(1, b, -1, H_DIFF, 2 * DH_DIFF)
    return (y_p, y_s.reshape(nbs, dec_seq, D_MODEL),
            hd(k_p, nbp), hv(v_p, nbp), tail_p[:, SUBLANES - (CONV_W - 1):][None], s_p[None],
            hd(k_s, nbs), hv(v_s, nbs), conv_s3[:, dec_seq - (CONV_W - 1):][None], s_s[None])
```

```python
import functools
import math

import numpy as np
import jax
import jax.numpy as jnp
from jax import lax
from jax.experimental import pallas as pl
from jax.experimental.pallas import tpu as pltpu

F32 = jnp.float32
BF16 = jnp.bfloat16

D_MODEL = 1024
DEPTH = 1
PAGE_SIZE = 128
H_GDN = 4
DK_GDN = 128
DV_GDN = 128
CONV_W = 4
CHUNK = 64
GDN_NORM_EPS = 1e-6
H_DIFF = 4
DH_DIFF = 64
DIFF_NORM_EPS = 1e-5
N_BUCKETS = 32
MAX_DISTANCE = 128
D_FF = 4 * D_MODEL
LN_EPS = 1e-5
GDN_QK = H_GDN * DK_GDN
GDN_V = H_GDN * DV_GDN
GDN_CONV_CH = 2 * GDN_QK + GDN_V
DIFF_QK = H_DIFF * 2 * DH_DIFF
DIFF_V = H_DIFF * 2 * DH_DIFF
HEAD_W = 2 * DH_DIFF
IN_SPLITS = (GDN_CONV_CH, GDN_V, H_GDN, H_GDN, DIFF_QK, DIFF_QK, DIFF_V, D_MODEL, D_MODEL)
ALPHA = (2.0 * DEPTH) ** 0.25

LANES = 128
SUBLANES = 8
VMEM_LIMIT = 56 * 1024 * 1024

Q_TILE = 256
PAGES_PER_STEP = 8
N_SLOTS = 4
SEQ_PER_ITER = 2
MASKED = N_BUCKETS

NN = (((1,), (0,)), ((), ()))
NT = (((1,), (1,)), ((), ()))
TN = (((0,), (0,)), ((), ()))


def _cparams(sem):
    return pltpu.CompilerParams(dimension_semantics=sem, vmem_limit_bytes=VMEM_LIMIT)


def _const_spec(shape):
    nd = len(shape)
    return pl.BlockSpec(shape, lambda *_: (0,) * nd, pipeline_mode=pl.Buffered(1))


def _sigmoid(x):
    return 1.0 / (1.0 + jnp.exp(-x))


def _silu(x):
    return x * _sigmoid(x)


def _softplus(x):
    return jnp.maximum(x, 0.0) + jnp.log(1.0 + jnp.exp(-jnp.abs(x)))


def _ln(x):
    mu = jnp.mean(x, axis=-1, keepdims=True)
    xc = x - mu
    var = jnp.mean(xc * xc, axis=-1, keepdims=True)
    return xc * lax.rsqrt(var + LN_EPS)


def _dot(a, b, dims=NN):
    return lax.dot_general(a, b, dims, preferred_element_type=F32)


def _dot_bf(a, b, dims=NN):
    return _dot(a.astype(BF16), b.astype(BF16), dims)


def _split(a):
    hi = a.astype(BF16)
    lo = (a - hi.astype(F32)).astype(BF16)
    return hi, lo


def _dot3(a, b, dims=NN, fuse=True):
    ah, al = _split(a)
    bh, bl = _split(b)
    if not fuse:
        return _dot(ah, bh, dims) + (_dot(ah, bl, dims) + _dot(al, bh, dims))
    lhs = jnp.concatenate([ah, ah, al], axis=1)
    rhs = jnp.concatenate([bh, bl, bh], axis=1 if dims == NT else 0)
    return _dot(lhs, rhs, dims)


def _dot_f32(a, b, dims=NN):
    return lax.dot_general(a, b, dims, precision=lax.Precision.HIGHEST, preferred_element_type=F32)


def _mod(mod_ref, i, per_token):
    return mod_ref[i] if per_token else mod_ref[i:i + 1, :]


def _ada_body(c_ref, w_ref, b_ref, o_ref):
    s = _silu(c_ref[...])
    o_ref[...] = _dot(s.astype(BF16), w_ref[...]) + b_ref[...]


def _ada(c_all, w_ada, b_ada):
    n = c_all.shape[0]
    return pl.pallas_call(
        _ada_body,
        grid=(6,),
        in_specs=[pl.BlockSpec((n, D_MODEL), lambda j: (0, 0)),
                  pl.BlockSpec((D_MODEL, D_MODEL), lambda j: (0, j)),
                  pl.BlockSpec((1, D_MODEL), lambda j: (0, j))],
        out_specs=pl.BlockSpec((n, D_MODEL), lambda j: (0, j)),
        out_shape=jax.ShapeDtypeStruct((n, 6 * D_MODEL), F32),
        compiler_params=_cparams(("arbitrary",)),
        name="ada",
    )(c_all, w_ada, b_ada)


def _t5_bucket_np(rel):
    n = np.maximum(rel, 0)
    max_exact = N_BUCKETS // 2
    nf = np.maximum(n, 1).astype(np.float32)
    large = max_exact + (np.log(nf / np.float32(max_exact)) / np.float32(math.log(MAX_DISTANCE / max_exact))
                         * np.float32(N_BUCKETS - max_exact)).astype(np.int32)
    large = np.minimum(large, N_BUCKETS - 1)
    return np.where(n < max_exact, n, large).astype(np.int32)


def _bias_codes(past_len, dec_seq):
    r = np.arange(Q_TILE)[:, None]
    c = np.arange(Q_TILE)[None, :]
    prev_tile = _t5_bucket_np(Q_TILE + r - c)
    diag_tile = np.where(r >= c, _t5_bucket_np(r - c), MASKED)
    prompt = np.concatenate([prev_tile, diag_tile], axis=1).astype(np.int32)
    prompt = np.concatenate([prompt, prompt], axis=0)
    nrow = H_DIFF * SUBLANES
    qrow = (np.arange(nrow) % dec_seq)[:, None]
    nkeys = PAGES_PER_STEP * PAGE_SIZE
    kpos = past_len - nkeys + np.arange(nkeys)[None, :]
    last = _t5_bucket_np(past_len + qrow - kpos).astype(np.int32)
    j = np.arange(SUBLANES)[None, :]
    new = np.where((j <= qrow) & (j < dec_seq), _t5_bucket_np(qrow - j), MASKED).astype(np.int32)
    return prompt, last, new


def _bias_body(tbl_ref, cp_ref, cl_ref, cn_ref, bp_ref, bl_ref, bn_ref, far_ref):
    def lookup(codes, h):
        acc = jnp.full(codes.shape, -jnp.inf, F32)
        for b in range(N_BUCKETS):
            acc = jnp.where(codes == b, tbl_ref[b, h], acc)
        return acc

    cp = cp_ref[...]
    for h in range(H_DIFF):
        bp_ref[h] = lookup(cp, h) - tbl_ref[N_BUCKETS - 1, h]
    cl = cl_ref[...]
    cn = cn_ref[...]
    nrow = cl.shape[0]
    row_head = lax.broadcasted_iota(jnp.int32, (nrow, 1), 0) // SUBLANES
    bl = jnp.zeros(cl.shape, F32)
    bn = jnp.zeros(cn.shape, F32)
    far = jnp.zeros((nrow, LANES), F32)
    for h in range(H_DIFF):
        bl = jnp.where(row_head == h, lookup(cl, h), bl)
        bn = jnp.where(row_head == h, lookup(cn, h), bn)
        far = jnp.where(row_head == h, tbl_ref[N_BUCKETS - 1, h], far)
    bl_ref[...] = bl
    bn_ref[...] = bn
    far_ref[...] = far


def _bias_prep(rel_bias, past_len, dec_seq):
    cp, cl, cn = _bias_codes(past_len, dec_seq)
    nrow = cl.shape[0]
    vm = pl.BlockSpec(memory_space=pltpu.VMEM)
    return pl.pallas_call(
        _bias_body,
        in_specs=[pl.BlockSpec(memory_space=pltpu.SMEM), vm, vm, vm],
        out_specs=[vm, vm, vm, vm],
        out_shape=[jax.ShapeDtypeStruct((H_DIFF,) + cp.shape, F32),
                   jax.ShapeDtypeStruct(cl.shape, F32),
                   jax.ShapeDtypeStruct(cn.shape, F32),
                   jax.ShapeDtypeStruct((nrow, LANES), F32)],
        name="bias_prep",
    )(rel_bias, jnp.asarray(cp), jnp.asarray(cl), jnp.asarray(cn))


def _lam(lq1_ref, lk1_ref, lq2_ref, lk2_ref, lam_init):
    s1 = jnp.sum(lq1_ref[...] * lk1_ref[...], axis=-1, keepdims=True)
    s2 = jnp.sum(lq2_ref[...] * lk2_ref[...], axis=-1, keepdims=True)
    return jnp.exp(s1) - jnp.exp(s2) + lam_init


_SEG = {"conv": (0, 1536), "z": (1536, 2048), "q": (2048, 2560), "k": (2560, 3072), "v": (3072, 3584),
        "gg": (3584, 4608), "gd": (4608, 5632)}
N_MAIN = 5632


def _inproj_h(x, mod_ref, per_token):
    h = _ln(x) * (1.0 + _mod(mod_ref, 1, per_token)) + _mod(mod_ref, 0, per_token)
    return h.astype(BF16)


def _inproj_seg(hb, wm_ref, name):
    a, b = _SEG[name]
    return _dot(hb, wm_ref[:, a:b])


def _inproj_rest(hb, r0, wm_ref, ws_ref, z_ref, q_ref, k_ref, v_ref, gg_ref, gd_ref, ba_ref):
    seg = functools.partial(_inproj_seg, hb, wm_ref)
    n = hb.shape[0]
    rows = slice(r0, r0 + n)
    z_ref[rows, :] = seg("z").astype(BF16)
    q_ref[rows, :] = seg("q").astype(BF16)
    k_ref[rows, :] = seg("k")
    sv = seg("v")
    for hh in range(H_DIFF):
        v_ref[pl.ds(r0 * H_DIFF + hh, n, stride=H_DIFF), :] = sv[:, hh * HEAD_W:(hh + 1) * HEAD_W]
    gg_ref[rows, :] = seg("gg").astype(BF16)
    gd_ref[rows, :] = seg("gd").astype(BF16)
    ba_ref[rows, :] = _dot(hb, ws_ref[...])


def _inproj_body(x_ref, mod_ref, wm_ref, ws_ref, conv_ref, z_ref, q_ref, k_ref, v_ref, gg_ref, gd_ref, ba_ref):
    hb = _inproj_h(x_ref[...], mod_ref, True)
    conv_ref[...] = _inproj_seg(hb, wm_ref, "conv")
    _inproj_rest(hb, 0, wm_ref, ws_ref, z_ref, q_ref, k_ref, v_ref, gg_ref, gd_ref, ba_ref)


def _inproj_conv_body(x_ref, mod_ref, wm_ref, ws_ref, wc_ref, qkv_ref, z_ref, q_ref, k_ref, v_ref, gg_ref, gd_ref,
                      ba_ref, tail_ref, cbuf_ref, *, tm):
    @pl.when(pl.program_id(1) == 0)
    def _():
        cbuf_ref[0:SUBLANES, :] = jnp.zeros((SUBLANES, GDN_CONV_CH), F32)

    sub = tm // 2
    hbs = [_inproj_h(x_ref[s * sub:(s + 1) * sub, :], mod_ref, False) for s in range(2)]
    for s in range(2):
        r0 = s * sub
        rows = slice(r0, r0 + sub)
        u_in = _inproj_seg(hbs[s], wm_ref, "conv")
        _inproj_rest(hbs[s], r0, wm_ref, ws_ref, z_ref, q_ref, k_ref, v_ref, gg_ref, gd_ref, ba_ref)
        cbuf_ref[SUBLANES + r0:SUBLANES + r0 + sub, :] = u_in
        y = u_in * wc_ref[CONV_W - 1:CONV_W, :]
        for j in range(CONV_W - 1):
            off = SUBLANES - (CONV_W - 1) + j + r0
            y = y + cbuf_ref[off:off + sub, :] * wc_ref[j:j + 1, :]
        qkv = _silu(y)
        for h in range(H_GDN):
            lo = h * DK_GDN
            qkv_ref[rows, lo:lo + DK_GDN] = _l2n(qkv[:, lo:lo + DK_GDN]) * (DK_GDN ** -0.5)
            qkv_ref[rows, GDN_QK + lo:GDN_QK + lo + DK_GDN] = _l2n(qkv[:, GDN_QK + lo:GDN_QK + lo + DK_GDN])
        qkv_ref[rows, 2 * GDN_QK:] = qkv[:, 2 * GDN_QK:]
    last = cbuf_ref[tm:tm + SUBLANES, :]
    cbuf_ref[0:SUBLANES, :] = last
    tail_ref[...] = last


def _inproj(x, mods, w_main, w_small, tm, w_conv=None):
    nb, t, _ = x.shape
    fuse = w_conv is not None

    def out(n, dt):
        return pl.BlockSpec((None, tm, n), lambda b, i: (b, i, 0)), jax.ShapeDtypeStruct((nb, t, n), dt)

    v_out = (pl.BlockSpec((None, tm * H_DIFF, HEAD_W), lambda b, i: (b, i, 0)),
             jax.ShapeDtypeStruct((nb, t * H_DIFF, HEAD_W), F32))
    outs = [out(GDN_CONV_CH, F32), out(GDN_V, BF16), out(DIFF_QK, BF16), out(DIFF_QK, F32), v_out,
            out(D_MODEL, BF16), out(D_MODEL, BF16), out(LANES, F32)]
    in_specs = [pl.BlockSpec((None, tm, D_MODEL), lambda b, i: (b, i, 0)),
                pl.BlockSpec((None, 6, D_MODEL), lambda b, i: (b, 0, 0)) if fuse
                else pl.BlockSpec((6, tm, D_MODEL), lambda b, i: (0, i, 0)),
                _const_spec((D_MODEL, N_MAIN)), _const_spec((D_MODEL, LANES))]
    args = [x, mods, w_main, w_small]
    scratch = []
    if fuse:
        in_specs.append(_const_spec((CONV_W, GDN_CONV_CH)))
        args.append(w_conv)
        outs.append((pl.BlockSpec((None, SUBLANES, GDN_CONV_CH), lambda b, i: (b, 0, 0)),
                     jax.ShapeDtypeStruct((nb, SUBLANES, GDN_CONV_CH), F32)))
        scratch.append(pltpu.VMEM((tm + SUBLANES, GDN_CONV_CH), F32))
    return pl.pallas_call(
        functools.partial(_inproj_conv_body, tm=tm) if fuse else _inproj_body,
        grid=(nb, t // tm),
        in_specs=in_specs,
        out_specs=[o[0] for o in outs],
        out_shape=[o[1] for o in outs],
        scratch_shapes=scratch,
        compiler_params=_cparams(("arbitrary", "arbitrary")),
        name="inproj_conv" if fuse else "inproj",
    )(*args)


def _gdn_solve(chains, c):
    ii = lax.broadcasted_iota(jnp.int32, (c, c), 0)
    jj = lax.broadcasted_iota(jnp.int32, (c, c), 1)
    incl = ii >= jj
    eye = (ii == jj).astype(F32)
    fuse = c % CHUNK == 0
    gamma = [jnp.where(incl, jnp.exp(jnp.where(incl, ch[4] - ch[5], 0.0)), 0.0) for ch in chains]
    kk = [_dot3(ch[1], ch[1], NT) for ch in chains]
    a = [jnp.where(ii > jj, ch[3] * k * g, 0.0) for ch, k, g in zip(chains, kk, gamma)]
    a8 = [jnp.where((ii >> 3) == (jj >> 3), m, 0.0) for m in a]
    p = [_dot3(m, m, fuse=fuse) for m in a8]
    x = [eye - m for m in a8]
    x = [xi + _dot3(xi, pi, fuse=fuse) for xi, pi in zip(x, p)]
    p = [_dot3(pi, pi, fuse=fuse) for pi in p]
    x = [xi + _dot3(xi, pi, fuse=fuse) for xi, pi in zip(x, p)]
    sh = 3
    while (1 << sh) < c:
        lvl = ((ii >> (sh + 1)) == (jj >> (sh + 1))) & ((ii >> sh) != (jj >> sh))
        t = [_dot3(jnp.where(lvl, m, 0.0), xi, fuse=fuse) for m, xi in zip(a, x)]
        x = [xi - _dot3(xi, ti, fuse=fuse) for xi, ti in zip(x, t)]
        sh += 1
    egc = [jnp.exp(ch[4]) for ch in chains]
    uw = [_dot3(xi, jnp.concatenate([ch[2] * ch[3], ch[1] * (ch[3] * e)], axis=1), fuse=fuse)
          for xi, ch, e in zip(x, chains, egc)]
    qk = [jnp.where(incl, _dot_bf(ch[0], ch[1], NT) * g, 0.0) for ch, g in zip(chains, gamma)]
    out = []
    for ch, e, uwi, qki in zip(chains, egc, uw, qk):
        g_last = ch[4][c - 1:c, :]
        wq = jnp.concatenate([uwi[:, DV_GDN:], ch[0] * e], axis=0).astype(BF16)
        k_dec = (ch[1] * jnp.exp(g_last - ch[4])).astype(BF16)
        out.append((uwi[:, :DV_GDN], wq, qki.astype(BF16), k_dec, jnp.exp(g_last)))
    return out


def _gdn_state_step(sols, states, c):
    sb = [s.astype(BF16) for s in states]
    wq = [_dot(sol[1], b) for sol, b in zip(sols, sb)]
    vb = [(sol[0] - m[:c]).astype(BF16) for sol, m in zip(sols, wq)]
    o = [m[c:] + _dot(sol[2], v) for sol, m, v in zip(sols, wq, vb)]
    s_new = [sol[4] * s + _dot(sol[3], v, TN) for sol, s, v in zip(sols, states, vb)]
    return o, s_new


def _l2n(x):
    return x * lax.rsqrt(jnp.sum(x * x, axis=-1, keepdims=True) + 1e-6)


def _gdn_gates(ba, alog_ref, dtb_ref):
    beta = _sigmoid(ba)
    g = -jnp.exp(alog_ref[...]) * _softplus(ba + dtb_ref[...])
    return beta, g


def _gdn_out(o, z, gng):
    o = o * lax.rsqrt(jnp.mean(o * o, axis=-1, keepdims=True) + GDN_NORM_EPS) * gng
    return o * _silu(z)


def _cumsum_mats(c):
    ii = lax.broadcasted_iota(jnp.int32, (c, c), 0)
    jj = lax.broadcasted_iota(jnp.int32, (c, c), 1)
    return (ii >= jj).astype(F32)


def _gdn_prompt_body(qkv_ref, z_ref, ba_ref, alog_ref, dtb_ref, gng_ref, og_ref, s_ref, *, tt):
    @pl.when(pl.program_id(1) == 0)
    def _():
        s_ref[...] = jnp.zeros(s_ref.shape, F32)

    beta_all, g_all = _gdn_gates(ba_ref[...], alog_ref, dtb_ref)
    ltri = _cumsum_mats(CHUNK)
    nch = tt // CHUNK
    gc_all = jnp.concatenate([_dot_f32(ltri, g_all[c * CHUNK:(c + 1) * CHUNK, :]) for c in range(nch)], axis=0)
    gc_t = gc_all.T
    gng = gng_ref[...]
    chains = []
    for c in range(nch):
        rows = slice(c * CHUNK, (c + 1) * CHUNK)
        for h in range(H_GDN):
            lo = h * DK_GDN
            chains.append((qkv_ref[rows, lo:lo + DK_GDN],
                           qkv_ref[rows, GDN_QK + lo:GDN_QK + lo + DK_GDN],
                           qkv_ref[rows, 2 * GDN_QK + lo:2 * GDN_QK + lo + DV_GDN],
                           beta_all[rows, h:h + 1], gc_all[rows, H_GDN + h:H_GDN + h + 1],
                           gc_t[H_GDN + h:H_GDN + h + 1, rows]))
    sols = _gdn_solve(chains, CHUNK)
    states = [s_ref[h] for h in range(H_GDN)]
    for c in range(nch):
        rows = slice(c * CHUNK, (c + 1) * CHUNK)
        o, states = _gdn_state_step(sols[c * H_GDN:(c + 1) * H_GDN], states, CHUNK)
        for h in range(H_GDN):
            cols = slice(h * DV_GDN, (h + 1) * DV_GDN)
            og_ref[rows, cols] = _gdn_out(o[h], z_ref[rows, cols].astype(F32), gng).astype(og_ref.dtype)
    for h in range(H_GDN):
        s_ref[h] = states[h]


def _gdn_prompt(qkv_act, z, ba, alog_pad, dtb_pad, gng, tt):
    nb, t, _ = qkv_act.shape
    row = lambda n: pl.BlockSpec((None, tt, n), lambda b, i: (b, i, 0))
    return pl.pallas_call(
        functools.partial(_gdn_prompt_body, tt=tt),
        grid=(nb, t // tt),
        in_specs=[row(GDN_CONV_CH), row(GDN_V), row(LANES),
                  _const_spec((1, LANES)), _const_spec((1, LANES)), _const_spec((1, DV_GDN))],
        out_specs=[row(GDN_V), pl.BlockSpec((None, H_GDN, DK_GDN, DV_GDN), lambda b, i: (b, 0, 0, 0))],
        out_shape=[jax.ShapeDtypeStruct((nb, t, GDN_V), BF16),
                   jax.ShapeDtypeStruct((nb, H_GDN, DK_GDN, DV_GDN), F32)],
        compiler_params=_cparams(("arbitrary", "arbitrary")),
        name="gdn_prompt",
    )(qkv_act, z, ba, alog_pad, dtb_pad, gng)


def _gdn_sample_body(upx_ref, z_ref, ba_ref, s0_ref, wc_ref, alog_ref, dtb_ref, gng_ref, og_ref, s_ref,
                     *, nb, n_valid):
    c = SUBLANES
    valid = lax.broadcasted_iota(jnp.int32, (c, 1), 0) < n_valid
    ltri = _cumsum_mats(c)
    gng = gng_ref[...]

    def body(i, carry):
        chains, states = [], []
        for n in (SEQ_PER_ITER * i + k for k in range(SEQ_PER_ITER)):
            y = upx_ref[n, pl.ds(SUBLANES, c), :] * wc_ref[CONV_W - 1:CONV_W, :]
            for j in range(CONV_W - 1):
                off = SUBLANES - (CONV_W - 1) + j
                y = y + upx_ref[n, pl.ds(off, c), :] * wc_ref[j:j + 1, :]
            qkv = _silu(y)
            beta_all, g_all = _gdn_gates(ba_ref[n], alog_ref, dtb_ref)
            beta_all = jnp.where(valid, beta_all, 0.0)
            g_all = jnp.where(valid, g_all, 0.0)
            gc_all = _dot_f32(ltri, g_all)
            gc_t = jnp.concatenate([gc_all, jnp.zeros((LANES - c, LANES), F32)], axis=0).T
            for h in range(H_GDN):
                lo = h * DK_GDN
                chains.append((_l2n(qkv[:, lo:lo + DK_GDN]) * (DK_GDN ** -0.5),
                               jnp.where(valid, _l2n(qkv[:, GDN_QK + lo:GDN_QK + lo + DK_GDN]), 0.0),
                               jnp.where(valid, qkv[:, 2 * GDN_QK + lo:2 * GDN_QK + lo + DV_GDN], 0.0),
                               beta_all[:, h:h + 1], gc_all[:, H_GDN + h:H_GDN + h + 1],
                               gc_t[H_GDN + h:H_GDN + h + 1, 0:c]))
                states.append(s0_ref[n, h])
        o, states = _gdn_state_step(_gdn_solve(chains, c), states, c)
        for k in range(SEQ_PER_ITER):
            n = SEQ_PER_ITER * i + k
            for h in range(H_GDN):
                cols = slice(h * DV_GDN, (h + 1) * DV_GDN)
                s_ref[n, h] = states[k * H_GDN + h]
                og_ref[n, :, cols] = _gdn_out(o[k * H_GDN + h], z_ref[n, :, cols], gng)
        return carry

    lax.fori_loop(0, nb // SEQ_PER_ITER, body, 0)


def _gdn_sample(upx, z8, ba8, s0, w_conv, alog_pad, dtb_pad, gng, n_valid, nb):
    n = upx.shape[0]
    blk = lambda *shape: pl.BlockSpec((nb,) + shape, lambda i: (i,) + (0,) * len(shape))
    return pl.pallas_call(
        functools.partial(_gdn_sample_body, nb=nb, n_valid=n_valid),
        grid=(n // nb,),
        in_specs=[blk(2 * SUBLANES, GDN_CONV_CH), blk(SUBLANES, GDN_V), blk(SUBLANES, LANES),
                  blk(H_GDN, DK_GDN, DV_GDN),
                  _const_spec((CONV_W, GDN_CONV_CH)), _const_spec((1, LANES)), _const_spec((1, LANES)),
                  _const_spec((1, DV_GDN))],
        out_specs=[blk(SUBLANES, GDN_V), blk(H_GDN, DK_GDN, DV_GDN)],
        out_shape=[jax.ShapeDtypeStruct((n, SUBLANES, GDN_V), F32),
                   jax.ShapeDtypeStruct((n, H_GDN, DK_GDN, DV_GDN), F32)],
        compiler_params=_cparams(("arbitrary",)),
        name="gdn_sample",
    )(upx, z8, ba8, s0, w_conv, alog_pad, dtb_pad, gng)


def _diff_norm(o, g, lam_init):
    return o * lax.rsqrt(jnp.mean(o * o, axis=-1, keepdims=True) + DIFF_NORM_EPS) * g * (1.0 - lam_init)


def _attn_prompt_body(lq1_ref, lk1_ref, lq2_ref, lk2_ref, g_ref, bias_ref, q_ref, k_ref, v_ref, o_ref,
                      *, lam_init, t):
    lam = _lam(lq1_ref, lk1_ref, lq2_ref, lk2_ref, lam_init)
    kb = k_ref[...].astype(BF16)
    vb = v_ref[pl.ds(pl.program_id(1), t, stride=H_DIFF), :].astype(BF16)
    vb = jnp.concatenate([vb, jnp.ones((t, HEAD_W), BF16)], axis=1)
    g = g_ref[...]
    lane = lax.broadcasted_iota(jnp.int32, (1, HEAD_W), 1)
    zero = jnp.zeros((), BF16)
    n_tiles = t // Q_TILE

    def scores(i):
        r0 = i * Q_TILE
        qi = q_ref[r0:r0 + Q_TILE, :] * jnp.asarray(DH_DIFF ** -0.5, BF16)
        q2 = jnp.concatenate([jnp.where(lane < DH_DIFF, qi, zero), jnp.where(lane >= DH_DIFF, qi, zero)], axis=0)
        near0 = max(i - 1, 0) * Q_TILE
        sn = _dot(q2, kb[near0:r0 + Q_TILE], NT) + bias_ref[:, 2 * Q_TILE - (r0 + Q_TILE - near0):]
        return sn, (_dot(q2, kb[:near0], NT) if near0 > 0 else None)

    pending = scores(0)
    for i in range(n_tiles):
        r0 = i * Q_TILE
        near0 = max(i - 1, 0) * Q_TILE
        sn, sf = pending
        if i + 1 < n_tiles:
            pending = scores(i + 1)
        m = jnp.max(sn, axis=-1, keepdims=True)
        if sf is not None:
            m = jnp.maximum(m, jnp.max(sf, axis=-1, keepdims=True))
        o2 = _dot(jnp.exp((sn - m).astype(BF16)), vb[near0:r0 + Q_TILE])
        if sf is not None:
            o2 = o2 + _dot(jnp.exp((sf - m).astype(BF16)), vb[:near0])
        o2 = o2[:, :HEAD_W] * (1.0 / o2[:, HEAD_W:HEAD_W + 1])
        o = o2[:Q_TILE] - lam * o2[Q_TILE:]
        o_ref[r0:r0 + Q_TILE, :] = _diff_norm(o, g, lam_init).astype(o_ref.dtype)


def _attn_prompt(lam_w, diff_g, bias_p, q, k, v, lam_init):
    nb, t, _ = q.shape
    head = lambda: pl.BlockSpec((None, t, HEAD_W), lambda b, h: (b, 0, h))
    small = _const_spec((1, DH_DIFF))
    return pl.pallas_call(
        functools.partial(_attn_prompt_body, lam_init=lam_init, t=t),
        grid=(nb, H_DIFF),
        in_specs=[small, small, small, small, _const_spec((1, HEAD_W)),
                  pl.BlockSpec((None, 2 * Q_TILE, 2 * Q_TILE), lambda b, h: (h, 0, 0)),
                  head(), head(), pl.BlockSpec((None, t * H_DIFF, HEAD_W), lambda b, h: (b, 0, 0))],
        out_specs=head(),
        out_shape=jax.ShapeDtypeStruct((nb, t, DIFF_V), BF16),
        compiler_params=_cparams(("arbitrary", "arbitrary")),
        name="attn_prompt",
    )(*lam_w, diff_g, bias_p, q, k, v)


NROW_S = H_DIFF * SUBLANES
CHUNK_KEYS = PAGES_PER_STEP * PAGE_SIZE
_HEAD_ROWS = [slice(h * SUBLANES, (h + 1) * SUBLANES) for h in range(H_DIFF)]


def _page_copies(pt_ref, ck_hbm, cv_hbm, kbuf, vbuf, sem, chunk, slot):
    vrows = PAGE_SIZE * H_DIFF
    out = []
    for p in range(PAGES_PER_STEP):
        page = pt_ref[chunk * PAGES_PER_STEP + p]
        out.append(pltpu.make_async_copy(ck_hbm.at[page], kbuf.at[slot, :, pl.ds(p * PAGE_SIZE, PAGE_SIZE)],
                                         sem.at[0, slot]))
        out.append(pltpu.make_async_copy(cv_hbm.at[page], vbuf.at[slot, pl.ds(p * vrows, vrows)], sem.at[1, slot]))
    return out


def _sample_queries(q_ref, dec_seq):
    lane_hc = lax.broadcasted_iota(jnp.int32, (SUBLANES, DIFF_QK), 1) // DH_DIFF
    row_c = lax.broadcasted_iota(jnp.int32, (SUBLANES, DIFF_QK), 0) // dec_seq
    q2 = q_ref[...] * (DH_DIFF ** -0.5)
    return jnp.concatenate([jnp.where(lane_hc == 2 * h + row_c, q2, 0.0) for h in range(H_DIFF)],
                           axis=0).astype(BF16)


def _sample_attn_init(qbd, kn_ref, vn_ref, bn_ref):
    s = _dot(qbd, kn_ref[...].astype(BF16), NT) + bn_ref[...]
    m = jnp.max(s, axis=-1, keepdims=True)
    p = jnp.exp(s - m)
    pb = p.astype(BF16)
    vn = vn_ref[...].astype(BF16)
    acc = jnp.concatenate([_dot(pb[_HEAD_ROWS[h]], vn[:, h * HEAD_W:(h + 1) * HEAD_W]) for h in range(H_DIFF)],
                          axis=0)
    return m, jnp.sum(p, axis=-1, keepdims=True), acc


def _sample_attn_update(state, s, vbuf, slot):
    m_old, l_old, acc = state
    m_new = jnp.maximum(m_old, jnp.max(s, axis=-1, keepdims=True))
    alpha = jnp.exp(m_old - m_new)
    p = jnp.exp(s - m_new)
    pb = p.astype(BF16)
    pv = jnp.concatenate(
        [_dot(pb[_HEAD_ROWS[h]], vbuf[slot, pl.ds(h, CHUNK_KEYS, stride=H_DIFF), :].astype(BF16))
         for h in range(H_DIFF)], axis=0)
    return m_new, alpha * l_old + jnp.sum(p, axis=-1, keepdims=True), alpha * acc + pv


def _sample_attn_finish(state, lam, g, lam_init, dec_seq):
    _, l, acc = state
    row = lax.broadcasted_iota(jnp.int32, (NROW_S, 1), 0)
    coef = jnp.where((row % SUBLANES) < dec_seq, 1.0, -lam) / l
    ri = lax.broadcasted_iota(jnp.int32, (NROW_S, NROW_S), 0)
    rj = lax.broadcasted_iota(jnp.int32, (NROW_S, NROW_S), 1)
    sel = ((ri // SUBLANES == rj // SUBLANES) & (ri % SUBLANES == rj % dec_seq)).astype(F32)
    o = _dot_f32(sel, acc * coef)
    return jnp.concatenate([_diff_norm(o[_HEAD_ROWS[h]], g, lam_init) for h in range(H_DIFF)], axis=1)


def _post_body(x_ref, mod_ref, og_ref, od_ref, gg_ref, gd_ref, wbg_ref, wbd_ref, wo_ref, l1g_ref, l1b_ref,
               wup_ref, bup_ref, wdn_ref, bdn_ref, l2g_ref, l2b_ref, y_ref, *, per_token):
    md = lambda i: _mod(mod_ref, i, per_token)
    m = (_sigmoid(gg_ref[...].astype(F32)) * _dot(og_ref[...], wbg_ref[...])
         + _sigmoid(gd_ref[...].astype(F32)) * _dot(od_ref[...], wbd_ref[...]))
    mix = _dot(m.astype(BF16), wo_ref[...])
    x1 = _ln(ALPHA * x_ref[...] + md(2) * mix) * l1g_ref[...] + l1b_ref[...]
    h2 = _ln(x1) * (1.0 + md(4)) + md(3)
    up = _dot(h2.astype(BF16), wup_ref[...]) + bup_ref[...]
    act = jnp.square(jnp.maximum(up, 0.0))
    f = _dot(act.astype(BF16), wdn_ref[...]) + bdn_ref[...]
    y_ref[...] = _ln(ALPHA * x1 + md(5) * f) * l2g_ref[...] + l2b_ref[...]


def _post(x, mods, og, od, gg, gd, wts, tm, per_token):
    nb, t, _ = x.shape
    if per_token:
        mod_spec = pl.BlockSpec((6, tm, D_MODEL), lambda b, i: (0, i, 0))
    else:
        mod_spec = pl.BlockSpec((None, 6, D_MODEL), lambda b, i: (b, 0, 0))
    row = lambda n: pl.BlockSpec((None, tm, n), lambda b, i: (b, i, 0))
    return pl.pallas_call(
        functools.partial(_post_body, per_token=per_token),
        grid=(nb, t // tm),
        in_specs=[row(D_MODEL), mod_spec, row(GDN_V), row(DIFF_V), row(D_MODEL), row(D_MODEL)]
                 + [_const_spec(w.shape) for w in wts],
        out_specs=row(D_MODEL),
        out_shape=jax.ShapeDtypeStruct((nb, t, D_MODEL), F32),
        compiler_params=_cparams(("arbitrary", "arbitrary")),
        name="post",
    )(x, mods, og, od, gg, gd, *wts)


def _post_attn_body(pt_ref, x_ref, mod_ref, og_ref, od_ref, gg_ref, gd_ref, wbg_ref, wbd_ref, wo_ref, l1g_ref, l1b_ref,
                    wup_ref, bup_ref, wdn_ref, bdn_ref, l2g_ref, l2b_ref,
                    lq1_ref, lk1_ref, lq2_ref, lk2_ref, g_ref, bl_ref, bn_ref, far_ref, q_ref, kn_ref, vn_ref,
                    ck_hbm, cv_hbm, y_ref, o_ref, kbuf, vbuf, sem, *, lam_init, n_chunks, dec_seq):
    seq = pl.program_id(0) * pl.num_programs(1) + pl.program_id(1)
    n_seq = pl.num_programs(0) * pl.num_programs(1)
    copies = functools.partial(_page_copies, pt_ref, ck_hbm, cv_hbm, kbuf, vbuf, sem)

    @pl.when(seq == 0)
    def _():
        for ahead in range(N_SLOTS - 1):
            for cp in copies(ahead, ahead):
                cp.start()

    md = lambda i: mod_ref[i:i + 1, :]
    m = (_sigmoid(gg_ref[...].astype(F32)) * _dot(og_ref[...], wbg_ref[...])
         + _sigmoid(gd_ref[...].astype(F32)) * _dot(od_ref[...], wbd_ref[...]))
    mix = _dot(m.astype(BF16), wo_ref[...])
    x1 = _ln(ALPHA * x_ref[...] + md(2) * mix) * l1g_ref[...] + l1b_ref[...]
    h2b = (_ln(x1) * (1.0 + md(4)) + md(3)).astype(BF16)

    qbd = _sample_queries(q_ref, dec_seq)
    state = _sample_attn_init(qbd, kn_ref, vn_ref, bn_ref)
    ff = D_FF // n_chunks
    f = None
    for j in range(n_chunks):
        slot = j % N_SLOTS
        nxt = j + N_SLOTS - 1
        if nxt < n_chunks:
            for cp in copies(seq * n_chunks + nxt, nxt % N_SLOTS):
                cp.start()
        else:
            @pl.when(seq + 1 < n_seq)
            def _():
                for cp in copies((seq + 1) * n_chunks + nxt - n_chunks, nxt % N_SLOTS):
                    cp.start()
        for cp in copies(seq * n_chunks + j, slot):
            cp.wait()
        s = _dot(qbd, kbuf[slot].astype(BF16)) + (bl_ref[...] if j == n_chunks - 1 else far_ref[:, 0:1])
        cols = slice(j * ff, (j + 1) * ff)
        up = _dot(h2b, wup_ref[:, cols]) + bup_ref[:, cols]
        part = _dot(jnp.square(jnp.maximum(up, 0.0)).astype(BF16), wdn_ref[cols, :])
        f = part if f is None else f + part
        state = _sample_attn_update(state, s, vbuf, slot)

    y_ref[...] = _ln(ALPHA * x1 + md(5) * (f + bdn_ref[...])) * l2g_ref[...] + l2b_ref[...]
    lam = _lam(lq1_ref, lk1_ref, lq2_ref, lk2_ref, lam_init)
    o_ref[...] = _sample_attn_finish(state, lam, g_ref[...], lam_init, dec_seq)


def _post_attn(x, mods, og, od, gg, gd, wts, tm, page_table, lam_w, diff_g, bias_last, bias_new, far, q8, kn8, vn8,
               cache_k, cache_v, lam_init, dec_seq):
    nb, t, _ = x.shape
    n_seq, n_pages = page_table.shape
    n_chunks = n_pages // PAGES_PER_STEP
    steps = t // tm
    assert nb * steps == n_seq and n_chunks % N_SLOTS == 0 and D_FF % n_chunks == 0 and n_seq * n_chunks >= N_SLOTS
    row = lambda n: pl.BlockSpec((None, tm, n), lambda b, i, pt: (b, i, 0))
    const = lambda shape: pl.BlockSpec(shape, lambda b, i, pt: (0,) * len(shape), pipeline_mode=pl.Buffered(1))
    per_seq = pl.BlockSpec((None, SUBLANES, DIFF_QK), lambda b, i, pt: (b * steps + i, 0, 0))
    grid_spec = pltpu.PrefetchScalarGridSpec(
        num_scalar_prefetch=1,
        grid=(nb, steps),
        in_specs=[row(D_MODEL), pl.BlockSpec((None, 6, D_MODEL), lambda b, i, pt: (b, 0, 0)),
                  row(GDN_V), row(DIFF_V), row(D_MODEL), row(D_MODEL)]
                 + [const(w.shape) for w in wts]
                 + [const((1, DH_DIFF))] * 4
                 + [const((1, HEAD_W)), const((NROW_S, CHUNK_KEYS)), const((NROW_S, SUBLANES)), const((NROW_S, LANES)),
                    per_seq, per_seq, per_seq,
                    pl.BlockSpec(memory_space=pl.ANY), pl.BlockSpec(memory_space=pl.ANY)],
        out_specs=[row(D_MODEL), per_seq],
        scratch_shapes=[pltpu.VMEM((N_SLOTS, DIFF_QK, CHUNK_KEYS), F32),
                        pltpu.VMEM((N_SLOTS, CHUNK_KEYS * H_DIFF, HEAD_W), F32),
                        pltpu.SemaphoreType.DMA((2, N_SLOTS))],
    )
    return pl.pallas_call(
        functools.partial(_post_attn_body, lam_init=lam_init, n_chunks=n_chunks, dec_seq=dec_seq),
        grid_spec=grid_spec,
        out_shape=[jax.ShapeDtypeStruct((nb, t, D_MODEL), F32), jax.ShapeDtypeStruct((n_seq, SUBLANES, DIFF_V), F32)],
        compiler_params=_cparams(("arbitrary", "arbitrary")),
        name="post_attn",
    )(page_table.reshape(-1), x, mods, og, od, gg, gd, *wts, *lam_w, diff_g, bias_last, bias_new, far, q8, kn8, vn8,
      cache_k, cache_v)


def kernel(x_prompt, x_sample, c_prompt, c_sample, cache_k, cache_v, page_table, state_conv, state_gdn, rel_bias,
           w_in, w_conv, a_log, dt_bias, gdn_norm_g, lam_q1, lam_k1, lam_q2, lam_k2, diff_norm_g, w_br_gdn,
           w_br_diff, w_o, w_ada, b_ada, ln1_g, ln1_b, ln2_g, ln2_b, w_up, b_up, w_down, b_down):
    assert DEPTH == 1 and w_in.shape[0] == 1
    l = 0
    lam_init = 0.8 - 0.6 * math.exp(-0.3 * l)
    nbp, seq, _ = x_prompt.shape
    nbs, dec_seq, _ = x_sample.shape
    n_tok_s = nbs * dec_seq
    past_len = page_table.shape[1] * PAGE_SIZE
    assert seq % Q_TILE == 0 and CONV_W - 1 <= dec_seq and 2 * dec_seq == SUBLANES
    assert page_table.shape[1] % PAGES_PER_STEP == 0 and nbs % (8 * SEQ_PER_ITER) == 0

    offs = np.concatenate([[0], np.cumsum(IN_SPLITS)])
    col = lambda i: w_in[l][:, int(offs[i]):int(offs[i + 1])]
    w_main = jnp.concatenate([col(0), col(1), col(4), col(5), col(6), col(7), col(8)], axis=1).astype(BF16)
    w_small = jnp.concatenate([col(2), col(3), jnp.zeros((D_MODEL, LANES - 2 * H_GDN), F32)], axis=1).astype(BF16)
    pad_gate = lambda a: jnp.zeros((1, LANES), F32).at[0, H_GDN:2 * H_GDN].set(a)
    alog_pad, dtb_pad = pad_gate(a_log[l]), pad_gate(dt_bias[l])
    gng = gdn_norm_g[l].reshape(1, DV_GDN)
    diff_g = diff_norm_g[l].reshape(1, HEAD_W)
    lam_w = (lam_q1[l].reshape(1, -1), lam_k1[l].reshape(1, -1), lam_q2[l].reshape(1, -1), lam_k2[l].reshape(1, -1))
    r2 = lambda a: a.reshape(1, -1)
    post_w = (w_br_gdn[l].astype(BF16), w_br_diff[l].astype(BF16), w_o[l].astype(BF16), r2(ln1_g[l]), r2(ln1_b[l]),
              w_up[l].astype(BF16), r2(b_up[l]), w_down[l].astype(BF16), r2(b_down[l]), r2(ln2_g[l]), r2(ln2_b[l]))

    ada = _ada(jnp.concatenate([c_prompt, c_sample], axis=0), w_ada[l].astype(BF16), r2(b_ada[l]))
    mods_p = ada[:nbp].reshape(nbp, 6, D_MODEL)
    mods_s = jnp.repeat(ada[nbp:].reshape(nbs, 6, D_MODEL), dec_seq, axis=0).transpose(1, 0, 2)
    bias_p, bias_last, bias_new, far = _bias_prep(rel_bias, past_len, dec_seq)

    qkv_p, z_p, q_p, k_p, v_p, gg_p, gd_p, ba_p, tail_p = _inproj(x_prompt, mods_p, w_main, w_small, 512, w_conv[l])
    og_p, s_p = _gdn_prompt(qkv_p, z_p, ba_p, alog_pad, dtb_pad, gng, 4 * CHUNK)
    od_p = _attn_prompt(lam_w, diff_g, bias_p, q_p, k_p, v_p, lam_init)

    xs = x_sample.reshape(1, n_tok_s, D_MODEL)
    conv_s, z_s, q_s, k_s, v_s, gg_s, gd_s, ba_s = _inproj(xs, mods_s, w_main, w_small, n_tok_s)
    per_seq = lambda a: a.reshape(nbs, dec_seq, a.shape[-1])
    pad8 = lambda a: jnp.pad(a.astype(F32), ((0, 0), (0, SUBLANES - dec_seq), (0, 0)))
    conv_s3 = per_seq(conv_s)
    upx = jnp.concatenate([jnp.zeros((nbs, SUBLANES - (CONV_W - 1), GDN_CONV_CH), F32), state_conv[l], conv_s3,
                           jnp.zeros((nbs, SUBLANES - dec_seq, GDN_CONV_CH), F32)], axis=1)
    og_s8, s_s = _gdn_sample(upx, pad8(per_seq(z_s)), pad8(per_seq(ba_s)), state_gdn[l], w_conv[l], alog_pad,
                             dtb_pad, gng, dec_seq, 8)
    ck = jnp.transpose(cache_k[l], (0, 2, 3, 4, 1)).reshape(cache_k.shape[1], DIFF_QK, PAGE_SIZE)
    cv = cache_v[l].reshape(cache_v.shape[1], PAGE_SIZE * H_DIFF, HEAD_W)
    q_s3 = per_seq(q_s).astype(F32)
    y_p, od_s8 = _post_attn(x_prompt, mods_p, og_p, od_p, gg_p, gd_p, post_w, seq * nbp // nbs, page_table, lam_w,
                            diff_g, bias_last, bias_new, far, jnp.concatenate([q_s3, q_s3], axis=1),
                            pad8(per_seq(k_s)), pad8(v_s.reshape(nbs, dec_seq, DIFF_V)), ck, cv, lam_init, dec_seq)
    flat = lambda a8: a8[:, :dec_seq].reshape(1, n_tok_s, a8.shape[-1]).astype(BF16)
    y_s = _post(xs, mods_s, flat(og_s8), flat(od_s8), gg_s, gd_s, post_w, 256, True)

    hd = lambda a, b: a.reshape(1, b, -1, H_DIFF, 2, DH_DIFF)
    hv = lambda a, b: a.reshape(1, b, -1, H_DIFF, 2 * DH_DIFF)
    return (y_p, y_s.reshape(nbs, dec_seq, D_MODEL),
            hd(k_p, nbp), hv(v_p, nbp), tail_p[:, SUBLANES - (CONV_W - 1):][None], s_p[None],
            hd(k_s, nbs), hv(v_s, nbs), conv_s3[:, dec_seq - (CONV_W - 1):][None], s_s[None])
```

```python
import functools
import math

import numpy as np
import jax
import jax.numpy as jnp
from jax import lax
from jax.experimental import pallas as pl
from jax.experimental.pallas import tpu as pltpu

F32 = jnp.float32
BF16 = jnp.bfloat16

D_MODEL = 1024
DEPTH = 1
PAGE_SIZE = 128
H_GDN = 4
DK_GDN = 128
DV_GDN = 128
CONV_W = 4
CHUNK = 64
GDN_NORM_EPS = 1e-6
H_DIFF = 4
DH_DIFF = 64
DIFF_NORM_EPS = 1e-5
N_BUCKETS = 32
MAX_DISTANCE = 128
D_FF = 4 * D_MODEL
LN_EPS = 1e-5
GDN_QK = H_GDN * DK_GDN
GDN_V = H_GDN * DV_GDN
GDN_CONV_CH = 2 * GDN_QK + GDN_V
DIFF_QK = H_DIFF * 2 * DH_DIFF
DIFF_V = H_DIFF * 2 * DH_DIFF
HEAD_W = 2 * DH_DIFF
IN_SPLITS = (GDN_CONV_CH, GDN_V, H_GDN, H_GDN, DIFF_QK, DIFF_QK, DIFF_V, D_MODEL, D_MODEL)
ALPHA = (2.0 * DEPTH) ** 0.25

LANES = 128
SUBLANES = 8
VMEM_LIMIT = 56 * 1024 * 1024

Q_TILE = 256
PAGES_PER_STEP = 8
N_SLOTS = 4
SEQ_PER_ITER = 4
MASKED = N_BUCKETS

NN = (((1,), (0,)), ((), ()))
NT = (((1,), (1,)), ((), ()))
TN = (((0,), (0,)), ((), ()))


def _cparams(sem):
    return pltpu.CompilerParams(dimension_semantics=sem, vmem_limit_bytes=VMEM_LIMIT)


def _const_spec(shape):
    nd = len(shape)
    return pl.BlockSpec(shape, lambda *_: (0,) * nd, pipeline_mode=pl.Buffered(1))


def _sigmoid(x):
    return 1.0 / (1.0 + jnp.exp(-x))


def _silu(x):
    return x * _sigmoid(x)


def _softplus(x):
    return jnp.maximum(x, 0.0) + jnp.log(1.0 + jnp.exp(-jnp.abs(x)))


def _ln(x):
    mu = jnp.mean(x, axis=-1, keepdims=True)
    xc = x - mu
    var = jnp.mean(xc * xc, axis=-1, keepdims=True)
    return xc * lax.rsqrt(var + LN_EPS)


def _dot(a, b, dims=NN):
    return lax.dot_general(a, b, dims, preferred_element_type=F32)


def _dot_bf(a, b, dims=NN):
    return _dot(a.astype(BF16), b.astype(BF16), dims)


def _split(a):
    hi = a.astype(BF16)
    lo = (a - hi.astype(F32)).astype(BF16)
    return hi, lo


def _dot3(a, b, dims=NN, fuse=True):
    ah, al = _split(a)
    bh, bl = _split(b)
    if not fuse:
        return _dot(ah, bh, dims) + (_dot(ah, bl, dims) + _dot(al, bh, dims))
    lhs = jnp.concatenate([ah, ah, al], axis=1)
    rhs = jnp.concatenate([bh, bl, bh], axis=1 if dims == NT else 0)
    return _dot(lhs, rhs, dims)


def _dot_f32(a, b, dims=NN):
    return lax.dot_general(a, b, dims, precision=lax.Precision.HIGHEST, preferred_element_type=F32)


def _mod(mod_ref, i, per_token):
    return mod_ref[i] if per_token else mod_ref[i:i + 1, :]


def _ada_body(c_ref, w_ref, b_ref, o_ref):
    s = _silu(c_ref[...])
    o_ref[...] = _dot(s.astype(BF16), w_ref[...]) + b_ref[...]


def _ada(c_all, w_ada, b_ada):
    n = c_all.shape[0]
    return pl.pallas_call(
        _ada_body,
        grid=(6,),
        in_specs=[pl.BlockSpec((n, D_MODEL), lambda j: (0, 0)),
                  pl.BlockSpec((D_MODEL, D_MODEL), lambda j: (0, j)),
                  pl.BlockSpec((1, D_MODEL), lambda j: (0, j))],
        out_specs=pl.BlockSpec((n, D_MODEL), lambda j: (0, j)),
        out_shape=jax.ShapeDtypeStruct((n, 6 * D_MODEL), F32),
        compiler_params=_cparams(("arbitrary",)),
        name="ada",
    )(c_all, w_ada, b_ada)


def _t5_bucket_np(rel):
    n = np.maximum(rel, 0)
    max_exact = N_BUCKETS // 2
    nf = np.maximum(n, 1).astype(np.float32)
    large = max_exact + (np.log(nf / np.float32(max_exact)) / np.float32(math.log(MAX_DISTANCE / max_exact))
                         * np.float32(N_BUCKETS - max_exact)).astype(np.int32)
    large = np.minimum(large, N_BUCKETS - 1)
    return np.where(n < max_exact, n, large).astype(np.int32)


def _bias_codes(past_len, dec_seq):
    r = np.arange(Q_TILE)[:, None]
    c = np.arange(Q_TILE)[None, :]
    prev_tile = _t5_bucket_np(Q_TILE + r - c)
    diag_tile = np.where(r >= c, _t5_bucket_np(r - c), MASKED)
    prompt = np.concatenate([prev_tile, diag_tile], axis=1).astype(np.int32)
    prompt = np.concatenate([prompt, prompt], axis=0)
    nrow = H_DIFF * SUBLANES
    qrow = (np.arange(nrow) % dec_seq)[:, None]
    nkeys = PAGES_PER_STEP * PAGE_SIZE
    kpos = past_len - nkeys + np.arange(nkeys)[None, :]
    last = _t5_bucket_np(past_len + qrow - kpos).astype(np.int32)
    j = np.arange(SUBLANES)[None, :]
    new = np.where((j <= qrow) & (j < dec_seq), _t5_bucket_np(qrow - j), MASKED).astype(np.int32)
    return prompt, last, new


def _bias_body(tbl_ref, cp_ref, cl_ref, cn_ref, bp_ref, bl_ref, bn_ref, far_ref):
    def lookup(codes, h):
        acc = jnp.full(codes.shape, -jnp.inf, F32)
        for b in range(N_BUCKETS):
            acc = jnp.where(codes == b, tbl_ref[b, h], acc)
        return acc

    cp = cp_ref[...]
    for h in range(H_DIFF):
        bp_ref[h] = lookup(cp, h) - tbl_ref[N_BUCKETS - 1, h]
    cl = cl_ref[...]
    cn = cn_ref[...]
    nrow = cl.shape[0]
    row_head = lax.broadcasted_iota(jnp.int32, (nrow, 1), 0) // SUBLANES
    bl = jnp.zeros(cl.shape, F32)
    bn = jnp.zeros(cn.shape, F32)
    far = jnp.zeros((nrow, LANES), F32)
    for h in range(H_DIFF):
        bl = jnp.where(row_head == h, lookup(cl, h), bl)
        bn = jnp.where(row_head == h, lookup(cn, h), bn)
        far = jnp.where(row_head == h, tbl_ref[N_BUCKETS - 1, h], far)
    bl_ref[...] = bl
    bn_ref[...] = bn
    far_ref[...] = far


def _bias_prep(rel_bias, past_len, dec_seq):
    cp, cl, cn = _bias_codes(past_len, dec_seq)
    nrow = cl.shape[0]
    vm = pl.BlockSpec(memory_space=pltpu.VMEM)
    return pl.pallas_call(
        _bias_body,
        in_specs=[pl.BlockSpec(memory_space=pltpu.SMEM), vm, vm, vm],
        out_specs=[vm, vm, vm, vm],
        out_shape=[jax.ShapeDtypeStruct((H_DIFF,) + cp.shape, F32),
                   jax.ShapeDtypeStruct(cl.shape, F32),
                   jax.ShapeDtypeStruct(cn.shape, F32),
                   jax.ShapeDtypeStruct((nrow, LANES), F32)],
        name="bias_prep",
    )(rel_bias, jnp.asarray(cp), jnp.asarray(cl), jnp.asarray(cn))


def _lam(lq1_ref, lk1_ref, lq2_ref, lk2_ref, lam_init):
    s1 = jnp.sum(lq1_ref[...] * lk1_ref[...], axis=-1, keepdims=True)
    s2 = jnp.sum(lq2_ref[...] * lk2_ref[...], axis=-1, keepdims=True)
    return jnp.exp(s1) - jnp.exp(s2) + lam_init


_SEG = {"conv": (0, 1536), "z": (1536, 2048), "q": (2048, 2560), "k": (2560, 3072), "v": (3072, 3584),
        "gg": (3584, 4608), "gd": (4608, 5632)}
N_MAIN = 5632


def _inproj_h(x, mod_ref, per_token):
    h = _ln(x) * (1.0 + _mod(mod_ref, 1, per_token)) + _mod(mod_ref, 0, per_token)
    return h.astype(BF16)


def _inproj_seg(hb, wm_ref, name):
    a, b = _SEG[name]
    return _dot(hb, wm_ref[:, a:b])


def _inproj_rest(hb, r0, wm_ref, ws_ref, z_ref, q_ref, k_ref, v_ref, gg_ref, gd_ref, ba_ref):
    seg = functools.partial(_inproj_seg, hb, wm_ref)
    n = hb.shape[0]
    rows = slice(r0, r0 + n)
    z_ref[rows, :] = seg("z").astype(BF16)
    q_ref[rows, :] = seg("q").astype(BF16)
    k_ref[rows, :] = seg("k")
    sv = seg("v")
    for hh in range(H_DIFF):
        v_ref[pl.ds(r0 * H_DIFF + hh, n, stride=H_DIFF), :] = sv[:, hh * HEAD_W:(hh + 1) * HEAD_W]
    gg_ref[rows, :] = seg("gg").astype(BF16)
    gd_ref[rows, :] = seg("gd").astype(BF16)
    ba_ref[rows, :] = _dot(hb, ws_ref[...])


def _inproj_body(x_ref, mod_ref, wm_ref, ws_ref, conv_ref, z_ref, q_ref, k_ref, v_ref, gg_ref, gd_ref, ba_ref):
    hb = _inproj_h(x_ref[...], mod_ref, True)
    conv_ref[...] = _inproj_seg(hb, wm_ref, "conv")
    _inproj_rest(hb, 0, wm_ref, ws_ref, z_ref, q_ref, k_ref, v_ref, gg_ref, gd_ref, ba_ref)


def _inproj_conv_body(x_ref, mod_ref, wm_ref, ws_ref, wc_ref, qkv_ref, z_ref, q_ref, k_ref, v_ref, gg_ref, gd_ref,
                      ba_ref, tail_ref, cbuf_ref, *, tm):
    @pl.when(pl.program_id(1) == 0)
    def _():
        cbuf_ref[0:SUBLANES, :] = jnp.zeros((SUBLANES, GDN_CONV_CH), F32)

    sub = tm // 2
    hbs = [_inproj_h(x_ref[s * sub:(s + 1) * sub, :], mod_ref, False) for s in range(2)]
    for s in range(2):
        r0 = s * sub
        rows = slice(r0, r0 + sub)
        u_in = _inproj_seg(hbs[s], wm_ref, "conv")
        _inproj_rest(hbs[s], r0, wm_ref, ws_ref, z_ref, q_ref, k_ref, v_ref, gg_ref, gd_ref, ba_ref)
        cbuf_ref[SUBLANES + r0:SUBLANES + r0 + sub, :] = u_in
        y = u_in * wc_ref[CONV_W - 1:CONV_W, :]
        for j in range(CONV_W - 1):
            off = SUBLANES - (CONV_W - 1) + j + r0
            y = y + cbuf_ref[off:off + sub, :] * wc_ref[j:j + 1, :]
        qkv = _silu(y)
        for h in range(H_GDN):
            lo = h * DK_GDN
            qkv_ref[rows, lo:lo + DK_GDN] = _l2n(qkv[:, lo:lo + DK_GDN]) * (DK_GDN ** -0.5)
            qkv_ref[rows, GDN_QK + lo:GDN_QK + lo + DK_GDN] = _l2n(qkv[:, GDN_QK + lo:GDN_QK + lo + DK_GDN])
        qkv_ref[rows, 2 * GDN_QK:] = qkv[:, 2 * GDN_QK:]
    last = cbuf_ref[tm:tm + SUBLANES, :]
    cbuf_ref[0:SUBLANES, :] = last
    tail_ref[...] = last


def _inproj(x, mods, w_main, w_small, tm, w_conv=None):
    nb, t, _ = x.shape
    fuse = w_conv is not None

    def out(n, dt):
        return pl.BlockSpec((None, tm, n), lambda b, i: (b, i, 0)), jax.ShapeDtypeStruct((nb, t, n), dt)

    v_out = (pl.BlockSpec((None, tm * H_DIFF, HEAD_W), lambda b, i: (b, i, 0)),
             jax.ShapeDtypeStruct((nb, t * H_DIFF, HEAD_W), F32))
    outs = [out(GDN_CONV_CH, F32), out(GDN_V, BF16), out(DIFF_QK, BF16), out(DIFF_QK, F32), v_out,
            out(D_MODEL, BF16), out(D_MODEL, BF16), out(LANES, F32)]
    in_specs = [pl.BlockSpec((None, tm, D_MODEL), lambda b, i: (b, i, 0)),
                pl.BlockSpec((None, 6, D_MODEL), lambda b, i: (b, 0, 0)) if fuse
                else pl.BlockSpec((6, tm, D_MODEL), lambda b, i: (0, i, 0)),
                _const_spec((D_MODEL, N_MAIN)), _const_spec((D_MODEL, LANES))]
    args = [x, mods, w_main, w_small]
    scratch = []
    if fuse:
        in_specs.append(_const_spec((CONV_W, GDN_CONV_CH)))
        args.append(w_conv)
        outs.append((pl.BlockSpec((None, SUBLANES, GDN_CONV_CH), lambda b, i: (b, 0, 0)),
                     jax.ShapeDtypeStruct((nb, SUBLANES, GDN_CONV_CH), F32)))
        scratch.append(pltpu.VMEM((tm + SUBLANES, GDN_CONV_CH), F32))
    return pl.pallas_call(
        functools.partial(_inproj_conv_body, tm=tm) if fuse else _inproj_body,
        grid=(nb, t // tm),
        in_specs=in_specs,
        out_specs=[o[0] for o in outs],
        out_shape=[o[1] for o in outs],
        scratch_shapes=scratch,
        compiler_params=_cparams(("arbitrary", "arbitrary")),
        name="inproj_conv" if fuse else "inproj",
    )(*args)


def _gdn_solve(chains, c):
    ii = lax.broadcasted_iota(jnp.int32, (c, c), 0)
    jj = lax.broadcasted_iota(jnp.int32, (c, c), 1)
    incl = ii >= jj
    eye = (ii == jj).astype(F32)
    fuse = c % CHUNK == 0
    gamma = [jnp.where(incl, jnp.exp(jnp.where(incl, ch[4] - ch[5], 0.0)), 0.0) for ch in chains]
    kk = [_dot3(ch[1], ch[1], NT) for ch in chains]
    a = [jnp.where(ii > jj, ch[3] * k * g, 0.0) for ch, k, g in zip(chains, kk, gamma)]
    a8 = [jnp.where((ii >> 3) == (jj >> 3), m, 0.0) for m in a]
    p = [_dot3(m, m, fuse=fuse) for m in a8]
    x = [eye - m for m in a8]
    x = [xi + _dot3(xi, pi, fuse=fuse) for xi, pi in zip(x, p)]
    p = [_dot3(pi, pi, fuse=fuse) for pi in p]
    x = [xi + _dot3(xi, pi, fuse=fuse) for xi, pi in zip(x, p)]
    sh = 3
    while (1 << sh) < c:
        lvl = ((ii >> (sh + 1)) == (jj >> (sh + 1))) & ((ii >> sh) != (jj >> sh))
        t = [_dot_bf(jnp.where(lvl, m, 0.0), xi) for m, xi in zip(a, x)]
        x = [xi - _dot_bf(xi, ti) for xi, ti in zip(x, t)]
        sh += 1
    egc = [jnp.exp(ch[4]) for ch in chains]
    uw = [_dot3(xi, jnp.concatenate([ch[2] * ch[3], ch[1] * (ch[3] * e)], axis=1), fuse=fuse)
          for xi, ch, e in zip(x, chains, egc)]
    qk = [jnp.where(incl, _dot_bf(ch[0], ch[1], NT) * g, 0.0) for ch, g in zip(chains, gamma)]
    out = []
    for ch, e, uwi, qki in zip(chains, egc, uw, qk):
        g_last = ch[4][c - 1:c, :]
        wq = jnp.concatenate([uwi[:, DV_GDN:], ch[0] * e], axis=0).astype(BF16)
        k_dec = (ch[1] * jnp.exp(g_last - ch[4])).astype(BF16)
        out.append((uwi[:, :DV_GDN], wq, qki.astype(BF16), k_dec, jnp.exp(g_last)))
    return out


def _gdn_state_step(sols, states, c):
    sb = [s.astype(BF16) for s in states]
    wq = [_dot(sol[1], b) for sol, b in zip(sols, sb)]
    vb = [(sol[0] - m[:c]).astype(BF16) for sol, m in zip(sols, wq)]
    o = [m[c:] + _dot(sol[2], v) for sol, m, v in zip(sols, wq, vb)]
    s_new = [sol[4] * s + _dot(sol[3], v, TN) for sol, s, v in zip(sols, states, vb)]
    return o, s_new


def _l2n(x):
    return x * lax.rsqrt(jnp.sum(x * x, axis=-1, keepdims=True) + 1e-6)


def _gdn_gates(ba, alog_ref, dtb_ref):
    beta = _sigmoid(ba)
    g = -jnp.exp(alog_ref[...]) * _softplus(ba + dtb_ref[...])
    return beta, g


def _gdn_out(o, z, gng):
    o = o * lax.rsqrt(jnp.mean(o * o, axis=-1, keepdims=True) + GDN_NORM_EPS) * gng
    return o * _silu(z)


def _cumsum_mats(c):
    ii = lax.broadcasted_iota(jnp.int32, (c, c), 0)
    jj = lax.broadcasted_iota(jnp.int32, (c, c), 1)
    return (ii >= jj).astype(F32)


def _gdn_prompt_body(qkv_ref, z_ref, ba_ref, alog_ref, dtb_ref, gng_ref, og_ref, s_ref, *, tt):
    @pl.when(pl.program_id(1) == 0)
    def _():
        s_ref[...] = jnp.zeros(s_ref.shape, F32)

    beta_all, g_all = _gdn_gates(ba_ref[...], alog_ref, dtb_ref)
    ltri = _cumsum_mats(CHUNK)
    nch = tt // CHUNK
    gc_all = jnp.concatenate([_dot_f32(ltri, g_all[c * CHUNK:(c + 1) * CHUNK, :]) for c in range(nch)], axis=0)
    gc_t = gc_all.T
    gng = gng_ref[...]
    chains = []
    for c in range(nch):
        rows = slice(c * CHUNK, (c + 1) * CHUNK)
        for h in range(H_GDN):
            lo = h * DK_GDN
            chains.append((qkv_ref[rows, lo:lo + DK_GDN],
                           qkv_ref[rows, GDN_QK + lo:GDN_QK + lo + DK_GDN],
                           qkv_ref[rows, 2 * GDN_QK + lo:2 * GDN_QK + lo + DV_GDN],
                           beta_all[rows, h:h + 1], gc_all[rows, H_GDN + h:H_GDN + h + 1],
                           gc_t[H_GDN + h:H_GDN + h + 1, rows]))
    sols = _gdn_solve(chains, CHUNK)
    states = [s_ref[h] for h in range(H_GDN)]
    for c in range(nch):
        rows = slice(c * CHUNK, (c + 1) * CHUNK)
        o, states = _gdn_state_step(sols[c * H_GDN:(c + 1) * H_GDN], states, CHUNK)
        for h in range(H_GDN):
            cols = slice(h * DV_GDN, (h + 1) * DV_GDN)
            og_ref[rows, cols] = _gdn_out(o[h], z_ref[rows, cols].astype(F32), gng).astype(og_ref.dtype)
    for h in range(H_GDN):
        s_ref[h] = states[h]


def _gdn_prompt(qkv_act, z, ba, alog_pad, dtb_pad, gng, tt):
    nb, t, _ = qkv_act.shape
    row = lambda n: pl.BlockSpec((None, tt, n), lambda b, i: (b, i, 0))
    return pl.pallas_call(
        functools.partial(_gdn_prompt_body, tt=tt),
        grid=(nb, t // tt),
        in_specs=[row(GDN_CONV_CH), row(GDN_V), row(LANES),
                  _const_spec((1, LANES)), _const_spec((1, LANES)), _const_spec((1, DV_GDN))],
        out_specs=[row(GDN_V), pl.BlockSpec((None, H_GDN, DK_GDN, DV_GDN), lambda b, i: (b, 0, 0, 0))],
        out_shape=[jax.ShapeDtypeStruct((nb, t, GDN_V), BF16),
                   jax.ShapeDtypeStruct((nb, H_GDN, DK_GDN, DV_GDN), F32)],
        compiler_params=_cparams(("arbitrary", "arbitrary")),
        name="gdn_prompt",
    )(qkv_act, z, ba, alog_pad, dtb_pad, gng)


def _gdn_sample_body(upx_ref, z_ref, ba_ref, s0_ref, wc_ref, alog_ref, dtb_ref, gng_ref, og_ref, s_ref,
                     *, nb, n_valid):
    c = SUBLANES
    valid = lax.broadcasted_iota(jnp.int32, (c, 1), 0) < n_valid
    ltri = _cumsum_mats(c)
    gng = gng_ref[...]

    def body(i, carry):
        chains, states = [], []
        for n in (SEQ_PER_ITER * i + k for k in range(SEQ_PER_ITER)):
            y = upx_ref[n, pl.ds(SUBLANES, c), :] * wc_ref[CONV_W - 1:CONV_W, :]
            for j in range(CONV_W - 1):
                off = SUBLANES - (CONV_W - 1) + j
                y = y + upx_ref[n, pl.ds(off, c), :] * wc_ref[j:j + 1, :]
            qkv = _silu(y)
            beta_all, g_all = _gdn_gates(ba_ref[n], alog_ref, dtb_ref)
            beta_all = jnp.where(valid, beta_all, 0.0)
            g_all = jnp.where(valid, g_all, 0.0)
            gc_all = _dot_f32(ltri, g_all)
            gc_t = jnp.concatenate([gc_all, jnp.zeros((LANES - c, LANES), F32)], axis=0).T
            for h in range(H_GDN):
                lo = h * DK_GDN
                chains.append((_l2n(qkv[:, lo:lo + DK_GDN]) * (DK_GDN ** -0.5),
                               jnp.where(valid, _l2n(qkv[:, GDN_QK + lo:GDN_QK + lo + DK_GDN]), 0.0),
                               jnp.where(valid, qkv[:, 2 * GDN_QK + lo:2 * GDN_QK + lo + DV_GDN], 0.0),
                               beta_all[:, h:h + 1], gc_all[:, H_GDN + h:H_GDN + h + 1],
                               gc_t[H_GDN + h:H_GDN + h + 1, 0:c]))
                states.append(s0_ref[n, h])
        o, states = _gdn_state_step(_gdn_solve(chains, c), states, c)
        for k in range(SEQ_PER_ITER):
            n = SEQ_PER_ITER * i + k
            for h in range(H_GDN):
                cols = slice(h * DV_GDN, (h + 1) * DV_GDN)
                s_ref[n, h] = states[k * H_GDN + h]
                og_ref[n, :, cols] = _gdn_out(o[k * H_GDN + h], z_ref[n, :, cols], gng)
        return carry

    lax.fori_loop(0, nb // SEQ_PER_ITER, body, 0)


def _gdn_sample(upx, z8, ba8, s0, w_conv, alog_pad, dtb_pad, gng, n_valid, nb):
    n = upx.shape[0]
    blk = lambda *shape: pl.BlockSpec((nb,) + shape, lambda i: (i,) + (0,) * len(shape))
    return pl.pallas_call(
        functools.partial(_gdn_sample_body, nb=nb, n_valid=n_valid),
        grid=(n // nb,),
        in_specs=[blk(2 * SUBLANES, GDN_CONV_CH), blk(SUBLANES, GDN_V), blk(SUBLANES, LANES),
                  blk(H_GDN, DK_GDN, DV_GDN),
                  _const_spec((CONV_W, GDN_CONV_CH)), _const_spec((1, LANES)), _const_spec((1, LANES)),
                  _const_spec((1, DV_GDN))],
        out_specs=[blk(SUBLANES, GDN_V), blk(H_GDN, DK_GDN, DV_GDN)],
        out_shape=[jax.ShapeDtypeStruct((n, SUBLANES, GDN_V), F32),
                   jax.ShapeDtypeStruct((n, H_GDN, DK_GDN, DV_GDN), F32)],
        compiler_params=_cparams(("arbitrary",)),
        name="gdn_sample",
    )(upx, z8, ba8, s0, w_conv, alog_pad, dtb_pad, gng)


def _diff_norm(o, g, lam_init):
    return o * lax.rsqrt(jnp.mean(o * o, axis=-1, keepdims=True) + DIFF_NORM_EPS) * g * (1.0 - lam_init)


def _attn_prompt_body(lq1_ref, lk1_ref, lq2_ref, lk2_ref, g_ref, bias_ref, q_ref, k_ref, v_ref, o_ref,
                      *, lam_init, t):
    lam = _lam(lq1_ref, lk1_ref, lq2_ref, lk2_ref, lam_init)
    kb = k_ref[...].astype(BF16)
    vb = v_ref[pl.ds(pl.program_id(1), t, stride=H_DIFF), :].astype(BF16)
    vb = jnp.concatenate([vb, jnp.ones((t, HEAD_W), BF16)], axis=1)
    g = g_ref[...]
    lane = lax.broadcasted_iota(jnp.int32, (1, HEAD_W), 1)
    zero = jnp.zeros((), BF16)
    n_tiles = t // Q_TILE

    def scores(i):
        r0 = i * Q_TILE
        qi = q_ref[r0:r0 + Q_TILE, :] * jnp.asarray(DH_DIFF ** -0.5, BF16)
        q2 = jnp.concatenate([jnp.where(lane < DH_DIFF, qi, zero), jnp.where(lane >= DH_DIFF, qi, zero)], axis=0)
        near0 = max(i - 1, 0) * Q_TILE
        sn = _dot(q2, kb[near0:r0 + Q_TILE], NT) + bias_ref[:, 2 * Q_TILE - (r0 + Q_TILE - near0):]
        return sn, (_dot(q2, kb[:near0], NT) if near0 > 0 else None)

    pending = scores(0)
    for i in range(n_tiles):
        r0 = i * Q_TILE
        near0 = max(i - 1, 0) * Q_TILE
        sn, sf = pending
        if i + 1 < n_tiles:
            pending = scores(i + 1)
        m = jnp.max(sn, axis=-1, keepdims=True)
        if sf is not None:
            m = jnp.maximum(m, jnp.max(sf, axis=-1, keepdims=True))
        o2 = _dot(jnp.exp((sn - m).astype(BF16)), vb[near0:r0 + Q_TILE])
        if sf is not None:
            o2 = o2 + _dot(jnp.exp((sf - m).astype(BF16)), vb[:near0])
        o2 = o2[:, :HEAD_W] * (1.0 / o2[:, HEAD_W:HEAD_W + 1])
        o = o2[:Q_TILE] - lam * o2[Q_TILE:]
        o_ref[r0:r0 + Q_TILE, :] = _diff_norm(o, g, lam_init).astype(o_ref.dtype)


def _attn_prompt(lam_w, diff_g, bias_p, q, k, v, lam_init):
    nb, t, _ = q.shape
    head = lambda: pl.BlockSpec((None, t, HEAD_W), lambda b, h: (b, 0, h))
    small = _const_spec((1, DH_DIFF))
    return pl.pallas_call(
        functools.partial(_attn_prompt_body, lam_init=lam_init, t=t),
        grid=(nb, H_DIFF),
        in_specs=[small, small, small, small, _const_spec((1, HEAD_W)),
                  pl.BlockSpec((None, 2 * Q_TILE, 2 * Q_TILE), lambda b, h: (h, 0, 0)),
                  head(), head(), pl.BlockSpec((None, t * H_DIFF, HEAD_W), lambda b, h: (b, 0, 0))],
        out_specs=head(),
        out_shape=jax.ShapeDtypeStruct((nb, t, DIFF_V), BF16),
        compiler_params=_cparams(("arbitrary", "arbitrary")),
        name="attn_prompt",
    )(*lam_w, diff_g, bias_p, q, k, v)


NROW_S = H_DIFF * SUBLANES
CHUNK_KEYS = PAGES_PER_STEP * PAGE_SIZE
_HEAD_ROWS = [slice(h * SUBLANES, (h + 1) * SUBLANES) for h in range(H_DIFF)]


def _page_copies(pt_ref, ck_hbm, cv_hbm, kbuf, vbuf, sem, chunk, slot):
    vrows = PAGE_SIZE * H_DIFF
    out = []
    for p in range(PAGES_PER_STEP):
        page = pt_ref[chunk * PAGES_PER_STEP + p]
        out.append(pltpu.make_async_copy(ck_hbm.at[page], kbuf.at[slot, :, pl.ds(p * PAGE_SIZE, PAGE_SIZE)],
                                         sem.at[0, slot]))
        out.append(pltpu.make_async_copy(cv_hbm.at[page], vbuf.at[slot, pl.ds(p * vrows, vrows)], sem.at[1, slot]))
    return out


def _sample_queries(q_ref, dec_seq):
    lane_hc = lax.broadcasted_iota(jnp.int32, (SUBLANES, DIFF_QK), 1) // DH_DIFF
    row_c = lax.broadcasted_iota(jnp.int32, (SUBLANES, DIFF_QK), 0) // dec_seq
    q2 = q_ref[...] * (DH_DIFF ** -0.5)
    return jnp.concatenate([jnp.where(lane_hc == 2 * h + row_c, q2, 0.0) for h in range(H_DIFF)],
                           axis=0).astype(BF16)


def _sample_attn_init(qbd, kn_ref, vn_ref, bn_ref):
    s = _dot(qbd, kn_ref[...].astype(BF16), NT) + bn_ref[...]
    m = jnp.max(s, axis=-1, keepdims=True)
    p = jnp.exp(s - m)
    pb = p.astype(BF16)
    vn = vn_ref[...].astype(BF16)
    acc = jnp.concatenate([_dot(pb[_HEAD_ROWS[h]], vn[:, h * HEAD_W:(h + 1) * HEAD_W]) for h in range(H_DIFF)],
                          axis=0)
    return m, jnp.sum(p, axis=-1, keepdims=True), acc


def _sample_attn_update(state, s, vbuf, slot):
    m_old, l_old, acc = state
    m_new = jnp.maximum(m_old, jnp.max(s, axis=-1, keepdims=True))
    alpha = jnp.exp(m_old - m_new)
    p = jnp.exp(s - m_new)
    pb = p.astype(BF16)
    pv = jnp.concatenate(
        [_dot(pb[_HEAD_ROWS[h]], vbuf[slot, pl.ds(h, CHUNK_KEYS, stride=H_DIFF), :].astype(BF16))
         for h in range(H_DIFF)], axis=0)
    return m_new, alpha * l_old + jnp.sum(p, axis=-1, keepdims=True), alpha * acc + pv


def _sample_attn_finish(state, lam, g, lam_init, dec_seq):
    _, l, acc = state
    row = lax.broadcasted_iota(jnp.int32, (NROW_S, 1), 0)
    coef = jnp.where((row % SUBLANES) < dec_seq, 1.0, -lam) / l
    ri = lax.broadcasted_iota(jnp.int32, (NROW_S, NROW_S), 0)
    rj = lax.broadcasted_iota(jnp.int32, (NROW_S, NROW_S), 1)
    sel = ((ri // SUBLANES == rj // SUBLANES) & (ri % SUBLANES == rj % dec_seq)).astype(F32)
    o = _dot_f32(sel, acc * coef)
    return jnp.concatenate([_diff_norm(o[_HEAD_ROWS[h]], g, lam_init) for h in range(H_DIFF)], axis=1)


def _post_body(x_ref, mod_ref, og_ref, od_ref, gg_ref, gd_ref, wbg_ref, wbd_ref, wo_ref, l1g_ref, l1b_ref,
               wup_ref, bup_ref, wdn_ref, bdn_ref, l2g_ref, l2b_ref, y_ref, *, per_token):
    md = lambda i: _mod(mod_ref, i, per_token)
    m = (_sigmoid(gg_ref[...].astype(F32)) * _dot(og_ref[...], wbg_ref[...])
         + _sigmoid(gd_ref[...].astype(F32)) * _dot(od_ref[...], wbd_ref[...]))
    mix = _dot(m.astype(BF16), wo_ref[...])
    x1 = _ln(ALPHA * x_ref[...] + md(2) * mix) * l1g_ref[...] + l1b_ref[...]
    h2 = _ln(x1) * (1.0 + md(4)) + md(3)
    up = _dot(h2.astype(BF16), wup_ref[...]) + bup_ref[...]
    act = jnp.square(jnp.maximum(up, 0.0))
    f = _dot(act.astype(BF16), wdn_ref[...]) + bdn_ref[...]
    y_ref[...] = _ln(ALPHA * x1 + md(5) * f) * l2g_ref[...] + l2b_ref[...]


def _post(x, mods, og, od, gg, gd, wts, tm, per_token):
    nb, t, _ = x.shape
    if per_token:
        mod_spec = pl.BlockSpec((6, tm, D_MODEL), lambda b, i: (0, i, 0))
    else:
        mod_spec = pl.BlockSpec((None, 6, D_MODEL), lambda b, i: (b, 0, 0))
    row = lambda n: pl.BlockSpec((None, tm, n), lambda b, i: (b, i, 0))
    return pl.pallas_call(
        functools.partial(_post_body, per_token=per_token),
        grid=(nb, t // tm),
        in_specs=[row(D_MODEL), mod_spec, row(GDN_V), row(DIFF_V), row(D_MODEL), row(D_MODEL)]
                 + [_const_spec(w.shape) for w in wts],
        out_specs=row(D_MODEL),
        out_shape=jax.ShapeDtypeStruct((nb, t, D_MODEL), F32),
        compiler_params=_cparams(("arbitrary", "arbitrary")),
        name="post",
    )(x, mods, og, od, gg, gd, *wts)


def _post_attn_body(pt_ref, x_ref, mod_ref, og_ref, od_ref, gg_ref, gd_ref, wbg_ref, wbd_ref, wo_ref, l1g_ref, l1b_ref,
                    wup_ref, bup_ref, wdn_ref, bdn_ref, l2g_ref, l2b_ref,
                    lq1_ref, lk1_ref, lq2_ref, lk2_ref, g_ref, bl_ref, bn_ref, far_ref, q_ref, kn_ref, vn_ref,
                    ck_hbm, cv_hbm, y_ref, o_ref, kbuf, vbuf, sem, *, lam_init, n_chunks, dec_seq):
    seq = pl.program_id(0) * pl.num_programs(1) + pl.program_id(1)
    n_seq = pl.num_programs(0) * pl.num_programs(1)
    copies = functools.partial(_page_copies, pt_ref, ck_hbm, cv_hbm, kbuf, vbuf, sem)

    @pl.when(seq == 0)
    def _():
        for ahead in range(N_SLOTS - 1):
            for cp in copies(ahead, ahead):
                cp.start()

    md = lambda i: mod_ref[i:i + 1, :]
    m = (_sigmoid(gg_ref[...].astype(F32)) * _dot(og_ref[...], wbg_ref[...])
         + _sigmoid(gd_ref[...].astype(F32)) * _dot(od_ref[...], wbd_ref[...]))
    mix = _dot(m.astype(BF16), wo_ref[...])
    x1 = _ln(ALPHA * x_ref[...] + md(2) * mix) * l1g_ref[...] + l1b_ref[...]
    h2b = (_ln(x1) * (1.0 + md(4)) + md(3)).astype(BF16)

    qbd = _sample_queries(q_ref, dec_seq)
    state = _sample_attn_init(qbd, kn_ref, vn_ref, bn_ref)
    ff = D_FF // n_chunks
    f = None
    for j in range(n_chunks):
        slot = j % N_SLOTS
        nxt = j + N_SLOTS - 1
        if nxt < n_chunks:
            for cp in copies(seq * n_chunks + nxt, nxt % N_SLOTS):
                cp.start()
        else:
            @pl.when(seq + 1 < n_seq)
            def _():
                for cp in copies((seq + 1) * n_chunks + nxt - n_chunks, nxt % N_SLOTS):
                    cp.start()
        for cp in copies(seq * n_chunks + j, slot):
            cp.wait()
        s = _dot(qbd, kbuf[slot].astype(BF16)) + (bl_ref[...] if j == n_chunks - 1 else far_ref[:, 0:1])
        cols = slice(j * ff, (j + 1) * ff)
        up = _dot(h2b, wup_ref[:, cols]) + bup_ref[:, cols]
        part = _dot(jnp.square(jnp.maximum(up, 0.0)).astype(BF16), wdn_ref[cols, :])
        f = part if f is None else f + part
        state = _sample_attn_update(state, s, vbuf, slot)

    y_ref[...] = _ln(ALPHA * x1 + md(5) * (f + bdn_ref[...])) * l2g_ref[...] + l2b_ref[...]
    lam = _lam(lq1_ref, lk1_ref, lq2_ref, lk2_ref, lam_init)
    o_ref[...] = _sample_attn_finish(state, lam, g_ref[...], lam_init, dec_seq)


def _post_attn(x, mods, og, od, gg, gd, wts, tm, page_table, lam_w, diff_g, bias_last, bias_new, far, q8, kn8, vn8,
               cache_k, cache_v, lam_init, dec_seq):
    nb, t, _ = x.shape
    n_seq, n_pages = page_table.shape
    n_chunks = n_pages // PAGES_PER_STEP
    steps = t // tm
    assert nb * steps == n_seq and n_chunks % N_SLOTS == 0 and D_FF % n_chunks == 0 and n_seq * n_chunks >= N_SLOTS
    row = lambda n: pl.BlockSpec((None, tm, n), lambda b, i, pt: (b, i, 0))
    const = lambda shape: pl.BlockSpec(shape, lambda b, i, pt: (0,) * len(shape), pipeline_mode=pl.Buffered(1))
    per_seq = pl.BlockSpec((None, SUBLANES, DIFF_QK), lambda b, i, pt: (b * steps + i, 0, 0))
    grid_spec = pltpu.PrefetchScalarGridSpec(
        num_scalar_prefetch=1,
        grid=(nb, steps),
        in_specs=[row(D_MODEL), pl.BlockSpec((None, 6, D_MODEL), lambda b, i, pt: (b, 0, 0)),
                  row(GDN_V), row(DIFF_V), row(D_MODEL), row(D_MODEL)]
                 + [const(w.shape) for w in wts]
                 + [const((1, DH_DIFF))] * 4
                 + [const((1, HEAD_W)), const((NROW_S, CHUNK_KEYS)), const((NROW_S, SUBLANES)), const((NROW_S, LANES)),
                    per_seq, per_seq, per_seq,
                    pl.BlockSpec(memory_space=pl.ANY), pl.BlockSpec(memory_space=pl.ANY)],
        out_specs=[row(D_MODEL), per_seq],
        scratch_shapes=[pltpu.VMEM((N_SLOTS, DIFF_QK, CHUNK_KEYS), F32),
                        pltpu.VMEM((N_SLOTS, CHUNK_KEYS * H_DIFF, HEAD_W), F32),
                        pltpu.SemaphoreType.DMA((2, N_SLOTS))],
    )
    return pl.pallas_call(
        functools.partial(_post_attn_body, lam_init=lam_init, n_chunks=n_chunks, dec_seq=dec_seq),
        grid_spec=grid_spec,
        out_shape=[jax.ShapeDtypeStruct((nb, t, D_MODEL), F32), jax.ShapeDtypeStruct((n_seq, SUBLANES, DIFF_V), F32)],
        compiler_params=_cparams(("arbitrary", "arbitrary")),
        name="post_attn",
    )(page_table.reshape(-1), x, mods, og, od, gg, gd, *wts, *lam_w, diff_g, bias_last, bias_new, far, q8, kn8, vn8,
      cache_k, cache_v)


def kernel(x_prompt, x_sample, c_prompt, c_sample, cache_k, cache_v, page_table, state_conv, state_gdn, rel_bias,
           w_in, w_conv, a_log, dt_bias, gdn_norm_g, lam_q1, lam_k1, lam_q2, lam_k2, diff_norm_g, w_br_gdn,
           w_br_diff, w_o, w_ada, b_ada, ln1_g, ln1_b, ln2_g, ln2_b, w_up, b_up, w_down, b_down):
    assert DEPTH == 1 and w_in.shape[0] == 1
    l = 0
    lam_init = 0.8 - 0.6 * math.exp(-0.3 * l)
    nbp, seq, _ = x_prompt.shape
    nbs, dec_seq, _ = x_sample.shape
    n_tok_s = nbs * dec_seq
    past_len = page_table.shape[1] * PAGE_SIZE
    assert seq % Q_TILE == 0 and CONV_W - 1 <= dec_seq and 2 * dec_seq == SUBLANES
    assert page_table.shape[1] % PAGES_PER_STEP == 0 and nbs % (8 * SEQ_PER_ITER) == 0

    offs = np.concatenate([[0], np.cumsum(IN_SPLITS)])
    col = lambda i: w_in[l][:, int(offs[i]):int(offs[i + 1])]
    w_main = jnp.concatenate([col(0), col(1), col(4), col(5), col(6), col(7), col(8)], axis=1).astype(BF16)
    w_small = jnp.concatenate([col(2), col(3), jnp.zeros((D_MODEL, LANES - 2 * H_GDN), F32)], axis=1).astype(BF16)
    pad_gate = lambda a: jnp.zeros((1, LANES), F32).at[0, H_GDN:2 * H_GDN].set(a)
    alog_pad, dtb_pad = pad_gate(a_log[l]), pad_gate(dt_bias[l])
    gng = gdn_norm_g[l].reshape(1, DV_GDN)
    diff_g = diff_norm_g[l].reshape(1, HEAD_W)
    lam_w = (lam_q1[l].reshape(1, -1), lam_k1[l].reshape(1, -1), lam_q2[l].reshape(1, -1), lam_k2[l].reshape(1, -1))
    r2 = lambda a: a.reshape(1, -1)
    post_w = (w_br_gdn[l].astype(BF16), w_br_diff[l].astype(BF16), w_o[l].astype(BF16), r2(ln1_g[l]), r2(ln1_b[l]),
              w_up[l].astype(BF16), r2(b_up[l]), w_down[l].astype(BF16), r2(b_down[l]), r2(ln2_g[l]), r2(ln2_b[l]))

    ada = _ada(jnp.concatenate([c_prompt, c_sample], axis=0), w_ada[l].astype(BF16), r2(b_ada[l]))
    mods_p = ada[:nbp].reshape(nbp, 6, D_MODEL)
    mods_s = jnp.repeat(ada[nbp:].reshape(nbs, 6, D_MODEL), dec_seq, axis=0).transpose(1, 0, 2)
    bias_p, bias_last, bias_new, far = _bias_prep(rel_bias, past_len, dec_seq)

    qkv_p, z_p, q_p, k_p, v_p, gg_p, gd_p, ba_p, tail_p = _inproj(x_prompt, mods_p, w_main, w_small, 512, w_conv[l])
    og_p, s_p = _gdn_prompt(qkv_p, z_p, ba_p, alog_pad, dtb_pad, gng, 8 * CHUNK)
    od_p = _attn_prompt(lam_w, diff_g, bias_p, q_p, k_p, v_p, lam_init)

    xs = x_sample.reshape(1, n_tok_s, D_MODEL)
    conv_s, z_s, q_s, k_s, v_s, gg_s, gd_s, ba_s = _inproj(xs, mods_s, w_main, w_small, n_tok_s)
    per_seq = lambda a: a.reshape(nbs, dec_seq, a.shape[-1])
    pad8 = lambda a: jnp.pad(a.astype(F32), ((0, 0), (0, SUBLANES - dec_seq), (0, 0)))
    conv_s3 = per_seq(conv_s)
    upx = jnp.concatenate([jnp.zeros((nbs, SUBLANES - (CONV_W - 1), GDN_CONV_CH), F32), state_conv[l], conv_s3,
                           jnp.zeros((nbs, SUBLANES - dec_seq, GDN_CONV_CH), F32)], axis=1)
    og_s8, s_s = _gdn_sample(upx, pad8(per_seq(z_s)), pad8(per_seq(ba_s)), state_gdn[l], w_conv[l], alog_pad,
                             dtb_pad, gng, dec_seq, 8)
    ck = jnp.transpose(cache_k[l], (0, 2, 3, 4, 1)).reshape(cache_k.shape[1], DIFF_QK, PAGE_SIZE)
    cv = cache_v[l].reshape(cache_v.shape[1], PAGE_SIZE * H_DIFF, HEAD_W)
    q_s3 = per_seq(q_s).astype(F32)
    y_p, od_s8 = _post_attn(x_prompt, mods_p, og_p, od_p, gg_p, gd_p, post_w, seq * nbp // nbs, page_table, lam_w,
                            diff_g, bias_last, bias_new, far, jnp.concatenate([q_s3, q_s3], axis=1),
                            pad8(per_seq(k_s)), pad8(v_s.reshape(nbs, dec_seq, DIFF_V)), ck, cv, lam_init, dec_seq)
    flat = lambda a8: a8[:, :dec_seq].reshape(1, n_tok_s, a8.shape[-1]).astype(BF16)
    y_s = _post(xs, mods_s, flat(og_s8), flat(od_s8), gg_s, gd_s, post_w, 256, True)

    hd = lambda a, b: a.reshape(1, b, -1, H_DIFF, 2, DH_DIFF)
    hv = lambda a, b: a.reshape(1, b, -1, H_DIFF, 2 * DH_DIFF)
    return (y_p, y_s.reshape(nbs, dec_seq, D_MODEL),
            hd(k_p, nbp), hv(v_p, nbp), tail_p[:, SUBLANES - (CONV_W - 1):][None], s_p[None],
            hd(k_s, nbs), hv(v_s, nbs), conv_s3[:, dec_seq - (CONV_W - 1):][None], s_s[None])
```

```python
import functools
import math

import numpy as np
import jax
import jax.numpy as jnp
from jax import lax
from jax.experimental import pallas as pl
from jax.experimental.pallas import tpu as pltpu

F32 = jnp.float32
BF16 = jnp.bfloat16

D_MODEL = 1024
DEPTH = 1
PAGE_SIZE = 128
H_GDN = 4
DK_GDN = 128
DV_GDN = 128
CONV_W = 4
CHUNK = 64
GDN_NORM_EPS = 1e-6
H_DIFF = 4
DH_DIFF = 64
DIFF_NORM_EPS = 1e-5
N_BUCKETS = 32
MAX_DISTANCE = 128
D_FF = 4 * D_MODEL
LN_EPS = 1e-5
GDN_QK = H_GDN * DK_GDN
GDN_V = H_GDN * DV_GDN
GDN_CONV_CH = 2 * GDN_QK + GDN_V
DIFF_QK = H_DIFF * 2 * DH_DIFF
DIFF_V = H_DIFF * 2 * DH_DIFF
HEAD_W = 2 * DH_DIFF
IN_SPLITS = (GDN_CONV_CH, GDN_V, H_GDN, H_GDN, DIFF_QK, DIFF_QK, DIFF_V, D_MODEL, D_MODEL)
ALPHA = (2.0 * DEPTH) ** 0.25

LANES = 128
SUBLANES = 8
VMEM_LIMIT = 56 * 1024 * 1024
Q_TILE = 256
PAGES_PER_STEP = 8
N_SLOTS = 4
SEQ_PER_ITER = 4
MASKED = N_BUCKETS

NN = (((1,), (0,)), ((), ()))
NT = (((1,), (1,)), ((), ()))
TN = (((0,), (0,)), ((), ()))


def _cparams(sem):
    return pltpu.CompilerParams(dimension_semantics=sem, vmem_limit_bytes=VMEM_LIMIT)


def _const_spec(shape):
    nd = len(shape)
    return pl.BlockSpec(shape, lambda *_: (0,) * nd, pipeline_mode=pl.Buffered(1))


def _sigmoid(x):
    return 1.0 / (1.0 + jnp.exp(-x))


def _silu(x):
    return x * _sigmoid(x)


def _softplus(x):
    return jnp.maximum(x, 0.0) + jnp.log(1.0 + jnp.exp(-jnp.abs(x)))


def _ln(x):
    mu = jnp.mean(x, axis=-1, keepdims=True)
    xc = x - mu
    var = jnp.mean(xc * xc, axis=-1, keepdims=True)
    return xc * lax.rsqrt(var + LN_EPS)


def _dot(a, b, dims=NN):
    return lax.dot_general(a, b, dims, preferred_element_type=F32)


def _dot_bf(a, b, dims=NN):
    return _dot(a.astype(BF16), b.astype(BF16), dims)


def _split(a):
    hi = a.astype(BF16)
    lo = (a - hi.astype(F32)).astype(BF16)
    return hi, lo


def _dot3(a, b, dims=NN, fuse=True):
    ah, al = _split(a)
    bh, bl = _split(b)
    if not fuse:
        return _dot(ah, bh, dims) + (_dot(ah, bl, dims) + _dot(al, bh, dims))
    lhs = jnp.concatenate([ah, ah, al], axis=1)
    rhs = jnp.concatenate([bh, bl, bh], axis=1 if dims == NT else 0)
    return _dot(lhs, rhs, dims)


def _dot_f32(a, b, dims=NN):
    return lax.dot_general(a, b, dims, precision=lax.Precision.HIGHEST, preferred_element_type=F32)


def _mod(mod_ref, i, per_token):
    return mod_ref[:, i * D_MODEL:(i + 1) * D_MODEL] if per_token else mod_ref[i:i + 1, :]


def _ada_body(c_ref, w_ref, b_ref, o_ref):
    s = _silu(c_ref[...])
    o_ref[...] = _dot(s.astype(BF16), w_ref[...]) + b_ref[...]


def _ada(c_all, w_ada, b_ada):
    n = c_all.shape[0]
    return pl.pallas_call(
        _ada_body,
        grid=(6,),
        in_specs=[pl.BlockSpec((n, D_MODEL), lambda j: (0, 0)),
                  pl.BlockSpec((D_MODEL, D_MODEL), lambda j: (0, j)),
                  pl.BlockSpec((1, D_MODEL), lambda j: (0, j))],
        out_specs=pl.BlockSpec((n, D_MODEL), lambda j: (0, j)),
        out_shape=jax.ShapeDtypeStruct((n, 6 * D_MODEL), F32),
        compiler_params=_cparams(("arbitrary",)),
        name="ada",
    )(c_all, w_ada, b_ada)


def _t5_bucket_np(rel):
    n = np.maximum(rel, 0)
    max_exact = N_BUCKETS // 2
    nf = np.maximum(n, 1).astype(np.float32)
    large = max_exact + (np.log(nf / np.float32(max_exact)) / np.float32(math.log(MAX_DISTANCE / max_exact))
                         * np.float32(N_BUCKETS - max_exact)).astype(np.int32)
    large = np.minimum(large, N_BUCKETS - 1)
    return np.where(n < max_exact, n, large).astype(np.int32)


def _bias_codes(past_len, dec_seq):
    r = np.arange(Q_TILE)[:, None]
    c = np.arange(Q_TILE)[None, :]
    prev_tile = _t5_bucket_np(Q_TILE + r - c)
    diag_tile = np.where(r >= c, _t5_bucket_np(r - c), MASKED)
    prompt = np.concatenate([prev_tile, diag_tile], axis=1).astype(np.int32)
    prompt = np.concatenate([prompt, prompt], axis=0)
    nrow = H_DIFF * SUBLANES
    qrow = (np.arange(nrow) % dec_seq)[:, None]
    nkeys = PAGES_PER_STEP * PAGE_SIZE
    kpos = past_len - nkeys + np.arange(nkeys)[None, :]
    last = _t5_bucket_np(past_len + qrow - kpos).astype(np.int32)
    j = np.arange(SUBLANES)[None, :]
    new = np.where((j <= qrow) & (j < dec_seq), _t5_bucket_np(qrow - j), MASKED).astype(np.int32)
    return prompt, last, new


def _bias_body(tbl_ref, cp_ref, cl_ref, cn_ref, bp_ref, bl_ref, bn_ref, far_ref):
    def lookup(codes, h):
        acc = jnp.full(codes.shape, -jnp.inf, F32)
        for b in range(N_BUCKETS):
            acc = jnp.where(codes == b, tbl_ref[b, h], acc)
        return acc

    cp = cp_ref[...]
    for h in range(H_DIFF):
        bp_ref[h] = lookup(cp, h) - tbl_ref[N_BUCKETS - 1, h]
    cl = cl_ref[...]
    cn = cn_ref[...]
    nrow = cl.shape[0]
    row_head = lax.broadcasted_iota(jnp.int32, (nrow, 1), 0) // SUBLANES
    bl = jnp.zeros(cl.shape, F32)
    bn = jnp.zeros(cn.shape, F32)
    far = jnp.zeros((nrow, LANES), F32)
    for h in range(H_DIFF):
        bl = jnp.where(row_head == h, lookup(cl, h), bl)
        bn = jnp.where(row_head == h, lookup(cn, h), bn)
        far = jnp.where(row_head == h, tbl_ref[N_BUCKETS - 1, h], far)
    bl_ref[...] = bl
    bn_ref[...] = bn
    far_ref[...] = far


def _bias_prep(rel_bias, past_len, dec_seq):
    cp, cl, cn = _bias_codes(past_len, dec_seq)
    nrow = cl.shape[0]
    vm = pl.BlockSpec(memory_space=pltpu.VMEM)
    return pl.pallas_call(
        _bias_body,
        in_specs=[pl.BlockSpec(memory_space=pltpu.SMEM), vm, vm, vm],
        out_specs=[vm, vm, vm, vm],
        out_shape=[jax.ShapeDtypeStruct((H_DIFF,) + cp.shape, F32),
                   jax.ShapeDtypeStruct(cl.shape, F32),
                   jax.ShapeDtypeStruct(cn.shape, F32),
                   jax.ShapeDtypeStruct((nrow, LANES), F32)],
        name="bias_prep",
    )(rel_bias, jnp.asarray(cp), jnp.asarray(cl), jnp.asarray(cn))


def _lam(lq1_ref, lk1_ref, lq2_ref, lk2_ref, lam_init):
    s1 = jnp.sum(lq1_ref[...] * lk1_ref[...], axis=-1, keepdims=True)
    s2 = jnp.sum(lq2_ref[...] * lk2_ref[...], axis=-1, keepdims=True)
    return jnp.exp(s1) - jnp.exp(s2) + lam_init


_SEG = {"conv": (0, 1536), "z": (1536, 2048), "q": (2048, 2560), "k": (2560, 3072), "v": (3072, 3584),
        "gg": (3584, 4608), "gd": (4608, 5632)}
N_MAIN = 5632


def _inproj_h(x, mod_ref, per_token):
    h = _ln(x) * (1.0 + _mod(mod_ref, 1, per_token)) + _mod(mod_ref, 0, per_token)
    return h.astype(BF16)


def _inproj_seg(hb, wm_ref, name):
    a, b = _SEG[name]
    return _dot(hb, wm_ref[:, a:b])


def _inproj_rest(hb, r0, wm_ref, ws_ref, z_ref, q_ref, k_ref, v_ref, gg_ref, gd_ref, ba_ref):
    seg = functools.partial(_inproj_seg, hb, wm_ref)
    n = hb.shape[0]
    rows = slice(r0, r0 + n)
    z_ref[rows, :] = seg("z").astype(BF16)
    q_ref[rows, :] = seg("q").astype(BF16)
    k_ref[rows, :] = seg("k")
    sv = seg("v")
    for hh in range(H_DIFF):
        v_ref[pl.ds(r0 * H_DIFF + hh, n, stride=H_DIFF), :] = sv[:, hh * HEAD_W:(hh + 1) * HEAD_W]
    gg_ref[rows, :] = seg("gg").astype(BF16)
    gd_ref[rows, :] = seg("gd").astype(BF16)
    ba_ref[rows, :] = _dot(hb, ws_ref[...])


def _inproj_body(x_ref, mod_ref, wm_ref, ws_ref, conv_ref, z_ref, q_ref, k_ref, v_ref, gg_ref, gd_ref, ba_ref):
    hb = _inproj_h(x_ref[...], mod_ref, True)
    conv_ref[...] = _inproj_seg(hb, wm_ref, "conv")
    _inproj_rest(hb, 0, wm_ref, ws_ref, z_ref, q_ref, k_ref, v_ref, gg_ref, gd_ref, ba_ref)


def _inproj_conv_body(x_ref, mod_ref, wm_ref, ws_ref, wc_ref, qkv_ref, z_ref, q_ref, k_ref, v_ref, gg_ref, gd_ref,
                      ba_ref, tail_ref, cbuf_ref, *, tm):
    @pl.when(pl.program_id(1) == 0)
    def _():
        cbuf_ref[0:SUBLANES, :] = jnp.zeros((SUBLANES, GDN_CONV_CH), F32)

    sub = tm // 2
    hbs = [_inproj_h(x_ref[s * sub:(s + 1) * sub, :], mod_ref, False) for s in range(2)]
    for s in range(2):
        r0 = s * sub
        rows = slice(r0, r0 + sub)
        u_in = _inproj_seg(hbs[s], wm_ref, "conv")
        _inproj_rest(hbs[s], r0, wm_ref, ws_ref, z_ref, q_ref, k_ref, v_ref, gg_ref, gd_ref, ba_ref)
        cbuf_ref[SUBLANES + r0:SUBLANES + r0 + sub, :] = u_in
        y = u_in * wc_ref[CONV_W - 1:CONV_W, :]
        for j in range(CONV_W - 1):
            off = SUBLANES - (CONV_W - 1) + j + r0
            y = y + cbuf_ref[off:off + sub, :] * wc_ref[j:j + 1, :]
        qkv = _silu(y)
        for h in range(H_GDN):
            lo = h * DK_GDN
            qkv_ref[rows, lo:lo + DK_GDN] = _l2n(qkv[:, lo:lo + DK_GDN]) * (DK_GDN ** -0.5)
            qkv_ref[rows, GDN_QK + lo:GDN_QK + lo + DK_GDN] = _l2n(qkv[:, GDN_QK + lo:GDN_QK + lo + DK_GDN])
        qkv_ref[rows, 2 * GDN_QK:] = qkv[:, 2 * GDN_QK:]
    last = cbuf_ref[tm:tm + SUBLANES, :]
    cbuf_ref[0:SUBLANES, :] = last
    tail_ref[...] = last


def _inproj(x, mods, w_main, w_small, tm, w_conv=None):
    nb, t, _ = x.shape
    fuse = w_conv is not None

    def out(n, dt):
        return pl.BlockSpec((None, tm, n), lambda b, i: (b, i, 0)), jax.ShapeDtypeStruct((nb, t, n), dt)

    v_out = (pl.BlockSpec((None, tm * H_DIFF, HEAD_W), lambda b, i: (b, i, 0)),
             jax.ShapeDtypeStruct((nb, t * H_DIFF, HEAD_W), F32))
    outs = [out(GDN_CONV_CH, F32), out(GDN_V, BF16), out(DIFF_QK, BF16), out(DIFF_QK, F32), v_out,
            out(D_MODEL, BF16), out(D_MODEL, BF16), out(LANES, F32)]
    in_specs = [pl.BlockSpec((None, tm, D_MODEL), lambda b, i: (b, i, 0)),
                pl.BlockSpec((None, 6, D_MODEL), lambda b, i: (b, 0, 0)) if fuse
                else pl.BlockSpec((tm, 6 * D_MODEL), lambda b, i: (i, 0)),
                _const_spec((D_MODEL, N_MAIN)), _const_spec((D_MODEL, LANES))]
    args = [x, mods, w_main, w_small]
    scratch = []
    if fuse:
        in_specs.append(_const_spec((CONV_W, GDN_CONV_CH)))
        args.append(w_conv)
        outs.append((pl.BlockSpec((None, SUBLANES, GDN_CONV_CH), lambda b, i: (b, 0, 0)),
                     jax.ShapeDtypeStruct((nb, SUBLANES, GDN_CONV_CH), F32)))
        scratch.append(pltpu.VMEM((tm + SUBLANES, GDN_CONV_CH), F32))
    return pl.pallas_call(
        functools.partial(_inproj_conv_body, tm=tm) if fuse else _inproj_body,
        grid=(nb, t // tm),
        in_specs=in_specs,
        out_specs=[o[0] for o in outs],
        out_shape=[o[1] for o in outs],
        scratch_shapes=scratch,
        compiler_params=_cparams(("arbitrary", "arbitrary")),
        name="inproj_conv" if fuse else "inproj",
    )(*args)


def _gdn_solve(chains, c):
    ii = lax.broadcasted_iota(jnp.int32, (c, c), 0)
    jj = lax.broadcasted_iota(jnp.int32, (c, c), 1)
    incl = ii >= jj
    eye = (ii == jj).astype(F32)
    fuse = c % CHUNK == 0
    gamma = [jnp.where(incl, jnp.exp(jnp.where(incl, ch[4] - ch[5], 0.0)), 0.0) for ch in chains]
    kq = [_dot_bf(jnp.concatenate([ch[1], ch[0]], axis=0), ch[1], NT) for ch in chains]
    a = [jnp.where(ii > jj, ch[3] * k[:c] * g, 0.0) for ch, k, g in zip(chains, kq, gamma)]
    a8 = [jnp.where((ii >> 3) == (jj >> 3), m, 0.0) for m in a]
    p = [_dot3(m, m, fuse=fuse) for m in a8]
    x = [eye - m for m in a8]
    x = [xi + _dot3(xi, pi, fuse=fuse) for xi, pi in zip(x, p)]
    p = [_dot3(pi, pi, fuse=fuse) for pi in p]
    x = [xi + _dot3(xi, pi, fuse=fuse) for xi, pi in zip(x, p)]
    sh = 3
    while (1 << sh) < c:
        lvl = ((ii >> (sh + 1)) == (jj >> (sh + 1))) & ((ii >> sh) != (jj >> sh))
        t = [_dot_bf(jnp.where(lvl, m, 0.0), xi) for m, xi in zip(a, x)]
        x = [xi - _dot_bf(xi, ti) for xi, ti in zip(x, t)]
        sh += 1
    egc = [jnp.exp(ch[4]) for ch in chains]
    uw = [_dot3(xi, jnp.concatenate([ch[2] * ch[3], ch[1] * (ch[3] * e)], axis=1), fuse=fuse)
          for xi, ch, e in zip(x, chains, egc)]
    qk = [jnp.where(incl, k[c:] * g, 0.0) for k, g in zip(kq, gamma)]
    out = []
    for ch, e, uwi, qki in zip(chains, egc, uw, qk):
        g_last = ch[4][c - 1:c, :]
        wq = jnp.concatenate([uwi[:, DV_GDN:], ch[0] * e], axis=0).astype(BF16)
        k_dec = (ch[1] * jnp.exp(g_last - ch[4])).astype(BF16)
        out.append((uwi[:, :DV_GDN], wq, qki.astype(BF16), k_dec, jnp.exp(g_last)))
    return out


def _gdn_state_step(sols, states, c):
    sb = [s.astype(BF16) for s in states]
    wq = [_dot(sol[1], b) for sol, b in zip(sols, sb)]
    vb = [(sol[0] - m[:c]).astype(BF16) for sol, m in zip(sols, wq)]
    o = [m[c:] + _dot(sol[2], v) for sol, m, v in zip(sols, wq, vb)]
    s_new = [sol[4] * s + _dot(sol[3], v, TN) for sol, s, v in zip(sols, states, vb)]
    return o, s_new


def _l2n(x):
    return x * lax.rsqrt(jnp.sum(x * x, axis=-1, keepdims=True) + 1e-6)


def _gdn_gates(ba, alog_ref, dtb_ref):
    beta = _sigmoid(ba)
    g = -jnp.exp(alog_ref[...]) * _softplus(ba + dtb_ref[...])
    return beta, g


def _gdn_out(o, z, gng):
    o = o * lax.rsqrt(jnp.mean(o * o, axis=-1, keepdims=True) + GDN_NORM_EPS) * gng
    return o * _silu(z)


def _cumsum_mats(c):
    ii = lax.broadcasted_iota(jnp.int32, (c, c), 0)
    jj = lax.broadcasted_iota(jnp.int32, (c, c), 1)
    return (ii >= jj).astype(F32)


def _gdn_prompt_body(qkv_ref, z_ref, ba_ref, alog_ref, dtb_ref, gng_ref, og_ref, s_ref, *, tt):
    @pl.when(pl.program_id(1) == 0)
    def _():
        s_ref[...] = jnp.zeros(s_ref.shape, F32)

    beta_all, g_all = _gdn_gates(ba_ref[...], alog_ref, dtb_ref)
    ltri = _cumsum_mats(CHUNK)
    nch = tt // CHUNK
    gc_all = jnp.concatenate([_dot_f32(ltri, g_all[c * CHUNK:(c + 1) * CHUNK, :]) for c in range(nch)], axis=0)
    gc_t = gc_all.T
    gng = gng_ref[...]
    chains = []
    for c in range(nch):
        rows = slice(c * CHUNK, (c + 1) * CHUNK)
        for h in range(H_GDN):
            lo = h * DK_GDN
            chains.append((qkv_ref[rows, lo:lo + DK_GDN],
                           qkv_ref[rows, GDN_QK + lo:GDN_QK + lo + DK_GDN],
                           qkv_ref[rows, 2 * GDN_QK + lo:2 * GDN_QK + lo + DV_GDN],
                           beta_all[rows, h:h + 1], gc_all[rows, H_GDN + h:H_GDN + h + 1],
                           gc_t[H_GDN + h:H_GDN + h + 1, rows]))
    sols = _gdn_solve(chains, CHUNK)
    states = [s_ref[h] for h in range(H_GDN)]
    for c in range(nch):
        rows = slice(c * CHUNK, (c + 1) * CHUNK)
        o, states = _gdn_state_step(sols[c * H_GDN:(c + 1) * H_GDN], states, CHUNK)
        for h in range(H_GDN):
            cols = slice(h * DV_GDN, (h + 1) * DV_GDN)
            og_ref[rows, cols] = _gdn_out(o[h], z_ref[rows, cols].astype(F32), gng).astype(og_ref.dtype)
    for h in range(H_GDN):
        s_ref[h] = states[h]


def _gdn_prompt(qkv_act, z, ba, alog_pad, dtb_pad, gng, tt):
    nb, t, _ = qkv_act.shape
    row = lambda n: pl.BlockSpec((None, tt, n), lambda b, i: (b, i, 0))
    return pl.pallas_call(
        functools.partial(_gdn_prompt_body, tt=tt),
        grid=(nb, t // tt),
        in_specs=[row(GDN_CONV_CH), row(GDN_V), row(LANES),
                  _const_spec((1, LANES)), _const_spec((1, LANES)), _const_spec((1, DV_GDN))],
        out_specs=[row(GDN_V), pl.BlockSpec((None, H_GDN, DK_GDN, DV_GDN), lambda b, i: (b, 0, 0, 0))],
        out_shape=[jax.ShapeDtypeStruct((nb, t, GDN_V), BF16),
                   jax.ShapeDtypeStruct((nb, H_GDN, DK_GDN, DV_GDN), F32)],
        compiler_params=_cparams(("arbitrary", "arbitrary")),
        name="gdn_prompt",
    )(qkv_act, z, ba, alog_pad, dtb_pad, gng)


def _gdn_sample_body(upx_ref, z_ref, ba_ref, s0_ref, wc_ref, alog_ref, dtb_ref, gng_ref, og_ref, s_ref,
                     *, nb, n_valid):
    c = SUBLANES
    valid = lax.broadcasted_iota(jnp.int32, (c, 1), 0) < n_valid
    ltri = _cumsum_mats(c)
    gng = gng_ref[...]

    def body(i, carry):
        chains, states = [], []
        for n in (SEQ_PER_ITER * i + k for k in range(SEQ_PER_ITER)):
            y = upx_ref[n, pl.ds(SUBLANES, c), :] * wc_ref[CONV_W - 1:CONV_W, :]
            for j in range(CONV_W - 1):
                off = SUBLANES - (CONV_W - 1) + j
                y = y + upx_ref[n, pl.ds(off, c), :] * wc_ref[j:j + 1, :]
            qkv = _silu(y)
            beta_all, g_all = _gdn_gates(ba_ref[n], alog_ref, dtb_ref)
            beta_all = jnp.where(valid, beta_all, 0.0)
            g_all = jnp.where(valid, g_all, 0.0)
            gc_all = _dot_f32(ltri, g_all)
            gc_t = jnp.concatenate([gc_all, jnp.zeros((LANES - c, LANES), F32)], axis=0).T
            for h in range(H_GDN):
                lo = h * DK_GDN
                chains.append((_l2n(qkv[:, lo:lo + DK_GDN]) * (DK_GDN ** -0.5),
                               jnp.where(valid, _l2n(qkv[:, GDN_QK + lo:GDN_QK + lo + DK_GDN]), 0.0),
                               jnp.where(valid, qkv[:, 2 * GDN_QK + lo:2 * GDN_QK + lo + DV_GDN], 0.0),
                               beta_all[:, h:h + 1], gc_all[:, H_GDN + h:H_GDN + h + 1],
                               gc_t[H_GDN + h:H_GDN + h + 1, 0:c]))
                states.append(s0_ref[n, h])
        o, states = _gdn_state_step(_gdn_solve(chains, c), states, c)
        for k in range(SEQ_PER_ITER):
            n = SEQ_PER_ITER * i + k
            for h in range(H_GDN):
                cols = slice(h * DV_GDN, (h + 1) * DV_GDN)
                s_ref[n, h] = states[k * H_GDN + h]
                og_ref[n, :, cols] = _gdn_out(o[k * H_GDN + h], z_ref[n, :, cols], gng)
        return carry

    lax.fori_loop(0, nb // SEQ_PER_ITER, body, 0)


def _gdn_sample(upx, z8, ba8, s0, w_conv, alog_pad, dtb_pad, gng, n_valid, nb):
    n = upx.shape[0]
    blk = lambda *shape: pl.BlockSpec((nb,) + shape, lambda i: (i,) + (0,) * len(shape))
    return pl.pallas_call(
        functools.partial(_gdn_sample_body, nb=nb, n_valid=n_valid),
        grid=(n // nb,),
        in_specs=[blk(2 * SUBLANES, GDN_CONV_CH), blk(SUBLANES, GDN_V), blk(SUBLANES, LANES),
                  blk(H_GDN, DK_GDN, DV_GDN),
                  _const_spec((CONV_W, GDN_CONV_CH)), _const_spec((1, LANES)), _const_spec((1, LANES)),
                  _const_spec((1, DV_GDN))],
        out_specs=[blk(SUBLANES, GDN_V), blk(H_GDN, DK_GDN, DV_GDN)],
        out_shape=[jax.ShapeDtypeStruct((n, SUBLANES, GDN_V), F32),
                   jax.ShapeDtypeStruct((n, H_GDN, DK_GDN, DV_GDN), F32)],
        compiler_params=_cparams(("arbitrary",)),
        name="gdn_sample",
    )(upx, z8, ba8, s0, w_conv, alog_pad, dtb_pad, gng)


def _diff_norm(o, g, lam_init):
    return o * lax.rsqrt(jnp.mean(o * o, axis=-1, keepdims=True) + DIFF_NORM_EPS) * g * (1.0 - lam_init)


def _attn_prompt_body(lq1_ref, lk1_ref, lq2_ref, lk2_ref, g_ref, bias_ref, q_ref, k_ref, v_ref, o_ref,
                      *, lam_init, t):
    lam = _lam(lq1_ref, lk1_ref, lq2_ref, lk2_ref, lam_init)
    kb = k_ref[...].astype(BF16)
    vb = v_ref[pl.ds(pl.program_id(1), t, stride=H_DIFF), :].astype(BF16)
    vb = jnp.concatenate([vb, jnp.ones((t, HEAD_W), BF16)], axis=1)
    g = g_ref[...]
    lane = lax.broadcasted_iota(jnp.int32, (1, HEAD_W), 1)
    zero = jnp.zeros((), BF16)
    n_tiles = t // Q_TILE

    def scores(i):
        r0 = i * Q_TILE
        qi = q_ref[r0:r0 + Q_TILE, :] * jnp.asarray(DH_DIFF ** -0.5, BF16)
        q2 = jnp.concatenate([jnp.where(lane < DH_DIFF, qi, zero), jnp.where(lane >= DH_DIFF, qi, zero)], axis=0)
        near0 = max(i - 1, 0) * Q_TILE
        sn = _dot(q2, kb[near0:r0 + Q_TILE], NT) + bias_ref[:, 2 * Q_TILE - (r0 + Q_TILE - near0):]
        return sn, (_dot(q2, kb[:near0], NT) if near0 > 0 else None)

    pending = scores(0)
    for i in range(n_tiles):
        r0 = i * Q_TILE
        near0 = max(i - 1, 0) * Q_TILE
        sn, sf = pending
        if i + 1 < n_tiles:
            pending = scores(i + 1)
        m = jnp.max(sn, axis=-1, keepdims=True)
        if sf is not None:
            m = jnp.maximum(m, jnp.max(sf, axis=-1, keepdims=True))
        o2 = _dot(jnp.exp((sn - m).astype(BF16)), vb[near0:r0 + Q_TILE])
        if sf is not None:
            o2 = o2 + _dot(jnp.exp((sf - m).astype(BF16)), vb[:near0])
        o2 = o2[:, :HEAD_W] * (1.0 / o2[:, HEAD_W:HEAD_W + 1])
        o = o2[:Q_TILE] - lam * o2[Q_TILE:]
        o_ref[r0:r0 + Q_TILE, :] = _diff_norm(o, g, lam_init).astype(o_ref.dtype)


def _attn_prompt(lam_w, diff_g, bias_p, q, k, v, lam_init):
    nb, t, _ = q.shape
    head = lambda: pl.BlockSpec((None, t, HEAD_W), lambda b, h: (b, 0, h))
    small = _const_spec((1, DH_DIFF))
    return pl.pallas_call(
        functools.partial(_attn_prompt_body, lam_init=lam_init, t=t),
        grid=(nb, H_DIFF),
        in_specs=[small, small, small, small, _const_spec((1, HEAD_W)),
                  pl.BlockSpec((None, 2 * Q_TILE, 2 * Q_TILE), lambda b, h: (h, 0, 0)),
                  head(), head(), pl.BlockSpec((None, t * H_DIFF, HEAD_W), lambda b, h: (b, 0, 0))],
        out_specs=head(),
        out_shape=jax.ShapeDtypeStruct((nb, t, DIFF_V), BF16),
        compiler_params=_cparams(("arbitrary", "arbitrary")),
        name="attn_prompt",
    )(*lam_w, diff_g, bias_p, q, k, v)


NROW_S = H_DIFF * SUBLANES
CHUNK_KEYS = PAGES_PER_STEP * PAGE_SIZE
_HEAD_ROWS = [slice(h * SUBLANES, (h + 1) * SUBLANES) for h in range(H_DIFF)]


def _page_copies(pt_ref, ck_hbm, cv_hbm, kbuf, vbuf, sem, chunk, slot):
    vrows = PAGE_SIZE * H_DIFF
    out = []
    for p in range(PAGES_PER_STEP):
        page = pt_ref[chunk * PAGES_PER_STEP + p]
        out.append(pltpu.make_async_copy(ck_hbm.at[page], kbuf.at[slot, :, pl.ds(p * PAGE_SIZE, PAGE_SIZE)],
                                         sem.at[0, slot]))
        out.append(pltpu.make_async_copy(cv_hbm.at[page], vbuf.at[slot, pl.ds(p * vrows, vrows)], sem.at[1, slot]))
    return out


def _sample_queries(q, dec_seq):
    lane_hc = lax.broadcasted_iota(jnp.int32, (SUBLANES, DIFF_QK), 1) // DH_DIFF
    row_c = lax.broadcasted_iota(jnp.int32, (SUBLANES, DIFF_QK), 0) // dec_seq
    q2 = q * (DH_DIFF ** -0.5)
    return jnp.concatenate([jnp.where(lane_hc == 2 * h + row_c, q2, 0.0) for h in range(H_DIFF)],
                           axis=0).astype(BF16)


def _sample_attn_init(qbd, kn, vn, bias_new):
    s = _dot(qbd, kn.astype(BF16), NT) + bias_new
    m = jnp.max(s, axis=-1, keepdims=True)
    p = jnp.exp(s - m)
    pb = p.astype(BF16)
    vn = vn.astype(BF16)
    acc = jnp.concatenate([_dot(pb[_HEAD_ROWS[h]], vn[:, h * HEAD_W:(h + 1) * HEAD_W]) for h in range(H_DIFF)],
                          axis=0)
    return m, jnp.sum(p, axis=-1, keepdims=True), acc


def _sample_attn_update(state, s, vbuf, slot):
    m_old, l_old, acc = state
    m_new = jnp.maximum(m_old, jnp.max(s, axis=-1, keepdims=True))
    alpha = jnp.exp(m_old - m_new)
    p = jnp.exp(s - m_new)
    pb = p.astype(BF16)
    pv = jnp.concatenate(
        [_dot(pb[_HEAD_ROWS[h]], vbuf[slot, pl.ds(h, CHUNK_KEYS, stride=H_DIFF), :].astype(BF16))
         for h in range(H_DIFF)], axis=0)
    return m_new, alpha * l_old + jnp.sum(p, axis=-1, keepdims=True), alpha * acc + pv


def _sample_attn_finish(state, lam, g, lam_init, dec_seq):
    _, l, acc = state
    row = lax.broadcasted_iota(jnp.int32, (NROW_S, 1), 0)
    coef = jnp.where((row % SUBLANES) < dec_seq, 1.0, -lam) / l
    ri = lax.broadcasted_iota(jnp.int32, (NROW_S, NROW_S), 0)
    rj = lax.broadcasted_iota(jnp.int32, (NROW_S, NROW_S), 1)
    sel = ((ri // SUBLANES == rj // SUBLANES) & (ri % SUBLANES == rj % dec_seq)).astype(F32)
    o = _dot_f32(sel, acc * coef)
    return jnp.concatenate([_diff_norm(o[_HEAD_ROWS[h]], g, lam_init) for h in range(H_DIFF)], axis=1)


def _mlp_block(h2b, wup_ref, bup_ref, wdn_ref, cols):
    up = _dot(h2b, wup_ref[:, cols]) + bup_ref[:, cols]
    return _dot(jnp.square(jnp.maximum(up, 0.0)).astype(BF16), wdn_ref[cols, :])


def _post_body(x_ref, mod_ref, og_ref, od_ref, gg_ref, gd_ref, wbg_ref, wbd_ref, wo_ref, l1g_ref, l1b_ref,
               wup_ref, bup_ref, wdn_ref, bdn_ref, l2g_ref, l2b_ref, y_ref, *, per_token):
    md = lambda i: _mod(mod_ref, i, per_token)
    m = (_sigmoid(gg_ref[...].astype(F32)) * _dot(og_ref[...], wbg_ref[...])
         + _sigmoid(gd_ref[...].astype(F32)) * _dot(od_ref[...], wbd_ref[...]))
    mix = _dot(m.astype(BF16), wo_ref[...])
    x1 = _ln(ALPHA * x_ref[...] + md(2) * mix) * l1g_ref[...] + l1b_ref[...]
    h2b = (_ln(x1) * (1.0 + md(4)) + md(3)).astype(BF16)
    f = _mlp_block(h2b, wup_ref, bup_ref, wdn_ref, slice(0, D_FF)) + bdn_ref[...]
    y_ref[...] = _ln(ALPHA * x1 + md(5) * f) * l2g_ref[...] + l2b_ref[...]


def _post(x, mods, og, od, gg, gd, wts, tm, per_token):
    nb, t, _ = x.shape
    if per_token:
        mod_spec = pl.BlockSpec((tm, 6 * D_MODEL), lambda b, i: (i, 0))
    else:
        mod_spec = pl.BlockSpec((None, 6, D_MODEL), lambda b, i: (b, 0, 0))
    row = lambda n: pl.BlockSpec((None, tm, n), lambda b, i: (b, i, 0))
    return pl.pallas_call(
        functools.partial(_post_body, per_token=per_token),
        grid=(nb, t // tm),
        in_specs=[row(D_MODEL), mod_spec, row(GDN_V), row(DIFF_V), row(D_MODEL), row(D_MODEL)]
                 + [_const_spec(w.shape) for w in wts],
        out_specs=row(D_MODEL),
        out_shape=jax.ShapeDtypeStruct((nb, t, D_MODEL), F32),
        compiler_params=_cparams(("arbitrary", "arbitrary")),
        name="post",
    )(x, mods, og, od, gg, gd, *wts)


def _post_attn_body(pt_ref, x_ref, mod_ref, og_ref, od_ref, gg_ref, gd_ref, wbg_ref, wbd_ref, wo_ref, l1g_ref, l1b_ref,
                    wup_ref, bup_ref, wdn_ref, bdn_ref, l2g_ref, l2b_ref,
                    lq1_ref, lk1_ref, lq2_ref, lk2_ref, g_ref, bl_ref, bn_ref, far_ref, q_ref, kn_ref, vn_ref,
                    ck_hbm, cv_hbm, y_ref, o_ref, kbuf, vbuf, sem, *, lam_init, n_chunks, dec_seq, sps):
    step = pl.program_id(0) * pl.num_programs(1) + pl.program_id(1)
    n_steps = pl.num_programs(0) * pl.num_programs(1)
    copies = functools.partial(_page_copies, pt_ref, ck_hbm, cv_hbm, kbuf, vbuf, sem)

    @pl.when(step == 0)
    def _():
        for ahead in range(N_SLOTS - 1):
            for cp in copies(ahead, ahead):
                cp.start()

    md = lambda i: mod_ref[i:i + 1, :]
    m = (_sigmoid(gg_ref[...].astype(F32)) * _dot(og_ref[...], wbg_ref[...])
         + _sigmoid(gd_ref[...].astype(F32)) * _dot(od_ref[...], wbd_ref[...]))
    mix = _dot(m.astype(BF16), wo_ref[...])
    x1 = _ln(ALPHA * x_ref[...] + md(2) * mix) * l1g_ref[...] + l1b_ref[...]
    h2b = (_ln(x1) * (1.0 + md(4)) + md(3)).astype(BF16)

    lam = _lam(lq1_ref, lk1_ref, lq2_ref, lk2_ref, lam_init)
    n_here = sps * n_chunks
    ff = D_FF // n_here
    f = None
    for jj in range(n_here):
        k, j = divmod(jj, n_chunks)
        if j == 0:
            qbd = _sample_queries(q_ref[k], dec_seq)
            state = _sample_attn_init(qbd, kn_ref[k], vn_ref[k], bn_ref[...])
        slot = jj % N_SLOTS
        nxt = jj + N_SLOTS - 1
        if nxt < n_here:
            for cp in copies(step * n_here + nxt, nxt % N_SLOTS):
                cp.start()
        else:
            @pl.when(step + 1 < n_steps)
            def _():
                for cp in copies(step * n_here + nxt, nxt % N_SLOTS):
                    cp.start()
        for cp in copies(step * n_here + jj, slot):
            cp.wait()
        s = _dot(qbd, kbuf[slot].astype(BF16)) + (bl_ref[...] if j == n_chunks - 1 else far_ref[:, 0:1])
        part = _mlp_block(h2b, wup_ref, bup_ref, wdn_ref, slice(jj * ff, (jj + 1) * ff))
        f = part if f is None else f + part
        state = _sample_attn_update(state, s, vbuf, slot)
        if j == n_chunks - 1:
            o_ref[k] = _sample_attn_finish(state, lam, g_ref[...], lam_init, dec_seq)

    y_ref[...] = _ln(ALPHA * x1 + md(5) * (f + bdn_ref[...])) * l2g_ref[...] + l2b_ref[...]


def _post_attn(x, mods, og, od, gg, gd, wts, tm, page_table, lam_w, diff_g, bias_last, bias_new, far, q8, kn8, vn8,
               cache_k, cache_v, lam_init, dec_seq):
    nb, t, _ = x.shape
    n_seq, n_pages = page_table.shape
    n_chunks = n_pages // PAGES_PER_STEP
    steps = t // tm
    sps = n_seq // (nb * steps)
    assert nb * steps * sps == n_seq and (sps * n_chunks) % N_SLOTS == 0 and D_FF % (sps * n_chunks) == 0
    row = lambda n: pl.BlockSpec((None, tm, n), lambda b, i, pt: (b, i, 0))
    const = lambda shape: pl.BlockSpec(shape, lambda b, i, pt: (0,) * len(shape), pipeline_mode=pl.Buffered(1))
    per_seq = pl.BlockSpec((sps, SUBLANES, DIFF_QK), lambda b, i, pt: (b * steps + i, 0, 0))
    grid_spec = pltpu.PrefetchScalarGridSpec(
        num_scalar_prefetch=1,
        grid=(nb, steps),
        in_specs=[row(D_MODEL), pl.BlockSpec((None, 6, D_MODEL), lambda b, i, pt: (b, 0, 0)),
                  row(GDN_V), row(DIFF_V), row(D_MODEL), row(D_MODEL)]
                 + [const(w.shape) for w in wts]
                 + [const((1, DH_DIFF))] * 4
                 + [const((1, HEAD_W)), const((NROW_S, CHUNK_KEYS)), const((NROW_S, SUBLANES)), const((NROW_S, LANES)),
                    per_seq, per_seq, per_seq,
                    pl.BlockSpec(memory_space=pl.ANY), pl.BlockSpec(memory_space=pl.ANY)],
        out_specs=[row(D_MODEL), per_seq],
        scratch_shapes=[pltpu.VMEM((N_SLOTS, DIFF_QK, CHUNK_KEYS), F32),
                        pltpu.VMEM((N_SLOTS, CHUNK_KEYS * H_DIFF, HEAD_W), F32),
                        pltpu.SemaphoreType.DMA((2, N_SLOTS))],
    )
    return pl.pallas_call(
        functools.partial(_post_attn_body, lam_init=lam_init, n_chunks=n_chunks, dec_seq=dec_seq, sps=sps),
        grid_spec=grid_spec,
        out_shape=[jax.ShapeDtypeStruct((nb, t, D_MODEL), F32), jax.ShapeDtypeStruct((n_seq, SUBLANES, DIFF_V), F32)],
        compiler_params=_cparams(("arbitrary", "arbitrary")),
        name="post_attn",
    )(page_table.reshape(-1), x, mods, og, od, gg, gd, *wts, *lam_w, diff_g, bias_last, bias_new, far, q8, kn8, vn8,
      cache_k, cache_v)


def kernel(x_prompt, x_sample, c_prompt, c_sample, cache_k, cache_v, page_table, state_conv, state_gdn, rel_bias,
           w_in, w_conv, a_log, dt_bias, gdn_norm_g, lam_q1, lam_k1, lam_q2, lam_k2, diff_norm_g, w_br_gdn,
           w_br_diff, w_o, w_ada, b_ada, ln1_g, ln1_b, ln2_g, ln2_b, w_up, b_up, w_down, b_down):
    assert DEPTH == 1 and w_in.shape[0] == 1
    l = 0
    lam_init = 0.8 - 0.6 * math.exp(-0.3 * l)
    nbp, seq, _ = x_prompt.shape
    nbs, dec_seq, _ = x_sample.shape
    n_tok_s = nbs * dec_seq
    past_len = page_table.shape[1] * PAGE_SIZE
    assert seq % Q_TILE == 0 and CONV_W - 1 <= dec_seq and 2 * dec_seq == SUBLANES
    assert page_table.shape[1] % PAGES_PER_STEP == 0 and nbs % (8 * SEQ_PER_ITER) == 0

    offs = np.concatenate([[0], np.cumsum(IN_SPLITS)])
    col = lambda i: w_in[l][:, int(offs[i]):int(offs[i + 1])]
    w_main = jnp.concatenate([col(0), col(1), col(4), col(5), col(6), col(7), col(8)], axis=1).astype(BF16)
    w_small = jnp.concatenate([col(2), col(3), jnp.zeros((D_MODEL, LANES - 2 * H_GDN), F32)], axis=1).astype(BF16)
    pad_gate = lambda a: jnp.zeros((1, LANES), F32).at[0, H_GDN:2 * H_GDN].set(a)
    alog_pad, dtb_pad = pad_gate(a_log[l]), pad_gate(dt_bias[l])
    gng = gdn_norm_g[l].reshape(1, DV_GDN)
    diff_g = diff_norm_g[l].reshape(1, HEAD_W)
    lam_w = (lam_q1[l].reshape(1, -1), lam_k1[l].reshape(1, -1), lam_q2[l].reshape(1, -1), lam_k2[l].reshape(1, -1))
    r2 = lambda a: a.reshape(1, -1)
    post_w = (w_br_gdn[l].astype(BF16), w_br_diff[l].astype(BF16), w_o[l].astype(BF16), r2(ln1_g[l]), r2(ln1_b[l]),
              w_up[l].astype(BF16), r2(b_up[l]), w_down[l].astype(BF16), r2(b_down[l]), r2(ln2_g[l]), r2(ln2_b[l]))

    ada = _ada(jnp.concatenate([jnp.repeat(c_sample, dec_seq, axis=0), c_prompt], axis=0), w_ada[l].astype(BF16),
               r2(b_ada[l]))
    mods_p = ada[n_tok_s:].reshape(nbp, 6, D_MODEL)
    mods_s = ada
    bias_p, bias_last, bias_new, far = _bias_prep(rel_bias, past_len, dec_seq)

    qkv_p, z_p, q_p, k_p, v_p, gg_p, gd_p, ba_p, tail_p = _inproj(x_prompt, mods_p, w_main, w_small, 512, w_conv[l])
    og_p, s_p = _gdn_prompt(qkv_p, z_p, ba_p, alog_pad, dtb_pad, gng, 8 * CHUNK)
    od_p = _attn_prompt(lam_w, diff_g, bias_p, q_p, k_p, v_p, lam_init)

    xs = x_sample.reshape(1, n_tok_s, D_MODEL)
    conv_s, z_s, q_s, k_s, v_s, gg_s, gd_s, ba_s = _inproj(xs, mods_s, w_main, w_small, n_tok_s)
    per_seq = lambda a: a.reshape(nbs, dec_seq, a.shape[-1])
    pad8 = lambda a: jnp.pad(a.astype(F32), ((0, 0), (0, SUBLANES - dec_seq), (0, 0)))
    conv_s3 = per_seq(conv_s)
    upx = jnp.concatenate([jnp.zeros((nbs, SUBLANES - (CONV_W - 1), GDN_CONV_CH), F32), state_conv[l], conv_s3,
                           jnp.zeros((nbs, SUBLANES - dec_seq, GDN_CONV_CH), F32)], axis=1)
    og_s8, s_s = _gdn_sample(upx, pad8(per_seq(z_s)), pad8(per_seq(ba_s)), state_gdn[l], w_conv[l], alog_pad,
                             dtb_pad, gng, dec_seq, 8)
    ck = jnp.transpose(cache_k[l], (0, 2, 3, 4, 1)).reshape(cache_k.shape[1], DIFF_QK, PAGE_SIZE)
    cv = cache_v[l].reshape(cache_v.shape[1], PAGE_SIZE * H_DIFF, HEAD_W)
    q_s3 = per_seq(q_s).astype(F32)
    y_p, od_s8 = _post_attn(x_prompt, mods_p, og_p, od_p, gg_p, gd_p, post_w, 256, page_table, lam_w,
                            diff_g, bias_last, bias_new, far, jnp.concatenate([q_s3, q_s3], axis=1),
                            pad8(per_seq(k_s)), pad8(v_s.reshape(nbs, dec_seq, DIFF_V)), ck, cv, lam_init, dec_seq)
    flat = lambda a8: a8[:, :dec_seq].reshape(1, n_tok_s, a8.shape[-1]).astype(BF16)
    y_s = _post(xs, mods_s, flat(og_s8), flat(od_s8), gg_s, gd_s, post_w, 256, True)

    hd = lambda a, b: a.reshape(1, b, -1, H_DIFF, 2, DH_DIFF)
    hv = lambda a, b: a.reshape(1, b, -1, H_DIFF, 2 * DH_DIFF)
    return (y_p, y_s.reshape(nbs, dec_seq, D_MODEL),
            hd(k_p, nbp), hv(v_p, nbp), tail_p[:, SUBLANES - (CONV_W - 1):][None], s_p[None],
            hd(k_s, nbs), hv(v_s, nbs), conv_s3[:, dec_seq - (CONV_W - 1):][None], s_s[None])
```

```python
import functools
import math

import numpy as np
import jax
import jax.numpy as jnp
from jax import lax
from jax.experimental import pallas as pl
from jax.experimental.pallas import tpu as pltpu

F32 = jnp.float32
BF16 = jnp.bfloat16

D_MODEL = 1024
DEPTH = 1
PAGE_SIZE = 128
H_GDN = 4
DK_GDN = 128
DV_GDN = 128
CONV_W = 4
CHUNK = 64
GDN_NORM_EPS = 1e-6
H_DIFF = 4
DH_DIFF = 64
DIFF_NORM_EPS = 1e-5
N_BUCKETS = 32
MAX_DISTANCE = 128
D_FF = 4 * D_MODEL
LN_EPS = 1e-5
GDN_QK = H_GDN * DK_GDN
GDN_V = H_GDN * DV_GDN
GDN_CONV_CH = 2 * GDN_QK + GDN_V
DIFF_QK = H_DIFF * 2 * DH_DIFF
DIFF_V = H_DIFF * 2 * DH_DIFF
HEAD_W = 2 * DH_DIFF
IN_SPLITS = (GDN_CONV_CH, GDN_V, H_GDN, H_GDN, DIFF_QK, DIFF_QK, DIFF_V, D_MODEL, D_MODEL)
ALPHA = (2.0 * DEPTH) ** 0.25

LANES = 128
SUBLANES = 8
VMEM_LIMIT = 56 * 1024 * 1024
Q_TILE = 256
PAGES_PER_STEP = 8
N_SLOTS = 4
SEQ_PER_ITER = 4
MASKED = N_BUCKETS

NN = (((1,), (0,)), ((), ()))
NT = (((1,), (1,)), ((), ()))
TN = (((0,), (0,)), ((), ()))


def _cparams(sem):
    return pltpu.CompilerParams(dimension_semantics=sem, vmem_limit_bytes=VMEM_LIMIT)


def _const_spec(shape):
    nd = len(shape)
    return pl.BlockSpec(shape, lambda *_: (0,) * nd, pipeline_mode=pl.Buffered(1))


def _sigmoid(x):
    return 1.0 / (1.0 + jnp.exp(-x))


def _silu(x):
    return x * _sigmoid(x)


def _softplus(x):
    return jnp.maximum(x, 0.0) + jnp.log(1.0 + jnp.exp(-jnp.abs(x)))


def _ln(x):
    mu = jnp.mean(x, axis=-1, keepdims=True)
    xc = x - mu
    var = jnp.mean(xc * xc, axis=-1, keepdims=True)
    return xc * lax.rsqrt(var + LN_EPS)


def _dot(a, b, dims=NN):
    return lax.dot_general(a, b, dims, preferred_element_type=F32)


def _dot_bf(a, b, dims=NN):
    return _dot(a.astype(BF16), b.astype(BF16), dims)


def _split(a):
    hi = a.astype(BF16)
    lo = (a - hi.astype(F32)).astype(BF16)
    return hi, lo


def _dot3(a, b, dims=NN, fuse=True):
    ah, al = _split(a)
    bh, bl = _split(b)
    if not fuse:
        return _dot(ah, bh, dims) + (_dot(ah, bl, dims) + _dot(al, bh, dims))
    lhs = jnp.concatenate([ah, ah, al], axis=1)
    rhs = jnp.concatenate([bh, bl, bh], axis=1 if dims == NT else 0)
    return _dot(lhs, rhs, dims)


def _dot_f32(a, b, dims=NN):
    return lax.dot_general(a, b, dims, precision=lax.Precision.HIGHEST, preferred_element_type=F32)


def _mod(mod_ref, i, per_token):
    return mod_ref[:, i * D_MODEL:(i + 1) * D_MODEL] if per_token else mod_ref[i:i + 1, :]


def _ada_body(c_ref, w_ref, b_ref, o_ref):
    s = _silu(c_ref[...])
    o_ref[...] = _dot(s.astype(BF16), w_ref[...]) + b_ref[...]


def _ada(c_all, w_ada, b_ada):
    n = c_all.shape[0]
    return pl.pallas_call(
        _ada_body,
        grid=(6,),
        in_specs=[pl.BlockSpec((n, D_MODEL), lambda j: (0, 0)),
                  pl.BlockSpec((D_MODEL, D_MODEL), lambda j: (0, j)),
                  pl.BlockSpec((1, D_MODEL), lambda j: (0, j))],
        out_specs=pl.BlockSpec((n, D_MODEL), lambda j: (0, j)),
        out_shape=jax.ShapeDtypeStruct((n, 6 * D_MODEL), F32),
        compiler_params=_cparams(("arbitrary",)),
        name="ada",
    )(c_all, w_ada, b_ada)


def _t5_bucket_np(rel):
    n = np.maximum(rel, 0)
    max_exact = N_BUCKETS // 2
    nf = np.maximum(n, 1).astype(np.float32)
    large = max_exact + (np.log(nf / np.float32(max_exact)) / np.float32(math.log(MAX_DISTANCE / max_exact))
                         * np.float32(N_BUCKETS - max_exact)).astype(np.int32)
    large = np.minimum(large, N_BUCKETS - 1)
    return np.where(n < max_exact, n, large).astype(np.int32)


def _bias_codes(past_len, dec_seq):
    r = np.arange(Q_TILE)[:, None]
    c = np.arange(Q_TILE)[None, :]
    prev_tile = _t5_bucket_np(Q_TILE + r - c)
    diag_tile = np.where(r >= c, _t5_bucket_np(r - c), MASKED)
    prompt = np.concatenate([prev_tile, diag_tile], axis=1).astype(np.int32)
    prompt = np.concatenate([prompt, prompt], axis=0)
    nrow = H_DIFF * SUBLANES
    qrow = (np.arange(nrow) % dec_seq)[:, None]
    nkeys = PAGES_PER_STEP * PAGE_SIZE
    kpos = past_len - nkeys + np.arange(nkeys)[None, :]
    last = _t5_bucket_np(past_len + qrow - kpos).astype(np.int32)
    j = np.arange(SUBLANES)[None, :]
    new = np.where((j <= qrow) & (j < dec_seq), _t5_bucket_np(qrow - j), MASKED).astype(np.int32)
    return prompt, last, new


def _bias_body(tbl_ref, cp_ref, cl_ref, cn_ref, bp_ref, bl_ref, bn_ref, far_ref):
    def lookup(codes, h):
        acc = jnp.full(codes.shape, -jnp.inf, F32)
        for b in range(N_BUCKETS):
            acc = jnp.where(codes == b, tbl_ref[b, h], acc)
        return acc

    cp = cp_ref[...]
    for h in range(H_DIFF):
        bp_ref[h] = lookup(cp, h) - tbl_ref[N_BUCKETS - 1, h]
    cl = cl_ref[...]
    cn = cn_ref[...]
    nrow = cl.shape[0]
    row_head = lax.broadcasted_iota(jnp.int32, (nrow, 1), 0) // SUBLANES
    bl = jnp.zeros(cl.shape, F32)
    bn = jnp.zeros(cn.shape, F32)
    far = jnp.zeros((nrow, LANES), F32)
    for h in range(H_DIFF):
        bl = jnp.where(row_head == h, lookup(cl, h), bl)
        bn = jnp.where(row_head == h, lookup(cn, h), bn)
        far = jnp.where(row_head == h, tbl_ref[N_BUCKETS - 1, h], far)
    bl_ref[...] = bl
    bn_ref[...] = bn
    far_ref[...] = far


def _bias_prep(rel_bias, past_len, dec_seq):
    cp, cl, cn = _bias_codes(past_len, dec_seq)
    nrow = cl.shape[0]
    vm = pl.BlockSpec(memory_space=pltpu.VMEM)
    return pl.pallas_call(
        _bias_body,
        in_specs=[pl.BlockSpec(memory_space=pltpu.SMEM), vm, vm, vm],
        out_specs=[vm, vm, vm, vm],
        out_shape=[jax.ShapeDtypeStruct((H_DIFF,) + cp.shape, F32),
                   jax.ShapeDtypeStruct(cl.shape, F32),
                   jax.ShapeDtypeStruct(cn.shape, F32),
                   jax.ShapeDtypeStruct((nrow, LANES), F32)],
        name="bias_prep",
    )(rel_bias, jnp.asarray(cp), jnp.asarray(cl), jnp.asarray(cn))


def _lam(lq1_ref, lk1_ref, lq2_ref, lk2_ref, lam_init):
    s1 = jnp.sum(lq1_ref[...] * lk1_ref[...], axis=-1, keepdims=True)
    s2 = jnp.sum(lq2_ref[...] * lk2_ref[...], axis=-1, keepdims=True)
    return jnp.exp(s1) - jnp.exp(s2) + lam_init


_SEG = {"conv": (0, 1536), "z": (1536, 2048), "q": (2048, 2560), "k": (2560, 3072), "v": (3072, 3584),
        "gg": (3584, 4608), "gd": (4608, 5632)}
N_MAIN = 5632


def _inproj_h(x, mod_ref, per_token):
    h = _ln(x) * (1.0 + _mod(mod_ref, 1, per_token)) + _mod(mod_ref, 0, per_token)
    return h.astype(BF16)


def _inproj_seg(hb, wm_ref, name):
    a, b = _SEG[name]
    return _dot(hb, wm_ref[:, a:b])


def _inproj_rest(hb, r0, wm_ref, ws_ref, z_ref, q_ref, k_ref, v_ref, gg_ref, gd_ref, ba_ref):
    seg = functools.partial(_inproj_seg, hb, wm_ref)
    n = hb.shape[0]
    rows = slice(r0, r0 + n)
    z_ref[rows, :] = seg("z").astype(BF16)
    q_ref[rows, :] = seg("q").astype(BF16)
    k_ref[rows, :] = seg("k")
    sv = seg("v")
    for hh in range(H_DIFF):
        v_ref[pl.ds(r0 * H_DIFF + hh, n, stride=H_DIFF), :] = sv[:, hh * HEAD_W:(hh + 1) * HEAD_W]
    gg_ref[rows, :] = seg("gg").astype(BF16)
    gd_ref[rows, :] = seg("gd").astype(BF16)
    ba_ref[rows, :] = _dot(hb, ws_ref[...])


def _inproj_body(x_ref, mod_ref, wm_ref, ws_ref, conv_ref, z_ref, q_ref, k_ref, v_ref, gg_ref, gd_ref, ba_ref):
    hb = _inproj_h(x_ref[...], mod_ref, True)
    conv_ref[...] = _inproj_seg(hb, wm_ref, "conv")
    _inproj_rest(hb, 0, wm_ref, ws_ref, z_ref, q_ref, k_ref, v_ref, gg_ref, gd_ref, ba_ref)


def _inproj_conv_body(x_ref, mod_ref, wm_ref, ws_ref, wc_ref, qkv_ref, z_ref, q_ref, k_ref, v_ref, gg_ref, gd_ref,
                      ba_ref, tail_ref, cbuf_ref, *, tm):
    @pl.when(pl.program_id(1) == 0)
    def _():
        cbuf_ref[0:SUBLANES, :] = jnp.zeros((SUBLANES, GDN_CONV_CH), F32)

    sub = tm // 2
    hbs = [_inproj_h(x_ref[s * sub:(s + 1) * sub, :], mod_ref, False) for s in range(2)]
    for s in range(2):
        r0 = s * sub
        rows = slice(r0, r0 + sub)
        u_in = _inproj_seg(hbs[s], wm_ref, "conv")
        _inproj_rest(hbs[s], r0, wm_ref, ws_ref, z_ref, q_ref, k_ref, v_ref, gg_ref, gd_ref, ba_ref)
        cbuf_ref[SUBLANES + r0:SUBLANES + r0 + sub, :] = u_in
        y = u_in * wc_ref[CONV_W - 1:CONV_W, :]
        for j in range(CONV_W - 1):
            off = SUBLANES - (CONV_W - 1) + j + r0
            y = y + cbuf_ref[off:off + sub, :] * wc_ref[j:j + 1, :]
        qkv = _silu(y)
        for h in range(H_GDN):
            lo = h * DK_GDN
            qkv_ref[rows, lo:lo + DK_GDN] = _l2n(qkv[:, lo:lo + DK_GDN]) * (DK_GDN ** -0.5)
            qkv_ref[rows, GDN_QK + lo:GDN_QK + lo + DK_GDN] = _l2n(qkv[:, GDN_QK + lo:GDN_QK + lo + DK_GDN])
        qkv_ref[rows, 2 * GDN_QK:] = qkv[:, 2 * GDN_QK:]
    last = cbuf_ref[tm:tm + SUBLANES, :]
    cbuf_ref[0:SUBLANES, :] = last
    tail_ref[...] = last


def _inproj(x, mods, w_main, w_small, tm, w_conv=None):
    nb, t, _ = x.shape
    fuse = w_conv is not None

    def out(n, dt):
        return pl.BlockSpec((None, tm, n), lambda b, i: (b, i, 0)), jax.ShapeDtypeStruct((nb, t, n), dt)

    v_out = (pl.BlockSpec((None, tm * H_DIFF, HEAD_W), lambda b, i: (b, i, 0)),
             jax.ShapeDtypeStruct((nb, t * H_DIFF, HEAD_W), F32))
    outs = [out(GDN_CONV_CH, F32), out(GDN_V, BF16), out(DIFF_QK, BF16), out(DIFF_QK, F32), v_out,
            out(D_MODEL, BF16), out(D_MODEL, BF16), out(LANES, F32)]
    in_specs = [pl.BlockSpec((None, tm, D_MODEL), lambda b, i: (b, i, 0)),
                pl.BlockSpec((None, 6, D_MODEL), lambda b, i: (b, 0, 0)) if fuse
                else pl.BlockSpec((tm, 6 * D_MODEL), lambda b, i: (i, 0)),
                _const_spec((D_MODEL, N_MAIN)), _const_spec((D_MODEL, LANES))]
    args = [x, mods, w_main, w_small]
    scratch = []
    if fuse:
        in_specs.append(_const_spec((CONV_W, GDN_CONV_CH)))
        args.append(w_conv)
        outs.append((pl.BlockSpec((None, SUBLANES, GDN_CONV_CH), lambda b, i: (b, 0, 0)),
                     jax.ShapeDtypeStruct((nb, SUBLANES, GDN_CONV_CH), F32)))
        scratch.append(pltpu.VMEM((tm + SUBLANES, GDN_CONV_CH), F32))
    return pl.pallas_call(
        functools.partial(_inproj_conv_body, tm=tm) if fuse else _inproj_body,
        grid=(nb, t // tm),
        in_specs=in_specs,
        out_specs=[o[0] for o in outs],
        out_shape=[o[1] for o in outs],
        scratch_shapes=scratch,
        compiler_params=_cparams(("arbitrary", "arbitrary")),
        name="inproj_conv" if fuse else "inproj",
    )(*args)


def _gdn_solve(chains, c):
    ii = lax.broadcasted_iota(jnp.int32, (c, c), 0)
    jj = lax.broadcasted_iota(jnp.int32, (c, c), 1)
    incl = ii >= jj
    eye = (ii == jj).astype(F32)
    fuse = c % CHUNK == 0
    gamma = [jnp.where(incl, jnp.exp(jnp.where(incl, ch[4] - ch[5], 0.0)), 0.0) for ch in chains]
    kq = [_dot_bf(jnp.concatenate([ch[1], ch[0]], axis=0), ch[1], NT) for ch in chains]
    a = [jnp.where(ii > jj, ch[3] * k[:c] * g, 0.0) for ch, k, g in zip(chains, kq, gamma)]
    a8 = [jnp.where((ii >> 3) == (jj >> 3), m, 0.0) for m in a]
    p = [_dot3(m, m, fuse=fuse) for m in a8]
    x = [eye - m for m in a8]
    x = [xi + _dot3(xi, pi, fuse=fuse) for xi, pi in zip(x, p)]
    p = [_dot3(pi, pi, fuse=fuse) for pi in p]
    x = [xi + _dot3(xi, pi, fuse=fuse) for xi, pi in zip(x, p)]
    sh = 3
    while (1 << sh) < c:
        lvl = ((ii >> (sh + 1)) == (jj >> (sh + 1))) & ((ii >> sh) != (jj >> sh))
        t = [_dot_bf(jnp.where(lvl, m, 0.0), xi) for m, xi in zip(a, x)]
        x = [xi - _dot_bf(xi, ti) for xi, ti in zip(x, t)]
        sh += 1
    egc = [jnp.exp(ch[4]) for ch in chains]
    uw = [_dot3(xi, jnp.concatenate([ch[2] * ch[3], ch[1] * (ch[3] * e)], axis=1), fuse=fuse)
          for xi, ch, e in zip(x, chains, egc)]
    qk = [k[c:] * g for k, g in zip(kq, gamma)]
    out = []
    for ch, e, uwi, qki in zip(chains, egc, uw, qk):
        g_last = ch[4][c - 1:c, :]
        wq = jnp.concatenate([uwi[:, DV_GDN:], ch[0] * e], axis=0).astype(BF16)
        k_dec = (ch[1] * jnp.exp(g_last - ch[4])).astype(BF16)
        out.append((uwi[:, :DV_GDN], wq, qki.astype(BF16), k_dec, jnp.exp(g_last)))
    return out


def _gdn_state_step(sols, states, c):
    sb = [s.astype(BF16) for s in states]
    wq = [_dot(sol[1], b) for sol, b in zip(sols, sb)]
    vb = [(sol[0] - m[:c]).astype(BF16) for sol, m in zip(sols, wq)]
    o = [m[c:] + _dot(sol[2], v) for sol, m, v in zip(sols, wq, vb)]
    s_new = [sol[4] * s + _dot(sol[3], v, TN) for sol, s, v in zip(sols, states, vb)]
    return o, s_new


def _l2n(x):
    return x * lax.rsqrt(jnp.sum(x * x, axis=-1, keepdims=True) + 1e-6)


def _gdn_gates(ba, alog_ref, dtb_ref):
    beta = _sigmoid(ba)
    g = -jnp.exp(alog_ref[...]) * _softplus(ba + dtb_ref[...])
    return beta, g


def _gdn_out(o, z, gng):
    o = o * lax.rsqrt(jnp.mean(o * o, axis=-1, keepdims=True) + GDN_NORM_EPS) * gng
    return o * _silu(z)


def _cumsum_mats(c):
    ii = lax.broadcasted_iota(jnp.int32, (c, c), 0)
    jj = lax.broadcasted_iota(jnp.int32, (c, c), 1)
    return (ii >= jj).astype(F32)


def _gdn_prompt_body(qkv_ref, z_ref, ba_ref, alog_ref, dtb_ref, gng_ref, og_ref, s_ref, *, tt):
    @pl.when(pl.program_id(1) == 0)
    def _():
        s_ref[...] = jnp.zeros(s_ref.shape, F32)

    beta_all, g_all = _gdn_gates(ba_ref[...], alog_ref, dtb_ref)
    ltri = _cumsum_mats(CHUNK)
    nch = tt // CHUNK
    gc_all = jnp.concatenate([_dot_f32(ltri, g_all[c * CHUNK:(c + 1) * CHUNK, :]) for c in range(nch)], axis=0)
    gc_t = gc_all.T
    gng = gng_ref[...]
    chains = []
    for c in range(nch):
        rows = slice(c * CHUNK, (c + 1) * CHUNK)
        for h in range(H_GDN):
            lo = h * DK_GDN
            chains.append((qkv_ref[rows, lo:lo + DK_GDN],
                           qkv_ref[rows, GDN_QK + lo:GDN_QK + lo + DK_GDN],
                           qkv_ref[rows, 2 * GDN_QK + lo:2 * GDN_QK + lo + DV_GDN],
                           beta_all[rows, h:h + 1], gc_all[rows, H_GDN + h:H_GDN + h + 1],
                           gc_t[H_GDN + h:H_GDN + h + 1, rows]))
    sols = _gdn_solve(chains, CHUNK)
    states = [s_ref[h] for h in range(H_GDN)]
    for c in range(nch):
        rows = slice(c * CHUNK, (c + 1) * CHUNK)
        o, states = _gdn_state_step(sols[c * H_GDN:(c + 1) * H_GDN], states, CHUNK)
        for h in range(H_GDN):
            cols = slice(h * DV_GDN, (h + 1) * DV_GDN)
            og_ref[rows, cols] = _gdn_out(o[h], z_ref[rows, cols].astype(F32), gng).astype(og_ref.dtype)
    for h in range(H_GDN):
        s_ref[h] = states[h]


def _gdn_prompt(qkv_act, z, ba, alog_pad, dtb_pad, gng, tt):
    nb, t, _ = qkv_act.shape
    row = lambda n: pl.BlockSpec((None, tt, n), lambda b, i: (b, i, 0))
    return pl.pallas_call(
        functools.partial(_gdn_prompt_body, tt=tt),
        grid=(nb, t // tt),
        in_specs=[row(GDN_CONV_CH), row(GDN_V), row(LANES),
                  _const_spec((1, LANES)), _const_spec((1, LANES)), _const_spec((1, DV_GDN))],
        out_specs=[row(GDN_V), pl.BlockSpec((None, H_GDN, DK_GDN, DV_GDN), lambda b, i: (b, 0, 0, 0))],
        out_shape=[jax.ShapeDtypeStruct((nb, t, GDN_V), BF16),
                   jax.ShapeDtypeStruct((nb, H_GDN, DK_GDN, DV_GDN), F32)],
        compiler_params=_cparams(("arbitrary", "arbitrary")),
        name="gdn_prompt",
    )(qkv_act, z, ba, alog_pad, dtb_pad, gng)


def _gdn_sample_body(upx_ref, z_ref, ba_ref, s0_ref, wc_ref, alog_ref, dtb_ref, gng_ref, og_ref, s_ref,
                     *, nb, n_valid):
    c = SUBLANES
    valid = lax.broadcasted_iota(jnp.int32, (c, 1), 0) < n_valid
    ltri = _cumsum_mats(c)
    gng = gng_ref[...]

    def body(i, carry):
        chains, states = [], []
        for n in (SEQ_PER_ITER * i + k for k in range(SEQ_PER_ITER)):
            y = upx_ref[n, pl.ds(SUBLANES, c), :] * wc_ref[CONV_W - 1:CONV_W, :]
            for j in range(CONV_W - 1):
                off = SUBLANES - (CONV_W - 1) + j
                y = y + upx_ref[n, pl.ds(off, c), :] * wc_ref[j:j + 1, :]
            qkv = _silu(y)
            beta_all, g_all = _gdn_gates(ba_ref[n], alog_ref, dtb_ref)
            beta_all = jnp.where(valid, beta_all, 0.0)
            g_all = jnp.where(valid, g_all, 0.0)
            gc_all = _dot_f32(ltri, g_all)
            gc_t = jnp.concatenate([gc_all, jnp.zeros((LANES - c, LANES), F32)], axis=0).T
            for h in range(H_GDN):
                lo = h * DK_GDN
                chains.append((_l2n(qkv[:, lo:lo + DK_GDN]) * (DK_GDN ** -0.5),
                               jnp.where(valid, _l2n(qkv[:, GDN_QK + lo:GDN_QK + lo + DK_GDN]), 0.0),
                               jnp.where(valid, qkv[:, 2 * GDN_QK + lo:2 * GDN_QK + lo + DV_GDN], 0.0),
                               beta_all[:, h:h + 1], gc_all[:, H_GDN + h:H_GDN + h + 1],
                               gc_t[H_GDN + h:H_GDN + h + 1, 0:c]))
                states.append(s0_ref[n, h])
        o, states = _gdn_state_step(_gdn_solve(chains, c), states, c)
        for k in range(SEQ_PER_ITER):
            n = SEQ_PER_ITER * i + k
            for h in range(H_GDN):
                cols = slice(h * DV_GDN, (h + 1) * DV_GDN)
                s_ref[n, h] = states[k * H_GDN + h]
                og_ref[n, :, cols] = _gdn_out(o[k * H_GDN + h], z_ref[n, :, cols], gng)
        return carry

    lax.fori_loop(0, nb // SEQ_PER_ITER, body, 0)


def _gdn_sample(upx, z8, ba8, s0, w_conv, alog_pad, dtb_pad, gng, n_valid, nb):
    n = upx.shape[0]
    blk = lambda *shape: pl.BlockSpec((nb,) + shape, lambda i: (i,) + (0,) * len(shape))
    return pl.pallas_call(
        functools.partial(_gdn_sample_body, nb=nb, n_valid=n_valid),
        grid=(n // nb,),
        in_specs=[blk(2 * SUBLANES, GDN_CONV_CH), blk(SUBLANES, GDN_V), blk(SUBLANES, LANES),
                  blk(H_GDN, DK_GDN, DV_GDN),
                  _const_spec((CONV_W, GDN_CONV_CH)), _const_spec((1, LANES)), _const_spec((1, LANES)),
                  _const_spec((1, DV_GDN))],
        out_specs=[blk(SUBLANES, GDN_V), blk(H_GDN, DK_GDN, DV_GDN)],
        out_shape=[jax.ShapeDtypeStruct((n, SUBLANES, GDN_V), F32),
                   jax.ShapeDtypeStruct((n, H_GDN, DK_GDN, DV_GDN), F32)],
        compiler_params=_cparams(("arbitrary",)),
        name="gdn_sample",
    )(upx, z8, ba8, s0, w_conv, alog_pad, dtb_pad, gng)


def _diff_norm(o, g, lam_init):
    return o * lax.rsqrt(jnp.mean(o * o, axis=-1, keepdims=True) + DIFF_NORM_EPS) * g * (1.0 - lam_init)


def _attn_prompt_body(lq1_ref, lk1_ref, lq2_ref, lk2_ref, g_ref, bias_ref, q_ref, k_ref, v_ref, o_ref,
                      *, lam_init, t):
    lam = _lam(lq1_ref, lk1_ref, lq2_ref, lk2_ref, lam_init)
    kb = k_ref[...].astype(BF16)
    vb = v_ref[pl.ds(pl.program_id(1), t, stride=H_DIFF), :].astype(BF16)
    vb = jnp.concatenate([vb, jnp.ones((t, HEAD_W), BF16)], axis=1)
    g = g_ref[...]
    lane = lax.broadcasted_iota(jnp.int32, (1, HEAD_W), 1)
    zero = jnp.zeros((), BF16)
    n_tiles = t // Q_TILE

    def scores(i):
        r0 = i * Q_TILE
        qi = q_ref[r0:r0 + Q_TILE, :] * jnp.asarray(DH_DIFF ** -0.5, BF16)
        q2 = jnp.concatenate([jnp.where(lane < DH_DIFF, qi, zero), jnp.where(lane >= DH_DIFF, qi, zero)], axis=0)
        near0 = max(i - 1, 0) * Q_TILE
        sn = _dot(q2, kb[near0:r0 + Q_TILE], NT) + bias_ref[:, 2 * Q_TILE - (r0 + Q_TILE - near0):]
        return sn, (_dot(q2, kb[:near0], NT) if near0 > 0 else None)

    pending = scores(0)
    for i in range(n_tiles):
        r0 = i * Q_TILE
        near0 = max(i - 1, 0) * Q_TILE
        sn, sf = pending
        if i + 1 < n_tiles:
            pending = scores(i + 1)
        m = jnp.max(sn, axis=-1, keepdims=True)
        if sf is not None:
            m = jnp.maximum(m, jnp.max(sf, axis=-1, keepdims=True))
        o2 = _dot(jnp.exp((sn - m).astype(BF16)), vb[near0:r0 + Q_TILE])
        if sf is not None:
            o2 = o2 + _dot(jnp.exp((sf - m).astype(BF16)), vb[:near0])
        o2 = o2[:, :HEAD_W] * (1.0 / o2[:, HEAD_W:HEAD_W + 1])
        o = o2[:Q_TILE] - lam * o2[Q_TILE:]
        o_ref[r0:r0 + Q_TILE, :] = _diff_norm(o, g, lam_init).astype(o_ref.dtype)


def _attn_prompt(lam_w, diff_g, bias_p, q, k, v, lam_init):
    nb, t, _ = q.shape
    head = lambda: pl.BlockSpec((None, t, HEAD_W), lambda b, h: (b, 0, h))
    small = _const_spec((1, DH_DIFF))
    return pl.pallas_call(
        functools.partial(_attn_prompt_body, lam_init=lam_init, t=t),
        grid=(nb, H_DIFF),
        in_specs=[small, small, small, small, _const_spec((1, HEAD_W)),
                  pl.BlockSpec((None, 2 * Q_TILE, 2 * Q_TILE), lambda b, h: (h, 0, 0)),
                  head(), head(), pl.BlockSpec((None, t * H_DIFF, HEAD_W), lambda b, h: (b, 0, 0))],
        out_specs=head(),
        out_shape=jax.ShapeDtypeStruct((nb, t, DIFF_V), BF16),
        compiler_params=_cparams(("arbitrary", "arbitrary")),
        name="attn_prompt",
    )(*lam_w, diff_g, bias_p, q, k, v)


NROW_S = H_DIFF * SUBLANES
CHUNK_KEYS = PAGES_PER_STEP * PAGE_SIZE
_HEAD_ROWS = [slice(h * SUBLANES, (h + 1) * SUBLANES) for h in range(H_DIFF)]


def _page_copies(pt_ref, ck_hbm, cv_hbm, kbuf, vbuf, sem, chunk, slot):
    vrows = PAGE_SIZE * H_DIFF
    out = []
    for p in range(PAGES_PER_STEP):
        page = pt_ref[chunk * PAGES_PER_STEP + p]
        out.append(pltpu.make_async_copy(ck_hbm.at[page], kbuf.at[slot, :, pl.ds(p * PAGE_SIZE, PAGE_SIZE)],
                                         sem.at[0, slot]))
        out.append(pltpu.make_async_copy(cv_hbm.at[page], vbuf.at[slot, pl.ds(p * vrows, vrows)], sem.at[1, slot]))
    return out


def _sample_queries(q, dec_seq):
    lane_hc = lax.broadcasted_iota(jnp.int32, (SUBLANES, DIFF_QK), 1) // DH_DIFF
    row_c = lax.broadcasted_iota(jnp.int32, (SUBLANES, DIFF_QK), 0) // dec_seq
    q2 = q * (DH_DIFF ** -0.5)
    return jnp.concatenate([jnp.where(lane_hc == 2 * h + row_c, q2, 0.0) for h in range(H_DIFF)],
                           axis=0).astype(BF16)


def _sample_attn_init(qbd, kn, vn, bias_new):
    s = _dot(qbd, kn.astype(BF16), NT) + bias_new
    m = jnp.max(s, axis=-1, keepdims=True)
    p = jnp.exp(s - m)
    pb = p.astype(BF16)
    vn = vn.astype(BF16)
    acc = jnp.concatenate([_dot(pb[_HEAD_ROWS[h]], vn[:, h * HEAD_W:(h + 1) * HEAD_W]) for h in range(H_DIFF)],
                          axis=0)
    return m, jnp.sum(p, axis=-1, keepdims=True), acc


def _sample_attn_update(state, s, vbuf, slot):
    m_old, l_old, acc = state
    m_new = jnp.maximum(m_old, jnp.max(s, axis=-1, keepdims=True))
    alpha = jnp.exp(m_old - m_new)
    p = jnp.exp(s - m_new)
    pb = p.astype(BF16)
    pv = jnp.concatenate(
        [_dot(pb[_HEAD_ROWS[h]], vbuf[slot, pl.ds(h, CHUNK_KEYS, stride=H_DIFF), :].astype(BF16))
         for h in range(H_DIFF)], axis=0)
    return m_new, alpha * l_old + jnp.sum(p, axis=-1, keepdims=True), alpha * acc + pv


def _sample_attn_finish(state, lam, g, lam_init, dec_seq):
    _, l, acc = state
    row = lax.broadcasted_iota(jnp.int32, (NROW_S, 1), 0)
    coef = jnp.where((row % SUBLANES) < dec_seq, 1.0, -lam) / l
    ri = lax.broadcasted_iota(jnp.int32, (NROW_S, NROW_S), 0)
    rj = lax.broadcasted_iota(jnp.int32, (NROW_S, NROW_S), 1)
    sel = ((ri // SUBLANES == rj // SUBLANES) & (ri % SUBLANES == rj % dec_seq)).astype(F32)
    o = _dot_f32(sel, acc * coef)
    return jnp.concatenate([_diff_norm(o[_HEAD_ROWS[h]], g, lam_init) for h in range(H_DIFF)], axis=1)


def _mlp_block(h2b, wup_ref, bup_ref, wdn_ref, cols):
    up = _dot(h2b, wup_ref[:, cols]) + bup_ref[:, cols]
    return _dot(jnp.square(jnp.maximum(up, 0.0)).astype(BF16), wdn_ref[cols, :])


def _post_body(x_ref, mod_ref, og_ref, od_ref, gg_ref, gd_ref, wbg_ref, wbd_ref, wo_ref, l1g_ref, l1b_ref,
               wup_ref, bup_ref, wdn_ref, bdn_ref, l2g_ref, l2b_ref, y_ref, *, per_token):
    md = lambda i: _mod(mod_ref, i, per_token)
    m = (_sigmoid(gg_ref[...].astype(F32)) * _dot(og_ref[...], wbg_ref[...])
         + _sigmoid(gd_ref[...].astype(F32)) * _dot(od_ref[...], wbd_ref[...]))
    mix = _dot(m.astype(BF16), wo_ref[...])
    x1 = _ln(ALPHA * x_ref[...] + md(2) * mix) * l1g_ref[...] + l1b_ref[...]
    h2b = (_ln(x1) * (1.0 + md(4)) + md(3)).astype(BF16)
    f = _mlp_block(h2b, wup_ref, bup_ref, wdn_ref, slice(0, D_FF)) + bdn_ref[...]
    y_ref[...] = _ln(ALPHA * x1 + md(5) * f) * l2g_ref[...] + l2b_ref[...]


def _post(x, mods, og, od, gg, gd, wts, tm, per_token):
    nb, t, _ = x.shape
    if per_token:
        mod_spec = pl.BlockSpec((tm, 6 * D_MODEL), lambda b, i: (i, 0))
    else:
        mod_spec = pl.BlockSpec((None, 6, D_MODEL), lambda b, i: (b, 0, 0))
    row = lambda n: pl.BlockSpec((None, tm, n), lambda b, i: (b, i, 0))
    return pl.pallas_call(
        functools.partial(_post_body, per_token=per_token),
        grid=(nb, t // tm),
        in_specs=[row(D_MODEL), mod_spec, row(GDN_V), row(DIFF_V), row(D_MODEL), row(D_MODEL)]
                 + [_const_spec(w.shape) for w in wts],
        out_specs=row(D_MODEL),
        out_shape=jax.ShapeDtypeStruct((nb, t, D_MODEL), F32),
        compiler_params=_cparams(("arbitrary", "arbitrary")),
        name="post",
    )(x, mods, og, od, gg, gd, *wts)


def _post_attn_body(pt_ref, x_ref, mod_ref, og_ref, od_ref, gg_ref, gd_ref, wbg_ref, wbd_ref, wo_ref, l1g_ref, l1b_ref,
                    wup_ref, bup_ref, wdn_ref, bdn_ref, l2g_ref, l2b_ref,
                    lq1_ref, lk1_ref, lq2_ref, lk2_ref, g_ref, bl_ref, bn_ref, far_ref, q_ref, kn_ref, vn_ref,
                    ck_hbm, cv_hbm, y_ref, o_ref, kbuf, vbuf, sem, *, lam_init, n_chunks, dec_seq, sps):
    step = pl.program_id(0) * pl.num_programs(1) + pl.program_id(1)
    n_steps = pl.num_programs(0) * pl.num_programs(1)
    copies = functools.partial(_page_copies, pt_ref, ck_hbm, cv_hbm, kbuf, vbuf, sem)

    @pl.when(step == 0)
    def _():
        for ahead in range(N_SLOTS - 1):
            for cp in copies(ahead, ahead):
                cp.start()

    md = lambda i: mod_ref[i:i + 1, :]
    m = (_sigmoid(gg_ref[...].astype(F32)) * _dot(og_ref[...], wbg_ref[...])
         + _sigmoid(gd_ref[...].astype(F32)) * _dot(od_ref[...], wbd_ref[...]))
    mix = _dot(m.astype(BF16), wo_ref[...])
    x1 = _ln(ALPHA * x_ref[...] + md(2) * mix) * l1g_ref[...] + l1b_ref[...]
    h2b = (_ln(x1) * (1.0 + md(4)) + md(3)).astype(BF16)

    lam = _lam(lq1_ref, lk1_ref, lq2_ref, lk2_ref, lam_init)
    n_here = sps * n_chunks
    ff = D_FF // n_here
    f = None
    for jj in range(n_here):
        k, j = divmod(jj, n_chunks)
        if j == 0:
            qbd = _sample_queries(q_ref[k], dec_seq)
            state = _sample_attn_init(qbd, kn_ref[k], vn_ref[k], bn_ref[...])
        slot = jj % N_SLOTS
        nxt = jj + N_SLOTS - 1
        if nxt < n_here:
            for cp in copies(step * n_here + nxt, nxt % N_SLOTS):
                cp.start()
        else:
            @pl.when(step + 1 < n_steps)
            def _():
                for cp in copies(step * n_here + nxt, nxt % N_SLOTS):
                    cp.start()
        for cp in copies(step * n_here + jj, slot):
            cp.wait()
        s = _dot(qbd, kbuf[slot].astype(BF16)) + (bl_ref[...] if j == n_chunks - 1 else far_ref[:, 0:1])
        part = _mlp_block(h2b, wup_ref, bup_ref, wdn_ref, slice(jj * ff, (jj + 1) * ff))
        f = part if f is None else f + part
        state = _sample_attn_update(state, s, vbuf, slot)
        if j == n_chunks - 1:
            o_ref[k] = _sample_attn_finish(state, lam, g_ref[...], lam_init, dec_seq)

    y_ref[...] = _ln(ALPHA * x1 + md(5) * (f + bdn_ref[...])) * l2g_ref[...] + l2b_ref[...]


def _post_attn(x, mods, og, od, gg, gd, wts, tm, page_table, lam_w, diff_g, bias_last, bias_new, far, q8, kn8, vn8,
               cache_k, cache_v, lam_init, dec_seq):
    nb, t, _ = x.shape
    n_seq, n_pages = page_table.shape
    n_chunks = n_pages // PAGES_PER_STEP
    steps = t // tm
    sps = n_seq // (nb * steps)
    assert nb * steps * sps == n_seq and (sps * n_chunks) % N_SLOTS == 0 and D_FF % (sps * n_chunks) == 0
    row = lambda n: pl.BlockSpec((None, tm, n), lambda b, i, pt: (b, i, 0))
    const = lambda shape: pl.BlockSpec(shape, lambda b, i, pt: (0,) * len(shape), pipeline_mode=pl.Buffered(1))
    per_seq = pl.BlockSpec((sps, SUBLANES, DIFF_QK), lambda b, i, pt: (b * steps + i, 0, 0))
    grid_spec = pltpu.PrefetchScalarGridSpec(
        num_scalar_prefetch=1,
        grid=(nb, steps),
        in_specs=[row(D_MODEL), pl.BlockSpec((None, 6, D_MODEL), lambda b, i, pt: (b, 0, 0)),
                  row(GDN_V), row(DIFF_V), row(D_MODEL), row(D_MODEL)]
                 + [const(w.shape) for w in wts]
                 + [const((1, DH_DIFF))] * 4
                 + [const((1, HEAD_W)), const((NROW_S, CHUNK_KEYS)), const((NROW_S, SUBLANES)), const((NROW_S, LANES)),
                    per_seq, per_seq, per_seq,
                    pl.BlockSpec(memory_space=pl.ANY), pl.BlockSpec(memory_space=pl.ANY)],
        out_specs=[row(D_MODEL), per_seq],
        scratch_shapes=[pltpu.VMEM((N_SLOTS, DIFF_QK, CHUNK_KEYS), F32),
                        pltpu.VMEM((N_SLOTS, CHUNK_KEYS * H_DIFF, HEAD_W), F32),
                        pltpu.SemaphoreType.DMA((2, N_SLOTS))],
    )
    return pl.pallas_call(
        functools.partial(_post_attn_body, lam_init=lam_init, n_chunks=n_chunks, dec_seq=dec_seq, sps=sps),
        grid_spec=grid_spec,
        out_shape=[jax.ShapeDtypeStruct((nb, t, D_MODEL), F32), jax.ShapeDtypeStruct((n_seq, SUBLANES, DIFF_V), F32)],
        compiler_params=_cparams(("arbitrary", "arbitrary")),
        name="post_attn",
    )(page_table.reshape(-1), x, mods, og, od, gg, gd, *wts, *lam_w, diff_g, bias_last, bias_new, far, q8, kn8, vn8,
      cache_k, cache_v)


def kernel(x_prompt, x_sample, c_prompt, c_sample, cache_k, cache_v, page_table, state_conv, state_gdn, rel_bias,
           w_in, w_conv, a_log, dt_bias, gdn_norm_g, lam_q1, lam_k1, lam_q2, lam_k2, diff_norm_g, w_br_gdn,
           w_br_diff, w_o, w_ada, b_ada, ln1_g, ln1_b, ln2_g, ln2_b, w_up, b_up, w_down, b_down):
    assert DEPTH == 1 and w_in.shape[0] == 1
    l = 0
    lam_init = 0.8 - 0.6 * math.exp(-0.3 * l)
    nbp, seq, _ = x_prompt.shape
    nbs, dec_seq, _ = x_sample.shape
    n_tok_s = nbs * dec_seq
    past_len = page_table.shape[1] * PAGE_SIZE
    assert seq % Q_TILE == 0 and CONV_W - 1 <= dec_seq and 2 * dec_seq == SUBLANES
    assert page_table.shape[1] % PAGES_PER_STEP == 0 and nbs % (8 * SEQ_PER_ITER) == 0
    assert Q_TILE >= MAX_DISTANCE and PAGES_PER_STEP * PAGE_SIZE >= MAX_DISTANCE + dec_seq

    offs = np.concatenate([[0], np.cumsum(IN_SPLITS)])
    col = lambda i: w_in[l][:, int(offs[i]):int(offs[i + 1])]
    w_main = jnp.concatenate([col(0), col(1), col(4), col(5), col(6), col(7), col(8)], axis=1).astype(BF16)
    w_small = jnp.concatenate([col(2), col(3), jnp.zeros((D_MODEL, LANES - 2 * H_GDN), F32)], axis=1).astype(BF16)
    pad_gate = lambda a: jnp.zeros((1, LANES), F32).at[0, H_GDN:2 * H_GDN].set(a)
    alog_pad, dtb_pad = pad_gate(a_log[l]), pad_gate(dt_bias[l])
    gng = gdn_norm_g[l].reshape(1, DV_GDN)
    diff_g = diff_norm_g[l].reshape(1, HEAD_W)
    lam_w = (lam_q1[l].reshape(1, -1), lam_k1[l].reshape(1, -1), lam_q2[l].reshape(1, -1), lam_k2[l].reshape(1, -1))
    r2 = lambda a: a.reshape(1, -1)
    post_w = (w_br_gdn[l].astype(BF16), w_br_diff[l].astype(BF16), w_o[l].astype(BF16), r2(ln1_g[l]), r2(ln1_b[l]),
              w_up[l].astype(BF16), r2(b_up[l]), w_down[l].astype(BF16), r2(b_down[l]), r2(ln2_g[l]), r2(ln2_b[l]))

    ada = _ada(jnp.concatenate([jnp.repeat(c_sample, dec_seq, axis=0), c_prompt], axis=0), w_ada[l].astype(BF16),
               r2(b_ada[l]))
    mods_p = ada[n_tok_s:].reshape(nbp, 6, D_MODEL)
    mods_s = ada
    bias_p, bias_last, bias_new, far = _bias_prep(rel_bias, past_len, dec_seq)

    qkv_p, z_p, q_p, k_p, v_p, gg_p, gd_p, ba_p, tail_p = _inproj(x_prompt, mods_p, w_main, w_small, 512, w_conv[l])
    og_p, s_p = _gdn_prompt(qkv_p, z_p, ba_p, alog_pad, dtb_pad, gng, 8 * CHUNK)
    od_p = _attn_prompt(lam_w, diff_g, bias_p, q_p, k_p, v_p, lam_init)

    xs = x_sample.reshape(1, n_tok_s, D_MODEL)
    conv_s, z_s, q_s, k_s, v_s, gg_s, gd_s, ba_s = _inproj(xs, mods_s, w_main, w_small, 256)
    per_seq = lambda a: a.reshape(nbs, dec_seq, a.shape[-1])
    pad8 = lambda a: jnp.pad(a.astype(F32), ((0, 0), (0, SUBLANES - dec_seq), (0, 0)))
    conv_s3 = per_seq(conv_s)
    upx = jnp.concatenate([jnp.zeros((nbs, SUBLANES - (CONV_W - 1), GDN_CONV_CH), F32), state_conv[l], conv_s3,
                           jnp.zeros((nbs, SUBLANES - dec_seq, GDN_CONV_CH), F32)], axis=1)
    og_s8, s_s = _gdn_sample(upx, pad8(per_seq(z_s)), pad8(per_seq(ba_s)), state_gdn[l], w_conv[l], alog_pad,
                             dtb_pad, gng, dec_seq, 8)
    ck = jnp.transpose(cache_k[l], (0, 2, 3, 4, 1)).reshape(cache_k.shape[1], DIFF_QK, PAGE_SIZE)
    cv = cache_v[l].reshape(cache_v.shape[1], PAGE_SIZE * H_DIFF, HEAD_W)
    q_s3 = per_seq(q_s).astype(F32)
    y_p, od_s8 = _post_attn(x_prompt, mods_p, og_p, od_p, gg_p, gd_p, post_w, 256, page_table, lam_w,
                            diff_g, bias_last, bias_new, far, jnp.concatenate([q_s3, q_s3], axis=1),
                            pad8(per_seq(k_s)), pad8(v_s.reshape(nbs, dec_seq, DIFF_V)), ck, cv, lam_init, dec_seq)
    flat = lambda a8: a8[:, :dec_seq].reshape(1, n_tok_s, a8.shape[-1]).astype(BF16)
    y_s = _post(xs, mods_s, flat(og_s8), flat(od_s8), gg_s, gd_s, post_w, 256, True)

    hd = lambda a, b: a.reshape(1, b, -1, H_DIFF, 2, DH_DIFF)
    hv = lambda a, b: a.reshape(1, b, -1, H_DIFF, 2 * DH_DIFF)
    return (y_p, y_s.reshape(nbs, dec_seq, D_MODEL),
            hd(k_p, nbp), hv(v_p, nbp), tail_p[:, SUBLANES - (CONV_W - 1):][None], s_p[None],
            hd(k_s, nbs), hv(v_s, nbs), conv_s3[:, dec_seq - (CONV_W - 1):][None], s_s[None])
```
